```python
import jax, jax.numpy as jnp
from jax import lax
import numpy as np

D_MODEL = 1024
BATCH = 2
SEQ = 8192
DEPTH = 2

HEAD_DIM = 64
BLOCK_Q = 128
ROPE_THETA = 10000.0
RMS_EPS = 1e-6
NEG_INF = -1e30
D_FF = 2816

NSA_HEADS = 8
NSA_KV_HEADS = 2
CMP_BLOCK = 32
CMP_STRIDE = 16
CMP_HIDDEN = 256
SLC_BLOCK = 64
N_SELECT = 16
NSA_WINDOW = 512
FORCE_SCORE = 1e9
SWA_HEADS = 8
SWA_KV_HEADS = 2
SWA_WINDOW = 128
FOX_HEADS = 8
MLA_HEADS = 8
MLA_Q_RANK = 256
MLA_KV_RANK = 128
MLA_NOPE_DIM = 64
MLA_ROPE_DIM = 32
MLA_V_DIM = 64

NSA_Q_COLS = NSA_HEADS * HEAD_DIM
NSA_KV_COLS = NSA_KV_HEADS * HEAD_DIM
NSA_GATE_COLS = 3 * NSA_HEADS
SWA_Q_COLS = SWA_HEADS * HEAD_DIM
SWA_KV_COLS = SWA_KV_HEADS * HEAD_DIM
EVEN_SPLITS = (NSA_Q_COLS,) + (NSA_KV_COLS,) * 6 + (NSA_GATE_COLS, SWA_Q_COLS, SWA_KV_COLS, SWA_KV_COLS)
EVEN_IN_COLS = sum(EVEN_SPLITS)
EVEN_OUT_COLS = NSA_Q_COLS + SWA_Q_COLS
FOX_COLS = FOX_HEADS * HEAD_DIM
ODD_SPLITS = (FOX_COLS, FOX_COLS, FOX_COLS, FOX_HEADS, MLA_Q_RANK, MLA_KV_RANK, MLA_ROPE_DIM)
ODD_IN_COLS = sum(ODD_SPLITS)
ODD_OUT_COLS = FOX_COLS + MLA_HEADS * MLA_V_DIM

kernel_name = 'hybrid_nsa_swa_fox_mla_macaron'


def rms_norm(x, g):
    xf = x.astype(jnp.float32)
    y = xf * lax.rsqrt(jnp.mean(xf * xf, axis=-1, keepdims=True) + RMS_EPS)
    return (y * g.astype(jnp.float32)).astype(x.dtype)


def rope_tables(seq, dim):
    inv_freq = 1.0 / (ROPE_THETA ** (jnp.arange(0, dim, 2, dtype=jnp.float32) / dim))
    ang = jnp.arange(seq, dtype=jnp.float32)[:, None] * inv_freq[None, :]
    return jnp.cos(ang), jnp.sin(ang)


def apply_rope(t, cos, sin):
    half = t.shape[-1] // 2
    t1, t2 = t[..., :half], t[..., half:]
    c = cos.astype(t.dtype)
    s = sin.astype(t.dtype)
    return jnp.concatenate([t1 * c - t2 * s, t2 * c + t1 * s], axis=-1)


def rope_tail(t, cos, sin):
    return jnp.concatenate([t[..., :-MLA_ROPE_DIM], apply_rope(t[..., -MLA_ROPE_DIM:], cos, sin)], axis=-1)


def masked_softmax(logits, mask):
    logits = jnp.where(mask, logits, NEG_INF)
    m = jnp.max(logits, axis=-1, keepdims=True)
    e = jnp.where(mask, jnp.exp(logits - m), 0.0)
    return e / jnp.maximum(jnp.sum(e, axis=-1, keepdims=True), 1e-30)


def swiglu(h, w_gate, w_up, w_down):
    return (jax.nn.silu(h @ w_gate) * (h @ w_up)) @ w_down


def split_cols(t, sizes):
    return jnp.split(t, np.cumsum(sizes)[:-1].tolist(), axis=-1)


def to_heads(t, n):
    B, S, C = t.shape
    return t.reshape(B, S, n, C // n).transpose(0, 2, 1, 3)


def merge_heads(t):
    B, H, S, D = t.shape
    return t.transpose(0, 2, 1, 3).reshape(B, S, H * D)


def banded_attention(q, k, v, window, sinks=None):
    B, Hk, G, S, D = q.shape
    Q = BLOCK_Q
    nb = -(-window // Q)
    nq = S // Q
    pad = ((0, 0), (0, 0), (nb * Q, 0), (0, 0))
    kp = jnp.pad(k, pad).reshape(B, Hk, nq + nb, Q, D)
    vp = jnp.pad(v, pad).reshape(B, Hk, nq + nb, Q, v.shape[-1])
    kb = jnp.concatenate([kp[:, :, j:j + nq] for j in range(nb + 1)], axis=3)
    vb = jnp.concatenate([vp[:, :, j:j + nq] for j in range(nb + 1)], axis=3)
    qb = q.reshape(B, Hk, G, nq, Q, D)
    logits = jnp.einsum('bhgnqd,bhnkd->bhgnqk', qb, kb).astype(jnp.float32) * (D ** -0.5)
    blk = jnp.arange(nq)[:, None, None] * Q
    t = blk + jnp.arange(Q)[None, :, None]
    s = blk - nb * Q + jnp.arange((nb + 1) * Q)[None, None, :]
    mask = (s >= 0) & (s <= t) & (t - s < window)
    logits = jnp.where(mask, logits, NEG_INF)
    m = jnp.max(logits, axis=-1, keepdims=True)
    if sinks is not None:
        sk = sinks.astype(jnp.float32).reshape(1, Hk, G, 1, 1, 1)
        m = jnp.maximum(m, sk)
    e = jnp.where(mask, jnp.exp(logits - m), 0.0)
    denom = jnp.sum(e, axis=-1, keepdims=True)
    if sinks is not None:
        denom = denom + jnp.exp(sk - m)
    p = (e / denom).astype(v.dtype)
    o = jnp.einsum('bhgnqk,bhnkd->bhgnqd', p, vb)
    return o.reshape(B, Hk, G, S, v.shape[-1])


def compress_blocks(t, pos_emb, w1, w2):
    B, Hk, S, D = t.shape
    ch = t.reshape(B, Hk, S // CMP_STRIDE, CMP_STRIDE, D)
    blocks = jnp.concatenate([ch[:, :, :-1], ch[:, :, 1:]], axis=3) + pos_emb
    flat = blocks.reshape(B, Hk, blocks.shape[2], CMP_BLOCK * D)
    return jax.nn.silu(flat @ w1) @ w2


def nsa_compressed_selected(q_cmp, q_slc, k_cmp, v_cmp, k_slc, v_slc):
    B, Hk, G, S, D = q_cmp.shape
    Q = BLOCK_Q
    nq = S // Q
    nc = k_cmp.shape[2]
    ns = S // SLC_BLOCK
    n_sel = min(N_SELECT, ns)
    scale = D ** -0.5
    c_start = jnp.arange(nc)[:, None] * CMP_STRIDE
    s_start = jnp.arange(ns)[None, :] * SLC_BLOCK
    overlap = jnp.maximum(jnp.minimum(c_start + CMP_BLOCK, s_start + SLC_BLOCK) - jnp.maximum(c_start, s_start), 0).astype(jnp.float32)
    cmp_end = jnp.arange(nc) * CMP_STRIDE + CMP_BLOCK - 1
    k_blocks = k_slc.reshape(B, Hk, ns, SLC_BLOCK * D)
    v_blocks = v_slc.reshape(B, Hk, ns, SLC_BLOCK * D)
    b_idx = jnp.arange(B)[:, None, None]
    h_idx = jnp.arange(Hk)[None, :, None]
    blk_ids = jnp.arange(ns)
    n_keys = n_sel * SLC_BLOCK

    def one_block(args):
        n, qc, qs = args
        t = n * Q + jnp.arange(Q)
        lc = jnp.einsum('bhgqd,bhcd->bhgqc', qc, k_cmp).astype(jnp.float32) * scale
        pc = masked_softmax(lc, cmp_end[None, :] <= t[:, None])
        o_c = jnp.einsum('bhgqc,bhcd->bhgqd', pc.astype(v_cmp.dtype), v_cmp)
        imp = jnp.einsum('bhgqc,cs->bhqs', pc, overlap)
        cur = (t // SLC_BLOCK)[:, None]
        forced = (blk_ids == 0) | (blk_ids == cur) | (blk_ids == cur - 1)
        score = jnp.where(forced, FORCE_SCORE, jnp.where(blk_ids <= cur, imp, NEG_INF))
        _, idx = lax.top_k(score, n_sel)
        flat = idx.reshape(B, Hk, Q * n_sel)
        ks = k_blocks[b_idx, h_idx, flat].reshape(B, Hk, Q, n_keys, D)
        vs = v_blocks[b_idx, h_idx, flat].reshape(B, Hk, Q, n_keys, D)
        kpos = (idx[..., None] * SLC_BLOCK + jnp.arange(SLC_BLOCK)).reshape(B, Hk, Q, n_keys)
        ls = jnp.einsum('bhgqd,bhqkd->bhgqk', qs, ks).astype(jnp.float32) * scale
        ps = masked_softmax(ls, (kpos <= t[:, None])[:, :, None])
        o_s = jnp.einsum('bhgqk,bhqkd->bhgqd', ps.astype(vs.dtype), vs)
        return o_c, o_s

    def to_blocks(a):
        return jnp.moveaxis(a.reshape(B, Hk, G, nq, Q, D), 3, 0)

    def from_blocks(a):
        return jnp.moveaxis(a, 0, 3).reshape(B, Hk, G, S, D)

    o_c, o_s = lax.map(one_block, (jnp.arange(nq), to_blocks(q_cmp), to_blocks(q_slc)))
    return from_blocks(o_c), from_blocks(o_s)


def causal_block_attention(q, k, v, decay_cum=None):
    B, H, S, Dk = q.shape
    Q = BLOCK_Q
    nq = S // Q
    scale = Dk ** -0.5
    key_pos = jnp.arange(S)
    qb = jnp.moveaxis(q.reshape(B, H, nq, Q, Dk), 2, 0)
    if decay_cum is None:
        xs = (jnp.arange(nq), qb)
    else:
        xs = (jnp.arange(nq), qb, jnp.moveaxis(decay_cum.reshape(B, H, nq, Q), 2, 0))

    def one_block(args):
        n, qi = args[0], args[1]
        t = n * Q + jnp.arange(Q)
        logits = jnp.einsum('bhqd,bhkd->bhqk', qi, k).astype(jnp.float32) * scale
        if decay_cum is not None:
            logits = logits + args[2][..., :, None] - decay_cum[:, :, None, :]
        p = masked_softmax(logits, key_pos[None, :] <= t[:, None])
        return jnp.einsum('bhqk,bhkd->bhqd', p.astype(v.dtype), v)

    o = lax.map(one_block, xs)
    return jnp.moveaxis(o, 0, 2).reshape(B, H, S, v.shape[-1])


def nsa_swa_mixer(h, w_in, nsa_gate_b, nsa_q_norm, nsa_kc_norm, nsa_ks_norm, nsa_kw_norm,
                  cmp_pos_k, cmp_pos_v, cmpk_w1, cmpk_w2, cmpv_w1, cmpv_w2,
                  swa_q_norm, swa_k_norm, swa_sinks, w_out):
    B, S, _ = h.shape
    (q_a, kc, vc, ks, vs, kw, vw, gate_logits, q_b, k_b, v_b) = split_cols(h @ w_in, EVEN_SPLITS)
    cos, sin = rope_tables(S, HEAD_DIM)
    ga = NSA_HEADS // NSA_KV_HEADS
    gb = SWA_HEADS // SWA_KV_HEADS
    qa = rms_norm(to_heads(q_a, NSA_HEADS), nsa_q_norm)
    qa_rot = apply_rope(qa, cos, sin)

    def grp(t):
        return t.reshape(B, NSA_KV_HEADS, ga, S, HEAD_DIM)

    k_cmp = rms_norm(compress_blocks(to_heads(kc, NSA_KV_HEADS), cmp_pos_k, cmpk_w1, cmpk_w2), nsa_kc_norm)
    v_cmp = compress_blocks(to_heads(vc, NSA_KV_HEADS), cmp_pos_v, cmpv_w1, cmpv_w2)
    k_slc = apply_rope(rms_norm(to_heads(ks, NSA_KV_HEADS), nsa_ks_norm), cos, sin)
    v_slc = to_heads(vs, NSA_KV_HEADS)
    k_win = apply_rope(rms_norm(to_heads(kw, NSA_KV_HEADS), nsa_kw_norm), cos, sin)
    v_win = to_heads(vw, NSA_KV_HEADS)
    o_cmp, o_slc = nsa_compressed_selected(grp(qa), grp(qa_rot), k_cmp, v_cmp, k_slc, v_slc)
    o_win = banded_attention(grp(qa_rot), k_win, v_win, NSA_WINDOW)
    gates = jax.nn.sigmoid(gate_logits + nsa_gate_b).reshape(B, S, NSA_HEADS, 3).transpose(0, 2, 1, 3)
    gates = gates.reshape(B, NSA_KV_HEADS, ga, S, 3)
    o_a = gates[..., 0:1] * o_cmp + gates[..., 1:2] * o_slc + gates[..., 2:3] * o_win
    o_a = o_a.reshape(B, NSA_HEADS, S, HEAD_DIM)
    qb = apply_rope(rms_norm(to_heads(q_b, SWA_HEADS), swa_q_norm), cos, sin)
    kb = apply_rope(rms_norm(to_heads(k_b, SWA_KV_HEADS), swa_k_norm), cos, sin)
    vb = to_heads(v_b, SWA_KV_HEADS)
    o_b = banded_attention(qb.reshape(B, SWA_KV_HEADS, gb, S, HEAD_DIM), kb, vb, SWA_WINDOW,
                           swa_sinks.reshape(SWA_KV_HEADS, gb))
    o_b = o_b.reshape(B, SWA_HEADS, S, HEAD_DIM)
    o = jnp.concatenate([merge_heads(o_a), merge_heads(o_b)], axis=-1)
    return o @ w_out


def fox_mla_mixer(h, w_in, fox_f_bias, fox_q_norm, fox_k_norm, mla_q_a_norm, mla_w_q_b,
                  mla_kv_a_norm, mla_w_kv_b, mla_q_norm, mla_k_norm, w_out):
    B, S, _ = h.shape
    q_c, k_c, v_c, f_c, c_q, c_kv, k_r = split_cols(h @ w_in, ODD_SPLITS)
    qf = rms_norm(to_heads(q_c, FOX_HEADS), fox_q_norm)
    kf = rms_norm(to_heads(k_c, FOX_HEADS), fox_k_norm)
    vf = to_heads(v_c, FOX_HEADS)
    log_f = jax.nn.log_sigmoid((f_c + fox_f_bias).astype(jnp.float32))
    decay_cum = jnp.cumsum(log_f, axis=1).transpose(0, 2, 1)
    o_fox = causal_block_attention(qf, kf, vf, decay_cum)
    qk_dim = MLA_NOPE_DIM + MLA_ROPE_DIM
    q = (rms_norm(c_q, mla_q_a_norm) @ mla_w_q_b).reshape(B, S, MLA_HEADS, qk_dim)
    kv = (rms_norm(c_kv, mla_kv_a_norm) @ mla_w_kv_b).reshape(B, S, MLA_HEADS, MLA_NOPE_DIM + MLA_V_DIM)
    k_nope, v_m = kv[..., :MLA_NOPE_DIM], kv[..., MLA_NOPE_DIM:]
    k_rope = jnp.broadcast_to(k_r[:, :, None, :], (B, S, MLA_HEADS, MLA_ROPE_DIM))
    k = jnp.concatenate([k_nope, k_rope], axis=-1)
    cos, sin = rope_tables(S, MLA_ROPE_DIM)
    q = rope_tail(rms_norm(q, mla_q_norm).transpose(0, 2, 1, 3), cos, sin)
    k = rope_tail(rms_norm(k, mla_k_norm).transpose(0, 2, 1, 3), cos, sin)
    o_mla = causal_block_attention(q, k, v_m.transpose(0, 2, 1, 3))
    o = jnp.concatenate([merge_heads(o_fox), merge_heads(o_mla)], axis=-1)
    return o @ w_out


def setup_inputs(seed: int = 0) -> dict:
    key = jax.random.key(seed)
    D, F, hd = D_MODEL, D_FF, HEAD_DIM

    def ffn(p):
        return [(p + 'norm', (D,), 'gain'), (p + 'w_gate', (D, F), 'w'),
                (p + 'w_up', (D, F), 'w'), (p + 'w_down', (F, D), 'w')]

    specs = [('x', (BATCH, SEQ, D), 'act')]
    specs += ffn('l0_ffn1_')
    specs += [('l0_mix_norm', (D,), 'gain'), ('l0_w_in', (D, EVEN_IN_COLS), 'w'),
              ('l0_nsa_gate_b', (NSA_GATE_COLS,), 'bias'),
              ('l0_nsa_q_norm', (hd,), 'gain'), ('l0_nsa_kc_norm', (hd,), 'gain'),
              ('l0_nsa_ks_norm', (hd,), 'gain'), ('l0_nsa_kw_norm', (hd,), 'gain'),
              ('l0_cmp_pos_k', (CMP_BLOCK, hd), 'bias'), ('l0_cmp_pos_v', (CMP_BLOCK, hd), 'bias'),
              ('l0_cmpk_w1', (CMP_BLOCK * hd, CMP_HIDDEN), 'w'), ('l0_cmpk_w2', (CMP_HIDDEN, hd), 'w'),
              ('l0_cmpv_w1', (CMP_BLOCK * hd, CMP_HIDDEN), 'w'), ('l0_cmpv_w2', (CMP_HIDDEN, hd), 'w'),
              ('l0_swa_q_norm', (hd,), 'gain'), ('l0_swa_k_norm', (hd,), 'gain'),
              ('l0_swa_sinks', (SWA_HEADS,), 'sink'), ('l0_w_out', (EVEN_OUT_COLS, D), 'w')]
    specs += ffn('l0_ffn2_')
    specs += ffn('l1_ffn1_')
    specs += [('l1_mix_norm', (D,), 'gain'), ('l1_w_in', (D, ODD_IN_COLS), 'w'),
              ('l1_fox_f_bias', (FOX_HEADS,), 'forget'),
              ('l1_fox_q_norm', (hd,), 'gain'), ('l1_fox_k_norm', (hd,), 'gain'),
              ('l1_mla_q_a_norm', (MLA_Q_RANK,), 'gain'),
              ('l1_mla_w_q_b', (MLA_Q_RANK, MLA_HEADS * (MLA_NOPE_DIM + MLA_ROPE_DIM)), 'w'),
              ('l1_mla_kv_a_norm', (MLA_KV_RANK,), 'gain'),
              ('l1_mla_w_kv_b', (MLA_KV_RANK, MLA_HEADS * (MLA_NOPE_DIM + MLA_V_DIM)), 'w'),
              ('l1_mla_q_norm', (MLA_NOPE_DIM + MLA_ROPE_DIM,), 'gain'),
              ('l1_mla_k_norm', (MLA_NOPE_DIM + MLA_ROPE_DIM,), 'gain'),
              ('l1_w_out', (ODD_OUT_COLS, D), 'w')]
    specs += ffn('l1_ffn2_')
    keys = jax.random.split(key, len(specs))
    out = {}
    for k, (name, shape, kind) in zip(keys, specs):
        if kind == 'act':
            val = jax.random.normal(k, shape, jnp.float32)
        elif kind == 'w':
            val = jax.random.normal(k, shape, jnp.float32) * (shape[0] ** -0.5)
        elif kind == 'gain':
            val = 1.0 + 0.02 * jax.random.normal(k, shape, jnp.float32)
        elif kind == 'bias':
            val = 0.02 * jax.random.normal(k, shape, jnp.float32)
        elif kind == 'sink':
            val = 0.5 * jax.random.normal(k, shape, jnp.float32)
        else:
            val = jax.random.uniform(k, shape, jnp.float32, minval=1.0, maxval=4.0)
        out[name] = val
    return out


def reference(x,
              l0_ffn1_norm, l0_ffn1_w_gate, l0_ffn1_w_up, l0_ffn1_w_down,
              l0_mix_norm, l0_w_in, l0_nsa_gate_b,
              l0_nsa_q_norm, l0_nsa_kc_norm, l0_nsa_ks_norm, l0_nsa_kw_norm,
              l0_cmp_pos_k, l0_cmp_pos_v, l0_cmpk_w1, l0_cmpk_w2, l0_cmpv_w1, l0_cmpv_w2,
              l0_swa_q_norm, l0_swa_k_norm, l0_swa_sinks, l0_w_out,
              l0_ffn2_norm, l0_ffn2_w_gate, l0_ffn2_w_up, l0_ffn2_w_down,
              l1_ffn1_norm, l1_ffn1_w_gate, l1_ffn1_w_up, l1_ffn1_w_down,
              l1_mix_norm, l1_w_in, l1_fox_f_bias, l1_fox_q_norm, l1_fox_k_norm,
              l1_mla_q_a_norm, l1_mla_w_q_b, l1_mla_kv_a_norm, l1_mla_w_kv_b,
              l1_mla_q_norm, l1_mla_k_norm, l1_w_out,
              l1_ffn2_norm, l1_ffn2_w_gate, l1_ffn2_w_up, l1_ffn2_w_down):
    ffn1 = [(l0_ffn1_norm, l0_ffn1_w_gate, l0_ffn1_w_up, l0_ffn1_w_down),
            (l1_ffn1_norm, l1_ffn1_w_gate, l1_ffn1_w_up, l1_ffn1_w_down)]
    ffn2 = [(l0_ffn2_norm, l0_ffn2_w_gate, l0_ffn2_w_up, l0_ffn2_w_down),
            (l1_ffn2_norm, l1_ffn2_w_gate, l1_ffn2_w_up, l1_ffn2_w_down)]
    mix_norms = [l0_mix_norm, l1_mix_norm]
    mix_params = [
        (l0_w_in, l0_nsa_gate_b, l0_nsa_q_norm, l0_nsa_kc_norm, l0_nsa_ks_norm, l0_nsa_kw_norm,
         l0_cmp_pos_k, l0_cmp_pos_v, l0_cmpk_w1, l0_cmpk_w2, l0_cmpv_w1, l0_cmpv_w2,
         l0_swa_q_norm, l0_swa_k_norm, l0_swa_sinks, l0_w_out),
        (l1_w_in, l1_fox_f_bias, l1_fox_q_norm, l1_fox_k_norm, l1_mla_q_a_norm, l1_mla_w_q_b,
         l1_mla_kv_a_norm, l1_mla_w_kv_b, l1_mla_q_norm, l1_mla_k_norm, l1_w_out)]
    mix_fns = (nsa_swa_mixer, fox_mla_mixer)
    for l in range(DEPTH):
        n1, g1, u1, d1 = ffn1[l]
        x = x + 0.5 * swiglu(rms_norm(x, n1), g1, u1, d1)
        x = x + mix_fns[l % 2](rms_norm(x, mix_norms[l]), *mix_params[l])
        n2, g2, u2, d2 = ffn2[l]
        x = x + 0.5 * swiglu(rms_norm(x, n2), g2, u2, d2)
    return x
```

```python
import functools
import math

import numpy as np
import jax
import jax.numpy as jnp
from jax import lax
from jax.experimental import pallas as pl
from jax.experimental.pallas import tpu as pltpu

F32 = jnp.float32
BF16 = jnp.bfloat16

HEAD_DIM = 64
LANES = 128
ROPE_THETA = 10000.0
RMS_EPS = 1e-6
NEG_INF = -1e30
FORCE_SCORE = 1e9
BELOW_ALL = -3e38

NSA_HEADS = 8
NSA_KV_HEADS = 2
CMP_BLOCK = 32
CMP_STRIDE = 16
CMP_HIDDEN = 256
SLC_BLOCK = 64
N_SELECT = 16
NSA_WINDOW = 512
SWA_HEADS = 8
SWA_KV_HEADS = 2
SWA_WINDOW = 128
FOX_HEADS = 8
MLA_HEADS = 8
MLA_Q_RANK = 256
MLA_KV_RANK = 128
MLA_NOPE_DIM = 64
MLA_ROPE_DIM = 32
MLA_V_DIM = 64
MLA_QK_DIM = MLA_NOPE_DIM + MLA_ROPE_DIM

VMEM_LIMIT = 48 * 1024 * 1024

NT_DIMS = (((1,), (1,)), ((), ()))


def _params(sem):
    return pltpu.CompilerParams(dimension_semantics=sem, vmem_limit_bytes=VMEM_LIMIT)


def _dot(a, b):
    return jnp.dot(a, b, preferred_element_type=F32)


def _dot_nt(a, b):
    return lax.dot_general(a, b, NT_DIMS, preferred_element_type=F32)


def _rms(x, gain, n):
    ms = jnp.sum(x * x, axis=-1, keepdims=True) * (1.0 / n)
    return (x * lax.rsqrt(ms + RMS_EPS)) * gain


def _lane(shape):
    return lax.broadcasted_iota(jnp.int32, shape, len(shape) - 1)


def _ffn_kernel(x_ref, g_ref, wg_ref, wu_ref, wd_ref, o_ref, h_sc, acc_sc, *, nf):
    x = x_ref[...]
    h_sc[...] = _rms(x, g_ref[...], x.shape[-1]).astype(BF16)
    acc_sc[...] = jnp.zeros_like(acc_sc)

    def body(c, carry):
        h = h_sc[...]
        g = _dot(h, wg_ref[c])
        u = _dot(h, wu_ref[c])
        a = (g * (1.0 / (1.0 + jnp.exp(-g)))) * u
        acc_sc[...] += _dot(a.astype(BF16), wd_ref[c])
        return carry

    lax.fori_loop(0, nf, body, 0)
    o_ref[...] = x + 0.5 * acc_sc[...]


def _ffn(x, norm, w_gate, w_up, w_down, *, tm=512, tf=256):
    n, d = x.shape
    f = w_gate.shape[1]
    nf = f // tf
    wg = w_gate.astype(BF16).reshape(d, nf, tf).transpose(1, 0, 2)
    wu = w_up.astype(BF16).reshape(d, nf, tf).transpose(1, 0, 2)
    wd = w_down.astype(BF16).reshape(nf, tf, d)
    wspec = lambda shp: pl.BlockSpec(shp, lambda i: (0, 0, 0), pipeline_mode=pl.Buffered(1))
    return pl.pallas_call(
        functools.partial(_ffn_kernel, nf=nf),
        grid=(n // tm,),
        in_specs=[pl.BlockSpec((tm, d), lambda i: (i, 0)),
                  pl.BlockSpec((1, d), lambda i: (0, 0)),
                  wspec((nf, d, tf)), wspec((nf, d, tf)), wspec((nf, tf, d))],
        out_specs=pl.BlockSpec((tm, d), lambda i: (i, 0)),
        out_shape=jax.ShapeDtypeStruct((n, d), F32),
        scratch_shapes=[pltpu.VMEM((tm, d), BF16), pltpu.VMEM((tm, d), F32)],
        compiler_params=_params(("arbitrary",)),
        name="ffn",
    )(x, norm.reshape(1, d), wg, wu, wd)


def _rms_matmul_kernel(x_ref, g_ref, w_ref, o_ref):
    x = x_ref[...]
    h = _rms(x, g_ref[...], x.shape[-1]).astype(BF16)
    o_ref[...] = _dot(h, w_ref[...])


def _rms_matmul(x, norm, w, *, tm=512):
    n, d = x.shape
    c = w.shape[1]
    return pl.pallas_call(
        _rms_matmul_kernel,
        grid=(n // tm,),
        in_specs=[pl.BlockSpec((tm, d), lambda i: (i, 0)),
                  pl.BlockSpec((1, d), lambda i: (0, 0)),
                  pl.BlockSpec((d, c), lambda i: (0, 0), pipeline_mode=pl.Buffered(1))],
        out_specs=pl.BlockSpec((tm, c), lambda i: (i, 0)),
        out_shape=jax.ShapeDtypeStruct((n, c), F32),
        compiler_params=_params(("arbitrary",)),
        name="rms_matmul",
    )(x, norm.reshape(1, d), w)


def _outproj_kernel(*refs, n_a, n_b):
    x_ref = refs[0]
    a_refs = refs[1:1 + n_a]
    b_refs = refs[1 + n_a:1 + n_a + n_b]
    wa_ref, wb_ref, o_ref = refs[1 + n_a + n_b:]
    a = a_refs[0][...]
    for r in a_refs[1:]:
        a = a + r[...]
    b = b_refs[0][...]
    for r in b_refs[1:]:
        b = b + r[...]
    o_ref[...] = x_ref[...] + _dot(a.astype(BF16), wa_ref[...]) + _dot(b.astype(BF16), wb_ref[...])


def _outproj(x, a_list, b_list, w_out, *, tm=512):
    n, d = x.shape
    ca = a_list[0].shape[1]
    cb = b_list[0].shape[1]
    wa = w_out[:ca].astype(BF16)
    wb = w_out[ca:].astype(BF16)
    row = lambda c: pl.BlockSpec((tm, c), lambda i: (i, 0))
    return pl.pallas_call(
        functools.partial(_outproj_kernel, n_a=len(a_list), n_b=len(b_list)),
        grid=(n // tm,),
        in_specs=[row(d)] + [row(ca)] * len(a_list) + [row(cb)] * len(b_list)
                 + [pl.BlockSpec((ca, d), lambda i: (0, 0)), pl.BlockSpec((cb, d), lambda i: (0, 0))],
        out_specs=row(d),
        out_shape=jax.ShapeDtypeStruct((n, d), F32),
        compiler_params=_params(("arbitrary",)),
        name="outproj",
    )(x, *a_list, *b_list, wa, wb)


def _prep_kernel(dst_ref, y_ref, gain_ref, cos_ref, sin_ref, o_ref, *, norm, rope, scale, expand):
    c = pl.program_id(1)
    x = y_ref[...]
    lane = _lane(x.shape)
    low = lane < HEAD_DIM
    if norm:
        x2 = x * x
        s_lo = jnp.sum(jnp.where(low, x2, 0.0), axis=-1, keepdims=True)
        s_hi = jnp.sum(jnp.where(low, 0.0, x2), axis=-1, keepdims=True)
        ms = jnp.where(low, s_lo, s_hi) * (1.0 / HEAD_DIM)
        x = (x * lax.rsqrt(ms + RMS_EPS)) * gain_ref[0]
    if rope:
        swapped = jnp.where((lane & (HEAD_DIM - 1)) < HEAD_DIM // 2,
                            pltpu.roll(x, LANES - HEAD_DIM // 2, 1), pltpu.roll(x, HEAD_DIM // 2, 1))
        x = x * cos_ref[...] + swapped * sin_ref[...]
    if scale != 1.0:
        x = x * scale
    if expand:
        r = pltpu.roll(x, HEAD_DIM, 1)
        de = jnp.broadcast_to(dst_ref[c, 0], x.shape) == 0
        do = jnp.broadcast_to(dst_ref[c, 1], x.shape) == 0
        zero = jnp.zeros_like(x)
        h_even = jnp.where(de, jnp.where(low, x, zero), jnp.where(low, zero, r))
        h_odd = jnp.where(do, jnp.where(low, r, zero), jnp.where(low, zero, x))
        o_ref[...] = jnp.concatenate([h_even, h_odd], axis=1).astype(o_ref.dtype)
    else:
        o_ref[...] = x.astype(o_ref.dtype)


def _prep(y, blk0, nblk, seq, cos_t, sin_t, *, gains=None, rope=False, scale=1.0, dst=None, tm=512):
    n = y.shape[0]
    norm = gains is not None
    expand = dst is not None
    if gains is None:
        gains = jnp.ones((nblk, 1, LANES), F32)
    if dst is None:
        dst = np.zeros((nblk, 2), np.int32)
    ow = 2 * LANES if expand else LANES
    sblocks = seq // tm
    return pl.pallas_call(
        functools.partial(_prep_kernel, norm=norm, rope=rope, scale=scale, expand=expand),
        grid=(n // tm, nblk),
        in_specs=[pl.BlockSpec(memory_space=pltpu.SMEM),
                  pl.BlockSpec((tm, LANES), lambda i, c: (i, blk0 + c)),
                  pl.BlockSpec((1, 1, LANES), lambda i, c: (c, 0, 0)),
                  pl.BlockSpec((tm, LANES), lambda i, c: (i % sblocks, 0)),
                  pl.BlockSpec((tm, LANES), lambda i, c: (i % sblocks, 0))],
        out_specs=pl.BlockSpec((tm, ow), lambda i, c: (i, c)),
        out_shape=jax.ShapeDtypeStruct((n, nblk * ow), BF16),
        compiler_params=_params(("arbitrary", "arbitrary")),
        name="head_prep",
    )(jnp.asarray(dst, jnp.int32), y, gains, cos_t, sin_t)


def _pair_gain(g):
    return jnp.concatenate([g, g]).reshape(1, LANES).astype(F32)


def _rope_tables(seq):
    half = HEAD_DIM // 2
    inv_freq = 1.0 / (ROPE_THETA ** (jnp.arange(0, HEAD_DIM, 2, dtype=F32) / HEAD_DIM))
    ang = jnp.arange(seq, dtype=F32)[:, None] * inv_freq[None, :]
    cos, sin = jnp.cos(ang), jnp.sin(ang)
    cos_t = jnp.concatenate([cos, cos, cos, cos], axis=1)
    sin_t = jnp.concatenate([-sin, sin, -sin, sin], axis=1)
    del half
    return cos_t, sin_t


def _compress_kernel(ch_ref, ptop_ref, pbot_ref, w1t_ref, w1b_ref, w2_ref, gain_ref, o_ref, *, norm):
    ch = ch_ref[0]
    a = _dot((ch + ptop_ref[...]).astype(BF16), w1t_ref[...])
    b = _dot((ch + pbot_ref[...]).astype(BF16), w1b_ref[...])
    nc = a.shape[0]
    hid = a + pltpu.roll(b, nc - 1, 0)
    act = hid * (1.0 / (1.0 + jnp.exp(-hid)))
    out = _dot(act.astype(BF16), w2_ref[...])
    if norm:
        lane = _lane(out.shape)
        low = lane < HEAD_DIM
        o2 = out * out
        s_lo = jnp.sum(jnp.where(low, o2, 0.0), axis=-1, keepdims=True)
        s_hi = jnp.sum(jnp.where(low, 0.0, o2), axis=-1, keepdims=True)
        ms = jnp.where(low, s_lo, s_hi) * (1.0 / HEAD_DIM)
        out = (out * lax.rsqrt(ms + RMS_EPS)) * gain_ref[...]
    o_ref[0] = out.astype(o_ref.dtype)


def _compress(t_pair, pos_emb, w1, w2, gain):
    b, s, _ = t_pair.shape
    nc = s // CMP_STRIDE
    hid = w1.shape[1]
    ch = t_pair.reshape(b, nc, CMP_STRIDE * LANES)
    eye2 = jnp.eye(2, dtype=F32)
    w1r = w1.reshape(CMP_BLOCK, HEAD_DIM, hid)
    def expand_w1(w):
        return jnp.einsum('pdj,kl->pkdlj', w, eye2).reshape(CMP_STRIDE * LANES, 2 * hid).astype(BF16)
    w1t, w1b = expand_w1(w1r[:CMP_STRIDE]), expand_w1(w1r[CMP_STRIDE:])
    w2e = jnp.einsum('jd,kl->kjld', w2, eye2).reshape(2 * hid, LANES).astype(BF16)
    def expand_pos(p):
        return jnp.broadcast_to(p[:, None, :], (CMP_STRIDE, 2, HEAD_DIM)).reshape(1, CMP_STRIDE * LANES)
    ptop, pbot = expand_pos(pos_emb[:CMP_STRIDE]), expand_pos(pos_emb[CMP_STRIDE:])
    norm = gain is not None
    g = _pair_gain(gain) if norm else jnp.ones((1, LANES), F32)
    full = lambda shp: pl.BlockSpec(shp, lambda i: (0,) * len(shp))
    return pl.pallas_call(
        functools.partial(_compress_kernel, norm=norm),
        grid=(b,),
        in_specs=[pl.BlockSpec((1, nc, CMP_STRIDE * LANES), lambda i: (i, 0, 0)),
                  full((1, CMP_STRIDE * LANES)), full((1, CMP_STRIDE * LANES)),
                  full((CMP_STRIDE * LANES, 2 * hid)), full((CMP_STRIDE * LANES, 2 * hid)),
                  full((2 * hid, LANES)), full((1, LANES))],
        out_specs=pl.BlockSpec((1, nc, LANES), lambda i: (i, 0, 0)),
        out_shape=jax.ShapeDtypeStruct((b, nc, LANES), BF16),
        compiler_params=_params(("arbitrary",)),
        name="nsa_compress",
    )(ch, ptop, pbot, w1t, w1b, w2e, g)


def _stack_heads(q, n):
    return jnp.concatenate([q[:, g * LANES:(g + 1) * LANES] for g in range(n)], axis=0)


def _gate_column(gl, col):
    lane = _lane(gl.shape)
    return jnp.sum(jnp.where(lane == col, gl, 0.0), axis=-1, keepdims=True)


def _sigmoid(x):
    return 1.0 / (1.0 + jnp.exp(-x))


def _compact_group(heads, hk):
    tq = heads[0].shape[0]
    lane = _lane((tq, LANES))
    low = lane < HEAD_DIM
    at_low = jnp.broadcast_to(hk, (tq, LANES)) == 0
    outs = []
    for e in range(0, len(heads), 2):
        he, ho = heads[e], heads[e + 1]
        he_r, ho_r = pltpu.roll(he, HEAD_DIM, 1), pltpu.roll(ho, HEAD_DIM, 1)
        lo_part = jnp.where(at_low, he, he_r)
        hi_part = jnp.where(at_low, ho_r, ho)
        outs.append(jnp.where(low, lo_part, hi_part))
    return jnp.concatenate(outs, axis=1)


def _nsa_cmp_kernel(gb_ref, q_ref, k_ref, v_ref, ov_ref, gl_ref, o_ref, sel_ref, *, tq, group, n_sel, ns):
    hk = pl.program_id(1)
    i = pl.program_id(2)
    q4 = _stack_heads(q_ref[0], group)
    kc = k_ref[0]
    ncp = kc.shape[0]
    logits = _dot_nt(q4, kc).reshape(group, tq, ncp)
    t = i * tq + lax.broadcasted_iota(jnp.int32, (tq, ncp), 0)
    cmp_end = lax.broadcasted_iota(jnp.int32, (tq, ncp), 1) * CMP_STRIDE + (CMP_BLOCK - 1)
    mask = (cmp_end <= t)[None]
    logits = jnp.where(mask, logits, NEG_INF)
    m = jnp.max(logits, axis=-1, keepdims=True)
    e = jnp.where(mask, jnp.exp(logits - m), 0.0)
    p = e / jnp.maximum(jnp.sum(e, axis=-1, keepdims=True), 1e-30)
    o4 = _dot(p.reshape(group * tq, ncp).astype(BF16), v_ref[0])
    gl = gl_ref[0]
    heads = []
    for g in range(group):
        col = (hk * group + g) * 3
        gate = _sigmoid(_gate_column(gl, col) + gb_ref[col])
        heads.append(o4[g * tq:(g + 1) * tq] * gate)
    o_ref[0] = _compact_group(heads, hk)

    ps = jnp.sum(p, axis=0)
    ps_hi = ps.astype(BF16)
    ps_lo = (ps - ps_hi.astype(F32)).astype(BF16)
    imp = _dot(ps_hi, ov_ref[...]) + _dot(ps_lo, ov_ref[...])
    blk = _lane((tq, LANES))
    tcol = i * tq + lax.broadcasted_iota(jnp.int32, (tq, LANES), 0)
    cur = tcol // SLC_BLOCK
    forced = (blk == 0) | (blk == cur) | (blk == cur - 1)
    score = jnp.where(forced, FORCE_SCORE, jnp.where(blk <= cur, imp, NEG_INF))
    score = jnp.where(blk < ns, score, BELOW_ALL)
    blk_f = blk.astype(F32)

    def pick(_, carry):
        sc, sel = carry
        mx = jnp.max(sc, axis=-1, keepdims=True)
        first = jnp.min(jnp.where(sc == mx, blk_f, float(LANES)), axis=-1, keepdims=True)
        hit = blk_f == first
        return jnp.where(hit, BELOW_ALL, sc), jnp.where(hit, 1.0, sel)

    _, sel = lax.fori_loop(0, n_sel, pick, (score, jnp.zeros((tq, LANES), F32)))
    sel_ref[0, 0] = jnp.where(blk <= cur, sel, 0.0).astype(sel_ref.dtype)


def _nsa_compressed(q_cmp, k_cmp, v_cmp, gates, gate_b, *, seq, tq=128):
    b = q_cmp.shape[0]
    group = NSA_HEADS // NSA_KV_HEADS
    ncp = k_cmp.shape[1]
    ns = seq // SLC_BLOCK
    n_sel = min(N_SELECT, ns)
    c_start = np.arange(ncp)[:, None] * CMP_STRIDE
    s_start = np.arange(LANES)[None, :] * SLC_BLOCK
    overlap = np.maximum(np.minimum(c_start + CMP_BLOCK, s_start + SLC_BLOCK) - np.maximum(c_start, s_start), 0)
    overlap = np.where((np.arange(LANES)[None, :] < ns) & (np.arange(ncp)[:, None] < ncp - 1), overlap, 0)
    overlap = jnp.asarray(overlap, BF16)
    gw = group * LANES
    return pl.pallas_call(
        functools.partial(_nsa_cmp_kernel, tq=tq, group=group, n_sel=n_sel, ns=ns),
        grid=(b, NSA_KV_HEADS, seq // tq),
        in_specs=[pl.BlockSpec(memory_space=pltpu.SMEM),
                  pl.BlockSpec((1, tq, gw), lambda bi, h, i: (bi, i, h)),
                  pl.BlockSpec((1, ncp, LANES), lambda bi, h, i: (bi, 0, 0)),
                  pl.BlockSpec((1, ncp, LANES), lambda bi, h, i: (bi, 0, 0)),
                  pl.BlockSpec((ncp, LANES), lambda bi, h, i: (0, 0)),
                  pl.BlockSpec((1, tq, LANES), lambda bi, h, i: (bi, i, 0))],
        out_specs=[pl.BlockSpec((1, tq, group * HEAD_DIM), lambda bi, h, i: (bi, i, h)),
                   pl.BlockSpec((1, 1, tq, LANES), lambda bi, h, i: (bi, h, i, 0))],
        out_shape=[jax.ShapeDtypeStruct((b, seq, NSA_HEADS * HEAD_DIM), F32),
                   jax.ShapeDtypeStruct((b, NSA_KV_HEADS, seq, LANES), BF16)],
        compiler_params=_params(("arbitrary", "arbitrary", "arbitrary")),
        name="nsa_compressed_select",
    )(gate_b, q_cmp, k_cmp, v_cmp, overlap, gates)


def _nsa_slc_kernel(gb_ref, q_ref, k_ref, v_ref, sel_ref, ex_ref, gl_ref, o_ref, *, tq, tk, group):
    hk = pl.program_id(1)
    i = pl.program_id(2)
    q4 = _stack_heads(q_ref[0], group)
    sel = sel_ref[0, 0]
    t = i * tq + lax.broadcasted_iota(jnp.int32, (tq, tk), 0)
    col = lax.broadcasted_iota(jnp.int32, (tq, tk), 1)
    n_tiles = (i * tq + tq - 1) // tk + 1

    def body(j, carry):
        m, l, acc = carry
        off = pl.multiple_of(j * tk, tk)
        kt = k_ref[0, pl.ds(off, tk), :]
        vt = v_ref[0, pl.ds(off, tk), :]
        s = _dot_nt(q4, kt).reshape(group, tq, tk)
        picked = _dot(sel, ex_ref[j])
        valid = (picked > 0.5) & (j * tk + col <= t)
        s = s + jnp.where(valid, 0.0, NEG_INF)[None]
        m_new = jnp.maximum(m, jnp.max(s, axis=-1, keepdims=True))
        p = jnp.exp(s - m_new)
        alpha = jnp.exp(m - m_new)
        l = alpha * l + jnp.sum(p, axis=-1, keepdims=True)
        pv = _dot(p.reshape(group * tq, tk).astype(BF16), vt).reshape(group, tq, LANES)
        return m_new, l, alpha * acc + pv

    init = (jnp.full((group, tq, 1), NEG_INF, F32), jnp.zeros((group, tq, 1), F32),
            jnp.zeros((group, tq, LANES), F32))
    m, l, acc = lax.fori_loop(0, n_tiles, body, init)
    gl = gl_ref[0]
    heads = []
    for g in range(group):
        colg = (hk * group + g) * 3 + 1
        gate = _sigmoid(_gate_column(gl, colg) + gb_ref[colg])
        heads.append(acc[g] * (gate / l[g]))
    o_ref[0] = _compact_group(heads, hk)


def _nsa_selected(q_rot, k_slc, v_slc, sel, gates, gate_b, *, seq, tq=128, tk=512):
    b = q_rot.shape[0]
    group = NSA_HEADS // NSA_KV_HEADS
    tk = min(tk, seq)
    nk = seq // tk
    key_blk = (np.arange(nk)[:, None, None] * tk + np.arange(tk)[None, None, :]) // SLC_BLOCK
    expand = jnp.asarray(key_blk == np.arange(LANES)[None, :, None], BF16)
    gw = group * LANES
    return pl.pallas_call(
        functools.partial(_nsa_slc_kernel, tq=tq, tk=tk, group=group),
        grid=(b, NSA_KV_HEADS, seq // tq),
        in_specs=[pl.BlockSpec(memory_space=pltpu.SMEM),
                  pl.BlockSpec((1, tq, gw), lambda bi, h, i: (bi, i, h)),
                  pl.BlockSpec((1, seq, LANES), lambda bi, h, i: (bi, 0, 0)),
                  pl.BlockSpec((1, seq, LANES), lambda bi, h, i: (bi, 0, 0)),
                  pl.BlockSpec((1, 1, tq, LANES), lambda bi, h, i: (bi, h, i, 0)),
                  pl.BlockSpec((nk, LANES, tk), lambda bi, h, i: (0, 0, 0)),
                  pl.BlockSpec((1, tq, LANES), lambda bi, h, i: (bi, i, 0))],
        out_specs=pl.BlockSpec((1, tq, group * HEAD_DIM), lambda bi, h, i: (bi, i, h)),
        out_shape=jax.ShapeDtypeStruct((b, seq, NSA_HEADS * HEAD_DIM), F32),
        compiler_params=_params(("arbitrary", "arbitrary", "arbitrary")),
        name="nsa_selected",
    )(gate_b, q_rot, k_slc, v_slc, sel, expand, gates)


def _window_kernel(sc_ref, q_ref, k_ref, v_ref, gl_ref, o_ref, *, tq, span, window, group, gated, sinks):
    hk = pl.program_id(1)
    i = pl.program_id(2)
    q4 = _stack_heads(q_ref[0], group)
    start = pl.multiple_of(jnp.maximum(i * tq + tq - span, 0), tq)
    kt = k_ref[0, pl.ds(start, span), :]
    vt = v_ref[0, pl.ds(start, span), :]
    s = _dot_nt(q4, kt).reshape(group, tq, span)
    t = i * tq + lax.broadcasted_iota(jnp.int32, (tq, span), 0)
    key = start + lax.broadcasted_iota(jnp.int32, (tq, span), 1)
    mask = ((key <= t) & (t - key < window))[None]
    s = jnp.where(mask, s, NEG_INF)
    m = jnp.max(s, axis=-1, keepdims=True)
    if sinks:
        sk = jnp.concatenate([jnp.full((1, 1, 1), sc_ref[hk * group + g], F32) for g in range(group)], axis=0)
        m = jnp.maximum(m, sk)
    e = jnp.where(mask, jnp.exp(s - m), 0.0)
    denom = jnp.sum(e, axis=-1, keepdims=True)
    if sinks:
        denom = denom + jnp.exp(sk - m)
    p = (e / denom).reshape(group * tq, span).astype(BF16)
    o4 = _dot(p, vt)
    heads = []
    for g in range(group):
        og = o4[g * tq:(g + 1) * tq]
        if gated:
            colg = (hk * group + g) * 3 + 2
            og = og * _sigmoid(_gate_column(gl_ref[0], colg) + sc_ref[colg])
        heads.append(og)
    o_ref[0] = _compact_group(heads, hk)


def _window_attention(q_rot, q_blk0, k_pair, v_pair, scalars, gates, *, seq, window, gated, sinks, tq=128):
    b = q_rot.shape[0]
    group = 4
    span = min(window + tq, seq)
    gw = group * LANES
    qb = q_blk0 // group
    return pl.pallas_call(
        functools.partial(_window_kernel, tq=tq, span=span, window=window, group=group, gated=gated, sinks=sinks),
        grid=(b, 2, seq // tq),
        in_specs=[pl.BlockSpec(memory_space=pltpu.SMEM),
                  pl.BlockSpec((1, tq, gw), lambda bi, h, i: (bi, i, qb + h)),
                  pl.BlockSpec((1, seq, LANES), lambda bi, h, i: (bi, 0, 0)),
                  pl.BlockSpec((1, seq, LANES), lambda bi, h, i: (bi, 0, 0)),
                  pl.BlockSpec((1, tq, LANES), lambda bi, h, i: (bi, i, 0))],
        out_specs=pl.BlockSpec((1, tq, group * HEAD_DIM), lambda bi, h, i: (bi, i, h)),
        out_shape=jax.ShapeDtypeStruct((b, seq, 8 * HEAD_DIM), F32),
        compiler_params=_params(("arbitrary", "arbitrary", "arbitrary")),
        name="window_attention",
    )(scalars, q_rot, k_pair, v_pair, gates)


def _dense_kernel(q_ref, k_ref, v_ref, dk_ref, o_ref, *, tq, tk, kw, decay):
    i = pl.program_id(2)
    t0 = i * tq
    n_full = t0 // tk
    outs = []
    for e in range(2):
        q = q_ref[0][:, e * LANES:(e + 1) * LANES]
        kcol = e * LANES if kw == 2 * LANES else 0

        def step(j, carry, masked):
            m, l, acc = carry
            off = pl.multiple_of(j * tk, tk)
            kt = k_ref[0, pl.ds(off, tk), kcol:kcol + LANES]
            vt = v_ref[0, pl.ds(off, tk), :]
            s = _dot_nt(q, kt)
            if decay:
                s = s - dk_ref[0, e, pl.ds(j, 1), :]
            if masked:
                t = t0 + lax.broadcasted_iota(jnp.int32, (tq, tk), 0)
                key = j * tk + lax.broadcasted_iota(jnp.int32, (tq, tk), 1)
                s = jnp.where(key <= t, s, NEG_INF)
            m_new = jnp.maximum(m, jnp.max(s, axis=-1, keepdims=True))
            p = jnp.exp(s - m_new)
            alpha = jnp.exp(m - m_new)
            l = alpha * l + jnp.sum(p, axis=-1, keepdims=True)
            return m_new, l, alpha * acc + _dot(p.astype(BF16), vt)

        init = (jnp.full((tq, 1), NEG_INF, F32), jnp.zeros((tq, 1), F32), jnp.zeros((tq, LANES), F32))
        carry = lax.fori_loop(0, n_full, functools.partial(step, masked=False), init)
        m, l, acc = step(n_full, carry, True)
        outs.append(acc / l)
    lane = _lane((tq, LANES))
    o_ref[0] = jnp.where(lane < HEAD_DIM, outs[0], outs[1])


def _dense_attention(q, q_blk0, k, k_blk0, kw, v, v_blk0, dk, *, seq, tq=256, tk=512):
    b = q.shape[0]
    tk = min(tk, seq)
    tq = min(tq, tk)
    nk = seq // tk
    decay = dk is not None
    if dk is None:
        dk = jnp.zeros((b, 8, nk, tk), F32)
    qb, kb, vb = q_blk0 // 2, k_blk0, v_blk0
    return pl.pallas_call(
        functools.partial(_dense_kernel, tq=tq, tk=tk, kw=kw, decay=decay),
        grid=(b, 4, seq // tq),
        in_specs=[pl.BlockSpec((1, tq, 2 * LANES), lambda bi, p, i: (bi, i, qb + p)),
                  pl.BlockSpec((1, seq, kw), lambda bi, p, i: (bi, 0, kb + p)),
                  pl.BlockSpec((1, seq, LANES), lambda bi, p, i: (bi, 0, vb + p)),
                  pl.BlockSpec((1, 2, nk, tk), lambda bi, p, i: (bi, p, 0, 0))],
        out_specs=pl.BlockSpec((1, tq, LANES), lambda bi, p, i: (bi, i, p)),
        out_shape=jax.ShapeDtypeStruct((b, seq, 4 * LANES), F32),
        compiler_params=_params(("arbitrary", "arbitrary", "arbitrary")),
        name="dense_causal_attention",
    )(q, k, v, dk)


def _decay_kernel(f_ref, b_ref, o_ref):
    x = f_ref[0] + b_ref[...]
    lf = jnp.minimum(x, 0.0) - jnp.log1p(jnp.exp(-jnp.abs(x)))
    n = lf.shape[-1]
    lane = _lane(lf.shape)
    d = 1
    while d < n:
        lf = lf + jnp.where(lane >= d, pltpu.roll(lf, d, 1), 0.0)
        d *= 2
    o_ref[0] = lf


def _decay_cumsum(f_t, bias):
    b, h, s = f_t.shape
    return pl.pallas_call(
        _decay_kernel,
        grid=(b,),
        in_specs=[pl.BlockSpec((1, h, s), lambda i: (i, 0, 0)), pl.BlockSpec((h, 1), lambda i: (0, 0))],
        out_specs=pl.BlockSpec((1, h, s), lambda i: (i, 0, 0)),
        out_shape=jax.ShapeDtypeStruct((b, h, s), F32),
        compiler_params=_params(("arbitrary",)),
        name="fox_decay_cumsum",
    )(f_t, bias.reshape(h, 1).astype(F32))


def _mla_prep_kernel(cq_ref, ckv_ref, misc_ref, gqa_ref, gkva_ref, wq_ref, wk_ref, wv_ref, gq_ref, gk_ref,
                     cos_ref, sin_ref, q_ref, k_ref, v_ref, *, scale):
    tm = cq_ref.shape[0]
    lane = _lane((tm, LANES))
    in_rope = (lane >= MLA_NOPE_DIM) & (lane < MLA_QK_DIM)
    first = lane < MLA_NOPE_DIM + MLA_ROPE_DIM // 2
    cos, sin = cos_ref[...], sin_ref[...]

    def rope_tail(x):
        sw = jnp.where(first, pltpu.roll(x, LANES - MLA_ROPE_DIM // 2, 1), pltpu.roll(x, MLA_ROPE_DIM // 2, 1))
        return x * cos + jnp.where(in_rope, sw, 0.0) * sin

    cq = _rms(cq_ref[...], gqa_ref[...], MLA_Q_RANK).astype(BF16)
    ckv = _rms(ckv_ref[...], gkva_ref[...], MLA_KV_RANK).astype(BF16)
    qa = _dot(cq, wq_ref[...])
    ka = _dot(ckv, wk_ref[...])
    k_rope = jnp.where(in_rope, misc_ref[...], 0.0)
    for h in range(MLA_HEADS):
        qh = _rms(qa[:, h * LANES:(h + 1) * LANES], gq_ref[...], MLA_QK_DIM)
        q_ref[:, h * LANES:(h + 1) * LANES] = (rope_tail(qh) * scale).astype(q_ref.dtype)
        kh = _rms(ka[:, h * LANES:(h + 1) * LANES] + k_rope, gk_ref[...], MLA_QK_DIM)
        k_ref[:, h * LANES:(h + 1) * LANES] = rope_tail(kh).astype(k_ref.dtype)
    v_ref[...] = _dot(ckv, wv_ref[...]).astype(v_ref.dtype)


def _mla_prep(y, cq_blk, ckv_blk, misc_blk, q_a_norm, w_q_b, kv_a_norm, w_kv_b, q_norm, k_norm, *, seq, tm=512):
    n = y.shape[0]
    h = MLA_HEADS
    pad = LANES - MLA_QK_DIM
    wq = jnp.pad(w_q_b.reshape(MLA_Q_RANK, h, MLA_QK_DIM), ((0, 0), (0, 0), (0, pad)))
    wq = wq.reshape(MLA_Q_RANK, h * LANES).astype(BF16)
    wkv = w_kv_b.reshape(MLA_KV_RANK, h, MLA_NOPE_DIM + MLA_V_DIM)
    wk = jnp.pad(wkv[:, :, :MLA_NOPE_DIM], ((0, 0), (0, 0), (0, LANES - MLA_NOPE_DIM)))
    wk = wk.reshape(MLA_KV_RANK, h * LANES).astype(BF16)
    wv = wkv[:, :, MLA_NOPE_DIM:].reshape(MLA_KV_RANK, h * MLA_V_DIM).astype(BF16)
    gq = jnp.pad(q_norm, (0, pad)).reshape(1, LANES)
    gk = jnp.pad(k_norm, (0, pad)).reshape(1, LANES)
    half = MLA_ROPE_DIM // 2
    inv_freq = 1.0 / (ROPE_THETA ** (jnp.arange(0, MLA_ROPE_DIM, 2, dtype=F32) / MLA_ROPE_DIM))
    ang = jnp.arange(seq, dtype=F32)[:, None] * inv_freq[None, :]
    cos, sin = jnp.cos(ang), jnp.sin(ang)
    ones = jnp.ones((seq, MLA_NOPE_DIM), F32)
    zeros = jnp.zeros((seq, MLA_NOPE_DIM), F32)
    cos_t = jnp.concatenate([ones, cos, cos, ones[:, :pad]], axis=1)
    sin_t = jnp.concatenate([zeros, -sin, sin, zeros[:, :pad]], axis=1)
    del half
    sblocks = seq // tm
    full = lambda shp: pl.BlockSpec(shp, lambda i: (0,) * len(shp))
    q, k, v = pl.pallas_call(
        functools.partial(_mla_prep_kernel, scale=MLA_QK_DIM ** -0.5),
        grid=(n // tm,),
        in_specs=[pl.BlockSpec((tm, MLA_Q_RANK), lambda i: (i, cq_blk // 2)),
                  pl.BlockSpec((tm, LANES), lambda i: (i, ckv_blk)),
                  pl.BlockSpec((tm, LANES), lambda i: (i, misc_blk)),
                  full((1, MLA_Q_RANK)), full((1, MLA_KV_RANK)),
                  full((MLA_Q_RANK, h * LANES)), full((MLA_KV_RANK, h * LANES)), full((MLA_KV_RANK, h * MLA_V_DIM)),
                  full((1, LANES)), full((1, LANES)),
                  pl.BlockSpec((tm, LANES), lambda i: (i % sblocks, 0)),
                  pl.BlockSpec((tm, LANES), lambda i: (i % sblocks, 0))],
        out_specs=[pl.BlockSpec((tm, h * LANES), lambda i: (i, 0)),
                   pl.BlockSpec((tm, h * LANES), lambda i: (i, 0)),
                   pl.BlockSpec((tm, h * MLA_V_DIM), lambda i: (i, 0))],
        out_shape=[jax.ShapeDtypeStruct((n, h * LANES), BF16),
                   jax.ShapeDtypeStruct((n, h * LANES), BF16),
                   jax.ShapeDtypeStruct((n, h * MLA_V_DIM), BF16)],
        compiler_params=_params(("arbitrary",)),
        name="mla_prep",
    )(y, y, y, q_a_norm.reshape(1, -1), kv_a_norm.reshape(1, -1), wq, wk, wv, gq, gk, cos_t, sin_t)
    return q, k, v


def _cols(w, a, b):
    return w[:, a:b]


def _nsa_swa_mixer(x, batch, seq, mix_norm, w_in, nsa_gate_b, nsa_q_norm, nsa_kc_norm, nsa_ks_norm, nsa_kw_norm,
                   cmp_pos_k, cmp_pos_v, cmpk_w1, cmpk_w2, cmpv_w1, cmpv_w2,
                   swa_q_norm, swa_k_norm, swa_sinks, w_out):
    n = batch * seq
    o = np.cumsum([0, 512, 128, 128, 128, 128, 128, 128, 24, 512, 128, 128])
    seg = lambda j: _cols(w_in, o[j], o[j + 1])
    q_a, kc, vc, ks, vs, kw, vw, gl, q_b, k_b, v_b = [seg(j) for j in range(11)]
    gl = jnp.pad(gl, ((0, 0), (0, LANES - gl.shape[1])))
    w = jnp.concatenate([q_a, q_b, ks, kw, k_b, kc, vc, vs, vw, v_b, gl], axis=1).astype(BF16)
    y = _rms_matmul(x, mix_norm, w)
    cos_t, sin_t = _rope_tables(seq)

    gq = jnp.stack([_pair_gain(nsa_q_norm)] * 4 + [_pair_gain(swa_q_norm)] * 4)
    dst = np.array([[c // 2, c // 2] for c in range(4)] * 2, np.int32)
    q_rot = _prep(y, 0, 8, seq, cos_t, sin_t, gains=gq, rope=True, scale=HEAD_DIM ** -0.5, dst=dst)
    q_cmp = _prep(y, 0, 4, seq, cos_t, sin_t, gains=gq[:4], scale=HEAD_DIM ** -0.5, dst=dst[:4])
    gk = jnp.stack([_pair_gain(nsa_ks_norm), _pair_gain(nsa_kw_norm), _pair_gain(swa_k_norm)])
    k_rot = _prep(y, 8, 3, seq, cos_t, sin_t, gains=gk, rope=True)
    v_all = _prep(y, 13, 3, seq, cos_t, sin_t)

    y3 = y.reshape(batch, seq, y.shape[1])
    k_cmp = _compress(y3[:, :, 11 * LANES:12 * LANES], cmp_pos_k, cmpk_w1, cmpk_w2, nsa_kc_norm)
    v_cmp = _compress(y3[:, :, 12 * LANES:13 * LANES], cmp_pos_v, cmpv_w1, cmpv_w2, None)
    gates = y3[:, :, 16 * LANES:17 * LANES]

    q_rot3 = q_rot.reshape(batch, seq, -1)
    q_cmp3 = q_cmp.reshape(batch, seq, -1)
    k3 = k_rot.reshape(batch, seq, -1)
    v3 = v_all.reshape(batch, seq, -1)
    k_slc, k_win, k_swa = (k3[:, :, j * LANES:(j + 1) * LANES] for j in range(3))
    v_slc, v_win, v_swa = (v3[:, :, j * LANES:(j + 1) * LANES] for j in range(3))
    gate_b = nsa_gate_b.astype(F32)

    o_cmp, sel = _nsa_compressed(q_cmp3, k_cmp, v_cmp, gates, gate_b, seq=seq)
    o_slc = _nsa_selected(q_rot3, k_slc, v_slc, sel, gates, gate_b, seq=seq)
    o_win = _window_attention(q_rot3, 0, k_win, v_win, gate_b, gates, seq=seq, window=NSA_WINDOW,
                              gated=True, sinks=False)
    o_swa = _window_attention(q_rot3, 8, k_swa, v_swa, swa_sinks.astype(F32), gates, seq=seq, window=SWA_WINDOW,
                              gated=False, sinks=True)
    flat = lambda a: a.reshape(n, -1)
    return _outproj(x, [flat(o_cmp), flat(o_slc), flat(o_win)], [flat(o_swa)], w_out)


def _fox_mla_mixer(x, batch, seq, mix_norm, w_in, fox_f_bias, fox_q_norm, fox_k_norm, mla_q_a_norm, mla_w_q_b,
                   mla_kv_a_norm, mla_w_kv_b, mla_q_norm, mla_k_norm, w_out):
    n = batch * seq
    o = np.cumsum([0, 512, 512, 512, 8, 256, 128, 32])
    seg = lambda j: _cols(w_in, o[j], o[j + 1])
    q_c, k_c, v_c, f_c, c_q, c_kv, k_r = [seg(j) for j in range(7)]
    d = w_in.shape[0]
    misc = jnp.concatenate([f_c, jnp.zeros((d, MLA_NOPE_DIM - 8), w_in.dtype), k_r,
                            jnp.zeros((d, LANES - MLA_QK_DIM), w_in.dtype)], axis=1)
    w = jnp.concatenate([q_c, k_c, v_c, c_q, c_kv, misc], axis=1).astype(BF16)
    y = _rms_matmul(x, mix_norm, w)
    cos_t, sin_t = _rope_tables(seq)

    dst = np.array([[0, 1]] * 4, np.int32)
    q_f = _prep(y, 0, 4, seq, cos_t, sin_t, gains=jnp.stack([_pair_gain(fox_q_norm)] * 4),
                scale=HEAD_DIM ** -0.5, dst=dst)
    k_f = _prep(y, 4, 4, seq, cos_t, sin_t, gains=jnp.stack([_pair_gain(fox_k_norm)] * 4))
    v_f = _prep(y, 8, 4, seq, cos_t, sin_t)

    y3 = y.reshape(batch, seq, y.shape[1])
    f_t = y3[:, :, 15 * LANES:15 * LANES + FOX_HEADS].transpose(0, 2, 1)
    dc = _decay_cumsum(f_t, fox_f_bias)
    tk = min(512, seq)
    dk = dc.reshape(batch, FOX_HEADS, seq // tk, tk)
    b3 = lambda a: a.reshape(batch, seq, -1)
    o_fox = _dense_attention(b3(q_f), 0, b3(k_f), 0, LANES, b3(v_f), 0, dk, seq=seq, tk=tk)

    q_m, k_m, v_m = _mla_prep(y, 12, 14, 15, mla_q_a_norm, mla_w_q_b, mla_kv_a_norm, mla_w_kv_b,
                              mla_q_norm, mla_k_norm, seq=seq)
    o_mla = _dense_attention(b3(q_m), 0, b3(k_m), 0, 2 * LANES, b3(v_m), 0, None, seq=seq, tk=tk)
    flat = lambda a: a.reshape(n, -1)
    return _outproj(x, [flat(o_fox)], [flat(o_mla)], w_out)


def kernel(x, l0_ffn1_norm, l0_ffn1_w_gate, l0_ffn1_w_up, l0_ffn1_w_down, l0_mix_norm, l0_w_in, l0_nsa_gate_b, l0_nsa_q_norm, l0_nsa_kc_norm, l0_nsa_ks_norm, l0_nsa_kw_norm, l0_cmp_pos_k, l0_cmp_pos_v, l0_cmpk_w1, l0_cmpk_w2, l0_cmpv_w1, l0_cmpv_w2, l0_swa_q_norm, l0_swa_k_norm, l0_swa_sinks, l0_w_out, l0_ffn2_norm, l0_ffn2_w_gate, l0_ffn2_w_up, l0_ffn2_w_down, l1_ffn1_norm, l1_ffn1_w_gate, l1_ffn1_w_up, l1_ffn1_w_down, l1_mix_norm, l1_w_in, l1_fox_f_bias, l1_fox_q_norm, l1_fox_k_norm, l1_mla_q_a_norm, l1_mla_w_q_b, l1_mla_kv_a_norm, l1_mla_w_kv_b, l1_mla_q_norm, l1_mla_k_norm, l1_w_out, l1_ffn2_norm, l1_ffn2_w_gate, l1_ffn2_w_up, l1_ffn2_w_down):
    batch, seq, d = x.shape
    h = x.reshape(batch * seq, d)
    h = _ffn(h, l0_ffn1_norm, l0_ffn1_w_gate, l0_ffn1_w_up, l0_ffn1_w_down)
    h = _nsa_swa_mixer(h, batch, seq, l0_mix_norm, l0_w_in, l0_nsa_gate_b, l0_nsa_q_norm, l0_nsa_kc_norm,
                       l0_nsa_ks_norm, l0_nsa_kw_norm, l0_cmp_pos_k, l0_cmp_pos_v, l0_cmpk_w1, l0_cmpk_w2,
                       l0_cmpv_w1, l0_cmpv_w2, l0_swa_q_norm, l0_swa_k_norm, l0_swa_sinks, l0_w_out)
    h = _ffn(h, l0_ffn2_norm, l0_ffn2_w_gate, l0_ffn2_w_up, l0_ffn2_w_down)
    h = _ffn(h, l1_ffn1_norm, l1_ffn1_w_gate, l1_ffn1_w_up, l1_ffn1_w_down)
    h = _fox_mla_mixer(h, batch, seq, l1_mix_norm, l1_w_in, l1_fox_f_bias, l1_fox_q_norm, l1_fox_k_norm,
                       l1_mla_q_a_norm, l1_mla_w_q_b, l1_mla_kv_a_norm, l1_mla_w_kv_b, l1_mla_q_norm,
                       l1_mla_k_norm, l1_w_out)
    h = _ffn(h, l1_ffn2_norm, l1_ffn2_w_gate, l1_ffn2_w_up, l1_ffn2_w_down)
    return h.reshape(batch, seq, d)
```

```python
import functools
import math

import numpy as np
import jax
import jax.numpy as jnp
from jax import lax
from jax.experimental import pallas as pl
from jax.experimental.pallas import tpu as pltpu

F32 = jnp.float32
BF16 = jnp.bfloat16

HEAD_DIM = 64
LANES = 128
ROPE_THETA = 10000.0
RMS_EPS = 1e-6
NEG_INF = -1e30
FORCE_SCORE = 1e9
BELOW_ALL = -3e38
LOG2E = 1.4426950408889634
BIG = 1e30
VT_ROWS = 80

NSA_HEADS = 8
NSA_KV_HEADS = 2
CMP_BLOCK = 32
CMP_STRIDE = 16
CMP_HIDDEN = 256
SLC_BLOCK = 64
N_SELECT = 16
NSA_WINDOW = 512
SWA_HEADS = 8
SWA_KV_HEADS = 2
SWA_WINDOW = 128
FOX_HEADS = 8
MLA_HEADS = 8
MLA_Q_RANK = 256
MLA_KV_RANK = 128
MLA_NOPE_DIM = 64
MLA_ROPE_DIM = 32
MLA_V_DIM = 64
MLA_QK_DIM = MLA_NOPE_DIM + MLA_ROPE_DIM

VMEM_LIMIT = 48 * 1024 * 1024

NT_DIMS = (((1,), (1,)), ((), ()))


def _params(sem):
    return pltpu.CompilerParams(dimension_semantics=sem, vmem_limit_bytes=VMEM_LIMIT)


def _dot(a, b):
    return jnp.dot(a, b, preferred_element_type=F32)


def _dot_nt(a, b):
    return lax.dot_general(a, b, NT_DIMS, preferred_element_type=F32)


def _rms(x, gain, n):
    ms = jnp.sum(x * x, axis=-1, keepdims=True) * (1.0 / n)
    return (x * lax.rsqrt(ms + RMS_EPS)) * gain


def _lane(shape):
    return lax.broadcasted_iota(jnp.int32, shape, len(shape) - 1)


def _ffn_kernel(x_ref, g_ref, wg_ref, wu_ref, wd_ref, o_ref, h_sc, acc_sc, *, nf):
    x = x_ref[...]
    h_sc[...] = _rms(x, g_ref[...], x.shape[-1]).astype(BF16)
    acc_sc[...] = jnp.zeros_like(acc_sc)

    def body(c, carry):
        h = h_sc[...]
        g = _dot(h, wg_ref[c])
        u = _dot(h, wu_ref[c])
        a = (g * (1.0 / (1.0 + jnp.exp(-g)))) * u
        acc_sc[...] += _dot(a.astype(BF16), wd_ref[c])
        return carry

    lax.fori_loop(0, nf, body, 0)
    o_ref[...] = x + 0.5 * acc_sc[...]


def _ffn(x, norm, w_gate, w_up, w_down, *, tm=512, tf=256):
    n, d = x.shape
    f = w_gate.shape[1]
    nf = f // tf
    wg = w_gate.astype(BF16).reshape(d, nf, tf).transpose(1, 0, 2)
    wu = w_up.astype(BF16).reshape(d, nf, tf).transpose(1, 0, 2)
    wd = w_down.astype(BF16).reshape(nf, tf, d)
    wspec = lambda shp: pl.BlockSpec(shp, lambda i: (0, 0, 0), pipeline_mode=pl.Buffered(1))
    return pl.pallas_call(
        functools.partial(_ffn_kernel, nf=nf),
        grid=(n // tm,),
        in_specs=[pl.BlockSpec((tm, d), lambda i: (i, 0)),
                  pl.BlockSpec((1, d), lambda i: (0, 0)),
                  wspec((nf, d, tf)), wspec((nf, d, tf)), wspec((nf, tf, d))],
        out_specs=pl.BlockSpec((tm, d), lambda i: (i, 0)),
        out_shape=jax.ShapeDtypeStruct((n, d), F32),
        scratch_shapes=[pltpu.VMEM((tm, d), BF16), pltpu.VMEM((tm, d), F32)],
        compiler_params=_params(("arbitrary",)),
        name="ffn",
    )(x, norm.reshape(1, d), wg, wu, wd)


def _rms_matmul_kernel(x_ref, g_ref, w_ref, o_ref):
    x = x_ref[...]
    h = _rms(x, g_ref[...], x.shape[-1]).astype(BF16)
    o_ref[...] = _dot(h, w_ref[...])


def _rms_matmul(x, norm, w, *, tm=512):
    n, d = x.shape
    c = w.shape[1]
    return pl.pallas_call(
        _rms_matmul_kernel,
        grid=(n // tm,),
        in_specs=[pl.BlockSpec((tm, d), lambda i: (i, 0)),
                  pl.BlockSpec((1, d), lambda i: (0, 0)),
                  pl.BlockSpec((d, c), lambda i: (0, 0), pipeline_mode=pl.Buffered(1))],
        out_specs=pl.BlockSpec((tm, c), lambda i: (i, 0)),
        out_shape=jax.ShapeDtypeStruct((n, c), F32),
        compiler_params=_params(("arbitrary",)),
        name="rms_matmul",
    )(x, norm.reshape(1, d), w)


def _outproj_kernel(*refs, n_a, n_b):
    x_ref = refs[0]
    a_refs = refs[1:1 + n_a]
    b_refs = refs[1 + n_a:1 + n_a + n_b]
    wa_ref, wb_ref, o_ref = refs[1 + n_a + n_b:]
    a = a_refs[0][...]
    for r in a_refs[1:]:
        a = a + r[...]
    b = b_refs[0][...]
    for r in b_refs[1:]:
        b = b + r[...]
    o_ref[...] = x_ref[...] + _dot(a.astype(BF16), wa_ref[...]) + _dot(b.astype(BF16), wb_ref[...])


def _outproj(x, a_list, b_list, w_out, *, tm=512):
    n, d = x.shape
    ca = a_list[0].shape[1]
    cb = b_list[0].shape[1]
    wa = w_out[:ca].astype(BF16)
    wb = w_out[ca:].astype(BF16)
    row = lambda c: pl.BlockSpec((tm, c), lambda i: (i, 0))
    return pl.pallas_call(
        functools.partial(_outproj_kernel, n_a=len(a_list), n_b=len(b_list)),
        grid=(n // tm,),
        in_specs=[row(d)] + [row(ca)] * len(a_list) + [row(cb)] * len(b_list)
                 + [pl.BlockSpec((ca, d), lambda i: (0, 0)), pl.BlockSpec((cb, d), lambda i: (0, 0))],
        out_specs=row(d),
        out_shape=jax.ShapeDtypeStruct((n, d), F32),
        compiler_params=_params(("arbitrary",)),
        name="outproj",
    )(x, *a_list, *b_list, wa, wb)


def _prep_kernel(dst_ref, y_ref, gain_ref, cos_ref, sin_ref, aux_ref, o_ref, *, norm, rope, scale, expand, aug):
    c = pl.program_id(1)
    x = y_ref[...]
    lane = _lane(x.shape)
    low = lane < HEAD_DIM
    if norm:
        x2 = x * x
        s_lo = jnp.sum(jnp.where(low, x2, 0.0), axis=-1, keepdims=True)
        s_hi = jnp.sum(jnp.where(low, 0.0, x2), axis=-1, keepdims=True)
        ms = jnp.where(low, s_lo, s_hi) * (1.0 / HEAD_DIM)
        x = (x * lax.rsqrt(ms + RMS_EPS)) * gain_ref[0]
    if rope:
        swapped = jnp.where((lane & (HEAD_DIM - 1)) < HEAD_DIM // 2,
                            pltpu.roll(x, LANES - HEAD_DIM // 2, 1), pltpu.roll(x, HEAD_DIM // 2, 1))
        x = x * cos_ref[...] + swapped * sin_ref[...]
    if scale != 1.0:
        x = x * scale
    if expand == "v":
        r = pltpu.roll(x, HEAD_DIM, 1)
        tail = jnp.where(lane == HEAD_DIM, 1.0, 0.0)
        o_ref[...] = jnp.concatenate([jnp.where(low, x, tail), jnp.where(low, r, tail)],
                                     axis=1).astype(o_ref.dtype)
    elif expand == "q":
        r = pltpu.roll(x, HEAD_DIM, 1)
        de = jnp.broadcast_to(dst_ref[c, 0], x.shape) == 0
        do = jnp.broadcast_to(dst_ref[c, 1], x.shape) == 0
        zero = jnp.zeros_like(x)
        h_even = jnp.where(de, jnp.where(low, x, zero), jnp.where(low, zero, r))
        h_odd = jnp.where(do, jnp.where(low, r, zero), jnp.where(low, zero, x))
        if aug is not None:
            tails = []
            for e in range(2):
                if aug == "ones":
                    tails.append(jnp.where((lane >= HEAD_DIM) & (lane < HEAD_DIM + 3), 1.0, 0.0))
                else:
                    d = jnp.sum(jnp.where(lane == 2 * c + e, aux_ref[...], 0.0), axis=-1, keepdims=True)
                    hi = d.astype(BF16).astype(F32)
                    mid = (d - hi).astype(BF16).astype(F32)
                    lo = d - hi - mid
                    tails.append(jnp.where(lane == HEAD_DIM, -hi, jnp.where(lane == HEAD_DIM + 1, -mid,
                                 jnp.where(lane == HEAD_DIM + 2, -lo, 0.0))))
            h_even = jnp.where(low, h_even, tails[0])
            h_odd = jnp.where(low, h_odd, tails[1])
        o_ref[...] = jnp.concatenate([h_even, h_odd], axis=1).astype(o_ref.dtype)
    else:
        o_ref[...] = x.astype(o_ref.dtype)


def _prep(y, blk0, nblk, seq, cos_t, sin_t, *, gains=None, rope=False, scale=1.0, dst=None, vexp=False,
          aug=None, aux=None, tm=512):
    n = y.shape[0]
    aux_spec = pl.BlockSpec((tm, LANES), lambda i, c: (i, 0))
    if aux is None:
        aux, aux_spec = jnp.zeros((tm, LANES), F32), pl.BlockSpec((tm, LANES), lambda i, c: (0, 0))
    norm = gains is not None
    expand = "v" if vexp else ("q" if dst is not None else None)
    if gains is None:
        gains = jnp.ones((nblk, 1, LANES), F32)
    if dst is None:
        dst = np.zeros((nblk, 2), np.int32)
    ow = 2 * LANES if expand else LANES
    sblocks = seq // tm
    return pl.pallas_call(
        functools.partial(_prep_kernel, norm=norm, rope=rope, scale=scale, expand=expand, aug=aug),
        grid=(n // tm, nblk),
        in_specs=[pl.BlockSpec(memory_space=pltpu.SMEM),
                  pl.BlockSpec((tm, LANES), lambda i, c: (i, blk0 + c)),
                  pl.BlockSpec((1, 1, LANES), lambda i, c: (c, 0, 0)),
                  pl.BlockSpec((tm, LANES), lambda i, c: (i % sblocks, 0)),
                  pl.BlockSpec((tm, LANES), lambda i, c: (i % sblocks, 0)),
                  aux_spec],
        out_specs=pl.BlockSpec((tm, ow), lambda i, c: (i, c)),
        out_shape=jax.ShapeDtypeStruct((n, nblk * ow), BF16),
        compiler_params=_params(("arbitrary", "arbitrary")),
        name="head_prep",
    )(jnp.asarray(dst, jnp.int32), y, gains, cos_t, sin_t, aux)


def _pair_gain(g):
    return jnp.concatenate([g, g]).reshape(1, LANES).astype(F32)


def _rope_tables(seq):
    half = HEAD_DIM // 2
    inv_freq = 1.0 / (ROPE_THETA ** (jnp.arange(0, HEAD_DIM, 2, dtype=F32) / HEAD_DIM))
    ang = jnp.arange(seq, dtype=F32)[:, None] * inv_freq[None, :]
    cos, sin = jnp.cos(ang), jnp.sin(ang)
    cos_t = jnp.concatenate([cos, cos, cos, cos], axis=1)
    sin_t = jnp.concatenate([-sin, sin, -sin, sin], axis=1)
    del half
    return cos_t, sin_t


def _compress_kernel(ch_ref, ptop_ref, pbot_ref, w1t_ref, w1b_ref, w2_ref, gain_ref, o_ref, *, norm):
    ch = ch_ref[0]
    a = _dot((ch + ptop_ref[...]).astype(BF16), w1t_ref[...])
    b = _dot((ch + pbot_ref[...]).astype(BF16), w1b_ref[...])
    nc = a.shape[0]
    hid = a + pltpu.roll(b, nc - 1, 0)
    act = hid * (1.0 / (1.0 + jnp.exp(-hid)))
    out = _dot(act.astype(BF16), w2_ref[...])
    if norm:
        lane = _lane(out.shape)
        low = lane < HEAD_DIM
        o2 = out * out
        s_lo = jnp.sum(jnp.where(low, o2, 0.0), axis=-1, keepdims=True)
        s_hi = jnp.sum(jnp.where(low, 0.0, o2), axis=-1, keepdims=True)
        ms = jnp.where(low, s_lo, s_hi) * (1.0 / HEAD_DIM)
        out = (out * lax.rsqrt(ms + RMS_EPS)) * gain_ref[...]
    o_ref[0] = out.astype(o_ref.dtype)


def _compress(t_pair, pos_emb, w1, w2, gain):
    b, s, _ = t_pair.shape
    nc = s // CMP_STRIDE
    hid = w1.shape[1]
    ch = t_pair.reshape(b, nc, CMP_STRIDE * LANES)
    eye2 = jnp.eye(2, dtype=F32)
    w1r = w1.reshape(CMP_BLOCK, HEAD_DIM, hid)
    def expand_w1(w):
        return jnp.einsum('pdj,kl->pkdlj', w, eye2).reshape(CMP_STRIDE * LANES, 2 * hid).astype(BF16)
    w1t, w1b = expand_w1(w1r[:CMP_STRIDE]), expand_w1(w1r[CMP_STRIDE:])
    w2e = jnp.einsum('jd,kl->kjld', w2, eye2).reshape(2 * hid, LANES).astype(BF16)
    def expand_pos(p):
        return jnp.broadcast_to(p[:, None, :], (CMP_STRIDE, 2, HEAD_DIM)).reshape(1, CMP_STRIDE * LANES)
    ptop, pbot = expand_pos(pos_emb[:CMP_STRIDE]), expand_pos(pos_emb[CMP_STRIDE:])
    norm = gain is not None
    g = _pair_gain(gain) if norm else jnp.ones((1, LANES), F32)
    full = lambda shp: pl.BlockSpec(shp, lambda i: (0,) * len(shp))
    return pl.pallas_call(
        functools.partial(_compress_kernel, norm=norm),
        grid=(b,),
        in_specs=[pl.BlockSpec((1, nc, CMP_STRIDE * LANES), lambda i: (i, 0, 0)),
                  full((1, CMP_STRIDE * LANES)), full((1, CMP_STRIDE * LANES)),
                  full((CMP_STRIDE * LANES, 2 * hid)), full((CMP_STRIDE * LANES, 2 * hid)),
                  full((2 * hid, LANES)), full((1, LANES))],
        out_specs=pl.BlockSpec((1, nc, LANES), lambda i: (i, 0, 0)),
        out_shape=jax.ShapeDtypeStruct((b, nc, LANES), BF16),
        compiler_params=_params(("arbitrary",)),
        name="nsa_compress",
    )(ch, ptop, pbot, w1t, w1b, w2e, g)


def _stack_heads(q, n):
    return jnp.concatenate([q[:, g * LANES:(g + 1) * LANES] for g in range(n)], axis=0)


def _gate_column(gl, col):
    lane = _lane(gl.shape)
    return jnp.sum(jnp.where(lane == col, gl, 0.0), axis=-1, keepdims=True)


def _sigmoid(x):
    return 1.0 / (1.0 + jnp.exp(-x))


def _compact_group(heads, hk):
    tq = heads[0].shape[0]
    lane = _lane((tq, LANES))
    low = lane < HEAD_DIM
    outs = []
    for e in range(0, len(heads), 2):
        he, ho = heads[e], heads[e + 1]
        ho_r = pltpu.roll(ho, HEAD_DIM, 1)
        if hk is None:
            lo_part, hi_part = he, ho_r
        else:
            at_low = jnp.broadcast_to(hk, (tq, LANES)) == 0
            lo_part = jnp.where(at_low, he, pltpu.roll(he, HEAD_DIM, 1))
            hi_part = jnp.where(at_low, ho_r, ho)
        outs.append(jnp.where(low, lo_part, hi_part))
    return jnp.concatenate(outs, axis=1)


def _nsa_cmp_kernel(gb_ref, q_ref, k_ref, v_ref, ov_ref, gl_ref, o_ref, sel_ref, *, tq, group, n_sel, ns):
    hk = pl.program_id(1)
    i = pl.program_id(2)
    q4 = _stack_heads(q_ref[0], group)
    kc = k_ref[0]
    ncp = kc.shape[0]
    logits = _dot_nt(q4, kc).reshape(group, tq, ncp)
    t = i * tq + lax.broadcasted_iota(jnp.int32, (tq, ncp), 0)
    cmp_end = lax.broadcasted_iota(jnp.int32, (tq, ncp), 1) * CMP_STRIDE + (CMP_BLOCK - 1)
    mask = (cmp_end <= t)[None]
    logits = jnp.where(mask, logits, NEG_INF)
    m = jnp.max(logits, axis=-1, keepdims=True)
    e = jnp.where(mask, jnp.exp2(logits - m), 0.0)
    p = e / jnp.maximum(jnp.sum(e, axis=-1, keepdims=True), 1e-30)
    o4 = _dot(p.reshape(group * tq, ncp).astype(BF16), v_ref[0])
    gl = gl_ref[0]
    heads = []
    for g in range(group):
        col = (hk * group + g) * 3
        gate = _sigmoid(_gate_column(gl, col) + gb_ref[col])
        heads.append(o4[g * tq:(g + 1) * tq] * gate)
    o_ref[0] = _compact_group(heads, hk)

    ps = jnp.sum(p, axis=0)
    ps_hi = ps.astype(BF16)
    ps_lo = (ps - ps_hi.astype(F32)).astype(BF16)
    imp = _dot(ps_hi, ov_ref[...]) + _dot(ps_lo, ov_ref[...])
    blk = _lane((tq, LANES))
    tcol = i * tq + lax.broadcasted_iota(jnp.int32, (tq, LANES), 0)
    cur = tcol // SLC_BLOCK
    forced = (blk == 0) | (blk == cur) | (blk == cur - 1)
    score = jnp.where(forced, FORCE_SCORE, jnp.where(blk <= cur, imp, NEG_INF))
    score = jnp.where(blk < ns, score, BELOW_ALL)
    blk_f = blk.astype(F32)

    def pick(_, carry):
        sc, sel = carry
        mx = jnp.max(sc, axis=-1, keepdims=True)
        first = jnp.min(jnp.where(sc == mx, blk_f, float(LANES)), axis=-1, keepdims=True)
        hit = blk_f == first
        return jnp.where(hit, BELOW_ALL, sc), jnp.where(hit, 1.0, sel)

    _, sel = lax.fori_loop(0, n_sel, pick, (score, jnp.zeros((tq, LANES), F32)))
    sel_ref[0, 0] = jnp.where(blk <= cur, sel, 0.0).astype(sel_ref.dtype)


def _nsa_compressed(q_cmp, k_cmp, v_cmp, gates, gate_b, *, seq, tq=128):
    b = q_cmp.shape[0]
    group = NSA_HEADS // NSA_KV_HEADS
    ncp = k_cmp.shape[1]
    ns = seq // SLC_BLOCK
    n_sel = min(N_SELECT, ns)
    c_start = np.arange(ncp)[:, None] * CMP_STRIDE
    s_start = np.arange(LANES)[None, :] * SLC_BLOCK
    overlap = np.maximum(np.minimum(c_start + CMP_BLOCK, s_start + SLC_BLOCK) - np.maximum(c_start, s_start), 0)
    overlap = np.where((np.arange(LANES)[None, :] < ns) & (np.arange(ncp)[:, None] < ncp - 1), overlap, 0)
    overlap = jnp.asarray(overlap, BF16)
    gw = group * LANES
    return pl.pallas_call(
        functools.partial(_nsa_cmp_kernel, tq=tq, group=group, n_sel=n_sel, ns=ns),
        grid=(b, NSA_KV_HEADS, seq // tq),
        in_specs=[pl.BlockSpec(memory_space=pltpu.SMEM),
                  pl.BlockSpec((1, tq, gw), lambda bi, h, i: (bi, i, h)),
                  pl.BlockSpec((1, ncp, LANES), lambda bi, h, i: (bi, 0, 0)),
                  pl.BlockSpec((1, ncp, LANES), lambda bi, h, i: (bi, 0, 0)),
                  pl.BlockSpec((ncp, LANES), lambda bi, h, i: (0, 0)),
                  pl.BlockSpec((1, tq, LANES), lambda bi, h, i: (bi, i, 0))],
        out_specs=[pl.BlockSpec((1, tq, group * HEAD_DIM), lambda bi, h, i: (bi, i, h)),
                   pl.BlockSpec((1, 1, tq, LANES), lambda bi, h, i: (bi, h, i, 0))],
        out_shape=[jax.ShapeDtypeStruct((b, seq, NSA_HEADS * HEAD_DIM), F32),
                   jax.ShapeDtypeStruct((b, NSA_KV_HEADS, seq, LANES), BF16)],
        compiler_params=_params(("arbitrary", "arbitrary", "arbitrary")),
        name="nsa_compressed_select",
    )(gate_b, q_cmp, k_cmp, v_cmp, overlap, gates)


def _nsa_slc_kernel(gb_ref, q_ref, k_ref, v_ref, sel_ref, gl_ref, o_ref, sa_ref, sb_ref, *, tq, tk, group):
    hk = pl.program_id(1)
    i = pl.program_id(2)
    q = q_ref[0]
    unsel = (sel_ref[0, 0].astype(F32) - 1.0).astype(BF16)
    lhs = [jnp.concatenate([jnp.concatenate([q[:, g * LANES:(g + 1) * LANES], unsel], axis=1)
                            for g in (2 * c, 2 * c + 1)], axis=0) for c in range(group // 2)]
    n_full = (i * tq) // tk
    chains = group // 2

    def logits_into(buf, j):
        off = pl.multiple_of(j * tk, tk)
        kt = k_ref[0, pl.ds(off, tk), :]
        for c in range(chains):
            buf[c] = _dot_nt(lhs[c], kt)

    def consume(buf, j, states, masked):
        off = pl.multiple_of(j * tk, tk)
        vt = v_ref[0, pl.ds(off, tk), :]
        out = []
        for c in range(chains):
            m, acc = states[c]
            s = buf[c]
            if masked:
                t = i * tq + lax.broadcasted_iota(jnp.int32, (tq, tk), 0)
                key = j * tk + lax.broadcasted_iota(jnp.int32, (tq, tk), 1)
                ok = key <= t
                s = jnp.where(jnp.concatenate([ok, ok], axis=0), s, NEG_INF)
            m_new = jnp.maximum(m, jnp.max(s, axis=-1, keepdims=True))
            p = jnp.exp2(s - m_new)
            out.append((m_new, jnp.exp2(m - m_new) * acc + _dot(p.astype(BF16), vt)))
        return tuple(out)

    def pair(jj, states):
        j = 2 * jj
        logits_into(sb_ref, j + 1)
        states = consume(sa_ref, j, states, False)
        logits_into(sa_ref, j + 2)
        return consume(sb_ref, j + 1, states, False)

    init = tuple((jnp.full((2 * tq, 1), NEG_INF, F32), jnp.zeros((2 * tq, LANES), F32)) for _ in range(chains))
    logits_into(sa_ref, 0)
    states = lax.fori_loop(0, n_full // 2, pair, init)
    r = 2 * (n_full // 2)

    def diagonal_is_next(states):
        return consume(sa_ref, r, states, True)

    def one_full_tile_left(states):
        logits_into(sb_ref, r + 1)
        return consume(sb_ref, r + 1, consume(sa_ref, r, states, False), True)

    carry = lax.cond(r == n_full, diagonal_is_next, one_full_tile_left, states)
    gl = gl_ref[0]
    heads = []
    for g in range(group):
        acc = carry[g // 2][1][(g % 2) * tq:(g % 2 + 1) * tq]
        colg = (hk * group + g) * 3 + 1
        gate = _sigmoid(_gate_column(gl, colg) + gb_ref[colg])
        heads.append(acc * (gate / acc[:, HEAD_DIM:HEAD_DIM + 1]))
    o_ref[0] = _compact_group(heads, None)


def _nsa_selected(q_rot, k_aug, v_exp, sel, gates, gate_b, *, seq, tq=128, tk=512):
    b = q_rot.shape[0]
    group = NSA_HEADS // NSA_KV_HEADS
    tk = min(tk, seq)
    gw = group * LANES
    return pl.pallas_call(
        functools.partial(_nsa_slc_kernel, tq=tq, tk=tk, group=group),
        grid=(b, NSA_KV_HEADS, seq // tq),
        in_specs=[pl.BlockSpec(memory_space=pltpu.SMEM),
                  pl.BlockSpec((1, tq, gw), lambda bi, h, i: (bi, i, h)),
                  pl.BlockSpec((1, seq, 2 * LANES), lambda bi, h, i: (bi, 0, 0)),
                  pl.BlockSpec((1, seq, LANES), lambda bi, h, i: (bi, 0, h)),
                  pl.BlockSpec((1, 1, tq, LANES), lambda bi, h, i: (bi, h, i, 0)),
                  pl.BlockSpec((1, tq, LANES), lambda bi, h, i: (bi, i, 0))],
        out_specs=pl.BlockSpec((1, tq, group * HEAD_DIM), lambda bi, h, i: (bi, i, h)),
        out_shape=jax.ShapeDtypeStruct((b, seq, NSA_HEADS * HEAD_DIM), F32),
        scratch_shapes=[pltpu.VMEM((group // 2, 2 * tq, tk), F32), pltpu.VMEM((group // 2, 2 * tq, tk), F32)],
        compiler_params=_params(("arbitrary", "arbitrary", "arbitrary")),
        name="nsa_selected",
    )(gate_b, q_rot, k_aug, v_exp, sel, gates)


def _window_kernel(sc_ref, q_ref, k_ref, v_ref, gl_ref, o_ref, *, tq, span, window, group, gated, sinks):
    hk = pl.program_id(1)
    i = pl.program_id(2)
    q4 = _stack_heads(q_ref[0], group)
    start = pl.multiple_of(jnp.maximum(i * tq + tq - span, 0), tq)
    kt = k_ref[0, pl.ds(start, span), :]
    vt = v_ref[0, pl.ds(start, span), :]
    s = _dot_nt(q4, kt).reshape(group, tq, span)
    t = i * tq + lax.broadcasted_iota(jnp.int32, (tq, span), 0)
    key = start + lax.broadcasted_iota(jnp.int32, (tq, span), 1)
    mask = ((key <= t) & (t - key < window))[None]
    s = jnp.where(mask, s, NEG_INF)
    m = jnp.max(s, axis=-1, keepdims=True)
    if sinks:
        sk = jnp.concatenate([jnp.full((1, 1, 1), sc_ref[hk * group + g] * LOG2E, F32) for g in range(group)],
                             axis=0)
        m = jnp.maximum(m, sk)
    e = jnp.where(mask, jnp.exp2(s - m), 0.0)
    denom = jnp.sum(e, axis=-1, keepdims=True)
    if sinks:
        denom = denom + jnp.exp2(sk - m)
    p = (e / denom).reshape(group * tq, span).astype(BF16)
    o4 = _dot(p, vt)
    heads = []
    for g in range(group):
        og = o4[g * tq:(g + 1) * tq]
        if gated:
            colg = (hk * group + g) * 3 + 2
            og = og * _sigmoid(_gate_column(gl_ref[0], colg) + sc_ref[colg])
        heads.append(og)
    o_ref[0] = _compact_group(heads, hk)


def _window_attention(q_rot, q_blk0, k_pair, v_pair, scalars, gates, *, seq, window, gated, sinks, tq=128):
    b = q_rot.shape[0]
    group = 4
    span = min(window + tq, seq)
    gw = group * LANES
    qb = q_blk0 // group
    return pl.pallas_call(
        functools.partial(_window_kernel, tq=tq, span=span, window=window, group=group, gated=gated, sinks=sinks),
        grid=(b, 2, seq // tq),
        in_specs=[pl.BlockSpec(memory_space=pltpu.SMEM),
                  pl.BlockSpec((1, tq, gw), lambda bi, h, i: (bi, i, qb + h)),
                  pl.BlockSpec((1, seq, LANES), lambda bi, h, i: (bi, 0, 0)),
                  pl.BlockSpec((1, seq, LANES), lambda bi, h, i: (bi, 0, 0)),
                  pl.BlockSpec((1, tq, LANES), lambda bi, h, i: (bi, i, 0))],
        out_specs=pl.BlockSpec((1, tq, group * HEAD_DIM), lambda bi, h, i: (bi, i, h)),
        out_shape=jax.ShapeDtypeStruct((b, seq, 8 * HEAD_DIM), F32),
        compiler_params=_params(("arbitrary", "arbitrary", "arbitrary")),
        name="window_attention",
    )(scalars, q_rot, k_pair, v_pair, gates)


def _dense_kernel(q_ref, k_ref, v_ref, o_ref, sa_ref, sb_ref, *, tq, tk, nh):
    i = pl.program_id(2)
    t0 = i * tq
    n_full = t0 // tk
    qs = [q_ref[0][:, e * LANES:(e + 1) * LANES] for e in range(nh)]

    def logits_into(buf, j):
        off = pl.multiple_of(j * tk, tk)
        for e in range(nh):
            buf[e] = _dot_nt(qs[e], k_ref[0, pl.ds(off, tk), e * LANES:(e + 1) * LANES])

    def consume(buf, j, states, masked):
        off = pl.multiple_of(j * tk, tk)
        out = []
        for e in range(nh):
            m, acc = states[e]
            s = buf[e]
            if masked:
                t = t0 + lax.broadcasted_iota(jnp.int32, (tq, tk), 0)
                key = j * tk + lax.broadcasted_iota(jnp.int32, (tq, tk), 1)
                s = jnp.where(key <= t, s, NEG_INF)
            vt = v_ref[0, pl.ds(off, tk), e * LANES:(e + 1) * LANES]
            m_new = jnp.maximum(m, jnp.max(s, axis=-1, keepdims=True))
            p = jnp.exp2(s - m_new)
            out.append((m_new, jnp.exp2(m - m_new) * acc + _dot(p.astype(BF16), vt)))
        return tuple(out)

    def pair(jj, states):
        j = 2 * jj
        logits_into(sb_ref, j + 1)
        states = consume(sa_ref, j, states, False)
        logits_into(sa_ref, j + 2)
        return consume(sb_ref, j + 1, states, False)

    init = tuple((jnp.full((tq, 1), NEG_INF, F32), jnp.zeros((tq, LANES), F32)) for _ in range(nh))
    logits_into(sa_ref, 0)
    states = lax.fori_loop(0, n_full // 2, pair, init)
    r = 2 * (n_full // 2)

    def diagonal_is_next(states):
        return consume(sa_ref, r, states, True)

    def one_full_tile_left(states):
        logits_into(sb_ref, r + 1)
        return consume(sb_ref, r + 1, consume(sa_ref, r, states, False), True)

    states = lax.cond(r == n_full, diagonal_is_next, one_full_tile_left, states)
    outs = [acc * (1.0 / acc[:, HEAD_DIM:HEAD_DIM + 1]) for _, acc in states]
    lane = _lane((tq, LANES))
    o_ref[0] = jnp.concatenate([jnp.where(lane < HEAD_DIM, outs[e], pltpu.roll(outs[e + 1], HEAD_DIM, 1))
                                for e in range(0, nh, 2)], axis=1)


def _dense_attention(q, k, v, *, seq, tq=256, tk=512, nh=2):
    b = q.shape[0]
    heads = q.shape[2] // LANES
    tk = min(tk, seq)
    tq = min(tq, tk)
    return pl.pallas_call(
        functools.partial(_dense_kernel, tq=tq, tk=tk, nh=nh),
        grid=(b, heads // nh, seq // tq),
        in_specs=[pl.BlockSpec((1, tq, nh * LANES), lambda bi, p, i: (bi, i, p)),
                  pl.BlockSpec((1, seq, nh * LANES), lambda bi, p, i: (bi, 0, p)),
                  pl.BlockSpec((1, seq, nh * LANES), lambda bi, p, i: (bi, 0, p))],
        out_specs=pl.BlockSpec((1, tq, nh * HEAD_DIM), lambda bi, p, i: (bi, i, p)),
        out_shape=jax.ShapeDtypeStruct((b, seq, heads * HEAD_DIM), F32),
        scratch_shapes=[pltpu.VMEM((nh, tq, tk), F32), pltpu.VMEM((nh, tq, tk), F32)],
        compiler_params=_params(("arbitrary", "arbitrary", "arbitrary")),
        name="dense_causal_attention",
    )(q, k, v)


def _decay_kernel(f_ref, b_ref, o_ref):
    x = f_ref[0] + b_ref[...]
    lf = jnp.minimum(x, 0.0) - jnp.log1p(jnp.exp(-jnp.abs(x)))
    n = lf.shape[-1]
    lane = _lane(lf.shape)
    d = 1
    while d < n:
        lf = lf + jnp.where(lane >= d, pltpu.roll(lf, d, 1), 0.0)
        d *= 2
    o_ref[0] = lf * LOG2E


def _decay_cumsum(f_t, bias):
    b, h, s = f_t.shape
    return pl.pallas_call(
        _decay_kernel,
        grid=(b,),
        in_specs=[pl.BlockSpec((1, h, s), lambda i: (i, 0, 0)), pl.BlockSpec((h, 1), lambda i: (0, 0))],
        out_specs=pl.BlockSpec((1, h, s), lambda i: (i, 0, 0)),
        out_shape=jax.ShapeDtypeStruct((b, h, s), F32),
        compiler_params=_params(("arbitrary",)),
        name="fox_decay_cumsum",
    )(f_t, bias.reshape(h, 1).astype(F32))


def _mla_prep_kernel(cq_ref, ckv_ref, misc_ref, gqa_ref, gkva_ref, wq_ref, wk_ref, wv_ref, gq_ref, gk_ref,
                     cos_ref, sin_ref, q_ref, k_ref, v_ref, *, scale):
    tm = cq_ref.shape[0]
    lane = _lane((tm, LANES))
    in_rope = (lane >= MLA_NOPE_DIM) & (lane < MLA_QK_DIM)
    first = lane < MLA_NOPE_DIM + MLA_ROPE_DIM // 2
    cos, sin = cos_ref[...], sin_ref[...]

    def rope_tail(x):
        sw = jnp.where(first, pltpu.roll(x, LANES - MLA_ROPE_DIM // 2, 1), pltpu.roll(x, MLA_ROPE_DIM // 2, 1))
        return x * cos + jnp.where(in_rope, sw, 0.0) * sin

    cq = _rms(cq_ref[...], gqa_ref[...], MLA_Q_RANK).astype(BF16)
    ckv = _rms(ckv_ref[...], gkva_ref[...], MLA_KV_RANK).astype(BF16)
    qa = _dot(cq, wq_ref[...])
    ka = _dot(ckv, wk_ref[...])
    k_rope = jnp.where(in_rope, misc_ref[...], 0.0)
    for h in range(MLA_HEADS):
        qh = _rms(qa[:, h * LANES:(h + 1) * LANES], gq_ref[...], MLA_QK_DIM)
        q_ref[:, h * LANES:(h + 1) * LANES] = (rope_tail(qh) * scale).astype(q_ref.dtype)
        kh = _rms(ka[:, h * LANES:(h + 1) * LANES] + k_rope, gk_ref[...], MLA_QK_DIM)
        k_ref[:, h * LANES:(h + 1) * LANES] = rope_tail(kh).astype(k_ref.dtype)
    v = _dot(ckv, wv_ref[...])
    ones_col = (_lane(v.shape) & (LANES - 1)) == MLA_V_DIM
    v_ref[...] = jnp.where(ones_col, 1.0, v).astype(v_ref.dtype)


def _mla_prep(y, cq_blk, ckv_blk, misc_blk, q_a_norm, w_q_b, kv_a_norm, w_kv_b, q_norm, k_norm, *, seq, tm=512):
    n = y.shape[0]
    h = MLA_HEADS
    pad = LANES - MLA_QK_DIM
    wq = jnp.pad(w_q_b.reshape(MLA_Q_RANK, h, MLA_QK_DIM), ((0, 0), (0, 0), (0, pad)))
    wq = wq.reshape(MLA_Q_RANK, h * LANES).astype(BF16)
    wkv = w_kv_b.reshape(MLA_KV_RANK, h, MLA_NOPE_DIM + MLA_V_DIM)
    wk = jnp.pad(wkv[:, :, :MLA_NOPE_DIM], ((0, 0), (0, 0), (0, LANES - MLA_NOPE_DIM)))
    wk = wk.reshape(MLA_KV_RANK, h * LANES).astype(BF16)
    wv = jnp.pad(wkv[:, :, MLA_NOPE_DIM:], ((0, 0), (0, 0), (0, LANES - MLA_V_DIM)))
    wv = wv.reshape(MLA_KV_RANK, h * LANES).astype(BF16)
    gq = jnp.pad(q_norm, (0, pad)).reshape(1, LANES)
    gk = jnp.pad(k_norm, (0, pad)).reshape(1, LANES)
    half = MLA_ROPE_DIM // 2
    inv_freq = 1.0 / (ROPE_THETA ** (jnp.arange(0, MLA_ROPE_DIM, 2, dtype=F32) / MLA_ROPE_DIM))
    ang = jnp.arange(seq, dtype=F32)[:, None] * inv_freq[None, :]
    cos, sin = jnp.cos(ang), jnp.sin(ang)
    ones = jnp.ones((seq, MLA_NOPE_DIM), F32)
    zeros = jnp.zeros((seq, MLA_NOPE_DIM), F32)
    cos_t = jnp.concatenate([ones, cos, cos, ones[:, :pad]], axis=1)
    sin_t = jnp.concatenate([zeros, -sin, sin, zeros[:, :pad]], axis=1)
    del half
    sblocks = seq // tm
    full = lambda shp: pl.BlockSpec(shp, lambda i: (0,) * len(shp))
    q, k, v = pl.pallas_call(
        functools.partial(_mla_prep_kernel, scale=MLA_QK_DIM ** -0.5 * LOG2E),
        grid=(n // tm,),
        in_specs=[pl.BlockSpec((tm, MLA_Q_RANK), lambda i: (i, cq_blk // 2)),
                  pl.BlockSpec((tm, LANES), lambda i: (i, ckv_blk)),
                  pl.BlockSpec((tm, LANES), lambda i: (i, misc_blk)),
                  full((1, MLA_Q_RANK)), full((1, MLA_KV_RANK)),
                  full((MLA_Q_RANK, h * LANES)), full((MLA_KV_RANK, h * LANES)), full((MLA_KV_RANK, h * LANES)),
                  full((1, LANES)), full((1, LANES)),
                  pl.BlockSpec((tm, LANES), lambda i: (i % sblocks, 0)),
                  pl.BlockSpec((tm, LANES), lambda i: (i % sblocks, 0))],
        out_specs=[pl.BlockSpec((tm, h * LANES), lambda i: (i, 0)),
                   pl.BlockSpec((tm, h * LANES), lambda i: (i, 0)),
                   pl.BlockSpec((tm, h * LANES), lambda i: (i, 0))],
        out_shape=[jax.ShapeDtypeStruct((n, h * LANES), BF16),
                   jax.ShapeDtypeStruct((n, h * LANES), BF16),
                   jax.ShapeDtypeStruct((n, h * LANES), BF16)],
        compiler_params=_params(("arbitrary",)),
        name="mla_prep",
    )(y, y, y, q_a_norm.reshape(1, -1), kv_a_norm.reshape(1, -1), wq, wk, wv, gq, gk, cos_t, sin_t)
    return q, k, v


def _cols(w, a, b):
    return w[:, a:b]


def _nsa_swa_mixer(x, batch, seq, mix_norm, w_in, nsa_gate_b, nsa_q_norm, nsa_kc_norm, nsa_ks_norm, nsa_kw_norm,
                   cmp_pos_k, cmp_pos_v, cmpk_w1, cmpk_w2, cmpv_w1, cmpv_w2,
                   swa_q_norm, swa_k_norm, swa_sinks, w_out):
    n = batch * seq
    o = np.cumsum([0, 512, 128, 128, 128, 128, 128, 128, 24, 512, 128, 128])
    seg = lambda j: _cols(w_in, o[j], o[j + 1])
    q_a, kc, vc, ks, vs, kw, vw, gl, q_b, k_b, v_b = [seg(j) for j in range(11)]
    gl = jnp.pad(gl, ((0, 0), (0, LANES - gl.shape[1])))
    w = jnp.concatenate([q_a, q_b, ks, kw, k_b, kc, vc, vs, vw, v_b, gl], axis=1).astype(BF16)
    y = _rms_matmul(x, mix_norm, w)
    cos_t, sin_t = _rope_tables(seq)

    gq = jnp.stack([_pair_gain(nsa_q_norm)] * 4 + [_pair_gain(swa_q_norm)] * 4)
    dst = np.array([[c // 2, c // 2] for c in range(4)] * 2, np.int32)
    q_rot = _prep(y, 0, 8, seq, cos_t, sin_t, gains=gq, rope=True, scale=HEAD_DIM ** -0.5 * LOG2E, dst=dst)
    q_cmp = _prep(y, 0, 4, seq, cos_t, sin_t, gains=gq[:4], scale=HEAD_DIM ** -0.5 * LOG2E, dst=dst[:4])
    gk = jnp.stack([_pair_gain(nsa_ks_norm), _pair_gain(nsa_kw_norm), _pair_gain(swa_k_norm)])
    k_rot = _prep(y, 8, 3, seq, cos_t, sin_t, gains=gk, rope=True)
    v_slc = _prep(y, 13, 1, seq, cos_t, sin_t, vexp=True)
    v_all = _prep(y, 14, 2, seq, cos_t, sin_t)

    y3 = y.reshape(batch, seq, y.shape[1])
    k_cmp = _compress(y3[:, :, 11 * LANES:12 * LANES], cmp_pos_k, cmpk_w1, cmpk_w2, nsa_kc_norm)
    v_cmp = _compress(y3[:, :, 12 * LANES:13 * LANES], cmp_pos_v, cmpv_w1, cmpv_w2, None)
    gates = y3[:, :, 16 * LANES:17 * LANES]

    q_rot3 = q_rot.reshape(batch, seq, -1)
    q_cmp3 = q_cmp.reshape(batch, seq, -1)
    k3 = k_rot.reshape(batch, seq, -1)
    v3 = v_all.reshape(batch, seq, -1)
    k_slc, k_win, k_swa = (k3[:, :, j * LANES:(j + 1) * LANES] for j in range(3))
    v_win, v_swa = (v3[:, :, j * LANES:(j + 1) * LANES] for j in range(2))
    block_of_key = np.arange(seq)[:, None] // SLC_BLOCK == np.arange(LANES)[None, :]
    big_onehot = jnp.asarray(np.where(block_of_key, BIG, 0.0), BF16)
    k_aug = jnp.concatenate([k_slc, jnp.broadcast_to(big_onehot, (batch, seq, LANES))], axis=-1)
    gate_b = nsa_gate_b.astype(F32)

    o_cmp, sel = _nsa_compressed(q_cmp3, k_cmp, v_cmp, gates, gate_b, seq=seq)
    o_slc = _nsa_selected(q_rot3, k_aug, v_slc.reshape(batch, seq, -1), sel, gates, gate_b, seq=seq)
    o_win = _window_attention(q_rot3, 0, k_win, v_win, gate_b, gates, seq=seq, window=NSA_WINDOW,
                              gated=True, sinks=False)
    o_swa = _window_attention(q_rot3, 8, k_swa, v_swa, swa_sinks.astype(F32), gates, seq=seq, window=SWA_WINDOW,
                              gated=False, sinks=True)
    flat = lambda a: a.reshape(n, -1)
    return _outproj(x, [flat(o_cmp), flat(o_slc), flat(o_win)], [flat(o_swa)], w_out)


def _fox_mla_mixer(x, batch, seq, mix_norm, w_in, fox_f_bias, fox_q_norm, fox_k_norm, mla_q_a_norm, mla_w_q_b,
                   mla_kv_a_norm, mla_w_kv_b, mla_q_norm, mla_k_norm, w_out):
    n = batch * seq
    o = np.cumsum([0, 512, 512, 512, 8, 256, 128, 32])
    seg = lambda j: _cols(w_in, o[j], o[j + 1])
    q_c, k_c, v_c, f_c, c_q, c_kv, k_r = [seg(j) for j in range(7)]
    d = w_in.shape[0]
    misc = jnp.concatenate([f_c, jnp.zeros((d, MLA_NOPE_DIM - 8), w_in.dtype), k_r,
                            jnp.zeros((d, LANES - MLA_QK_DIM), w_in.dtype)], axis=1)
    w = jnp.concatenate([q_c, k_c, v_c, c_q, c_kv, misc], axis=1).astype(BF16)
    y = _rms_matmul(x, mix_norm, w)
    cos_t, sin_t = _rope_tables(seq)

    y3 = y.reshape(batch, seq, y.shape[1])
    f_t = y3[:, :, 15 * LANES:15 * LANES + FOX_HEADS].transpose(0, 2, 1)
    dc = _decay_cumsum(f_t, fox_f_bias)
    dc_tok = jnp.pad(dc.transpose(0, 2, 1).reshape(n, FOX_HEADS), ((0, 0), (0, LANES - FOX_HEADS)))
    low = np.zeros((4, 2), np.int32)
    q_f = _prep(y, 0, 4, seq, cos_t, sin_t, gains=jnp.stack([_pair_gain(fox_q_norm)] * 4),
                scale=HEAD_DIM ** -0.5 * LOG2E, dst=low, aug="ones")
    k_f = _prep(y, 4, 4, seq, cos_t, sin_t, gains=jnp.stack([_pair_gain(fox_k_norm)] * 4), dst=low,
                aug="decay", aux=dc_tok)
    v_f = _prep(y, 8, 4, seq, cos_t, sin_t, vexp=True)
    b3 = lambda a: a.reshape(batch, seq, -1)
    o_fox = _dense_attention(b3(q_f), b3(k_f), b3(v_f), seq=seq)

    q_m, k_m, v_m = _mla_prep(y, 12, 14, 15, mla_q_a_norm, mla_w_q_b, mla_kv_a_norm, mla_w_kv_b,
                              mla_q_norm, mla_k_norm, seq=seq)
    o_mla = _dense_attention(b3(q_m), b3(k_m), b3(v_m), seq=seq)
    flat = lambda a: a.reshape(n, -1)
    return _outproj(x, [flat(o_fox)], [flat(o_mla)], w_out)


def kernel(x, l0_ffn1_norm, l0_ffn1_w_gate, l0_ffn1_w_up, l0_ffn1_w_down, l0_mix_norm, l0_w_in, l0_nsa_gate_b, l0_nsa_q_norm, l0_nsa_kc_norm, l0_nsa_ks_norm, l0_nsa_kw_norm, l0_cmp_pos_k, l0_cmp_pos_v, l0_cmpk_w1, l0_cmpk_w2, l0_cmpv_w1, l0_cmpv_w2, l0_swa_q_norm, l0_swa_k_norm, l0_swa_sinks, l0_w_out, l0_ffn2_norm, l0_ffn2_w_gate, l0_ffn2_w_up, l0_ffn2_w_down, l1_ffn1_norm, l1_ffn1_w_gate, l1_ffn1_w_up, l1_ffn1_w_down, l1_mix_norm, l1_w_in, l1_fox_f_bias, l1_fox_q_norm, l1_fox_k_norm, l1_mla_q_a_norm, l1_mla_w_q_b, l1_mla_kv_a_norm, l1_mla_w_kv_b, l1_mla_q_norm, l1_mla_k_norm, l1_w_out, l1_ffn2_norm, l1_ffn2_w_gate, l1_ffn2_w_up, l1_ffn2_w_down):
    batch, seq, d = x.shape
    h = x.reshape(batch * seq, d)
    h = _ffn(h, l0_ffn1_norm, l0_ffn1_w_gate, l0_ffn1_w_up, l0_ffn1_w_down)
    h = _nsa_swa_mixer(h, batch, seq, l0_mix_norm, l0_w_in, l0_nsa_gate_b, l0_nsa_q_norm, l0_nsa_kc_norm,
                       l0_nsa_ks_norm, l0_nsa_kw_norm, l0_cmp_pos_k, l0_cmp_pos_v, l0_cmpk_w1, l0_cmpk_w2,
                       l0_cmpv_w1, l0_cmpv_w2, l0_swa_q_norm, l0_swa_k_norm, l0_swa_sinks, l0_w_out)
    h = _ffn(h, l0_ffn2_norm, l0_ffn2_w_gate, l0_ffn2_w_up, l0_ffn2_w_down)
    h = _ffn(h, l1_ffn1_norm, l1_ffn1_w_gate, l1_ffn1_w_up, l1_ffn1_w_down)
    h = _fox_mla_mixer(h, batch, seq, l1_mix_norm, l1_w_in, l1_fox_f_bias, l1_fox_q_norm, l1_fox_k_norm,
                       l1_mla_q_a_norm, l1_mla_w_q_b, l1_mla_kv_a_norm, l1_mla_w_kv_b, l1_mla_q_norm,
                       l1_mla_k_norm, l1_w_out)
    h = _ffn(h, l1_ffn2_norm, l1_ffn2_w_gate, l1_ffn2_w_up, l1_ffn2_w_down)
    return h.reshape(batch, seq, d)
```

```python
import functools
from typing import NamedTuple, Optional

import numpy as np
import jax
import jax.numpy as jnp
from jax import lax
from jax.experimental import pallas as pl
from jax.experimental.pallas import tpu as pltpu

F32 = jnp.float32
BF16 = jnp.bfloat16

HEAD_DIM = 64
LANES = 128
ROPE_THETA = 10000.0
RMS_EPS = 1e-6
NEG_INF = -1e30
FORCE_SCORE = 1e9
BELOW_ALL = -3e38
LOG2E = 1.4426950408889634
BIG = 1e30
VT_ROWS = 80

NSA_HEADS = 8
NSA_KV_HEADS = 2
CMP_BLOCK = 32
CMP_STRIDE = 16
CMP_HIDDEN = 256
SLC_BLOCK = 64
N_SELECT = 16
NSA_WINDOW = 512
SWA_HEADS = 8
SWA_KV_HEADS = 2
SWA_WINDOW = 128
FOX_HEADS = 8
MLA_HEADS = 8
MLA_Q_RANK = 256
MLA_KV_RANK = 128
MLA_NOPE_DIM = 64
MLA_ROPE_DIM = 32
MLA_V_DIM = 64
MLA_QK_DIM = MLA_NOPE_DIM + MLA_ROPE_DIM

VMEM_LIMIT = 48 * 1024 * 1024

NT_DIMS = (((1,), (1,)), ((), ()))


def _params(sem):
    return pltpu.CompilerParams(dimension_semantics=sem, vmem_limit_bytes=VMEM_LIMIT)


def _dot(a, b):
    return jnp.dot(a, b, preferred_element_type=F32)


def _dot_nt(a, b):
    return lax.dot_general(a, b, NT_DIMS, preferred_element_type=F32)


def _rms(x, gain, n):
    ms = jnp.sum(x * x, axis=-1, keepdims=True) * (1.0 / n)
    return (x * lax.rsqrt(ms + RMS_EPS)) * gain


def _lane(shape):
    return lax.broadcasted_iota(jnp.int32, shape, len(shape) - 1)


def _ffn_kernel(x_ref, g_ref, wg_ref, wu_ref, wd_ref, o_ref, h_sc, acc_sc, *, nf):
    x = x_ref[...]
    h_sc[...] = _rms(x, g_ref[...], x.shape[-1]).astype(BF16)
    acc_sc[...] = jnp.zeros_like(acc_sc)

    def body(c, carry):
        h = h_sc[...]
        g = _dot(h, wg_ref[c])
        u = _dot(h, wu_ref[c])
        a = (g * (1.0 / (1.0 + jnp.exp(-g)))) * u
        acc_sc[...] += _dot(a.astype(BF16), wd_ref[c])
        return carry

    lax.fori_loop(0, nf, body, 0, unroll=True)
    o_ref[...] = x + 0.5 * acc_sc[...]


def _ffn(x, norm, w_gate, w_up, w_down, *, tm=512, tf=256):
    n, d = x.shape
    f = w_gate.shape[1]
    nf = f // tf
    wg = w_gate.astype(BF16).reshape(d, nf, tf).transpose(1, 0, 2)
    wu = w_up.astype(BF16).reshape(d, nf, tf).transpose(1, 0, 2)
    wd = w_down.astype(BF16).reshape(nf, tf, d)
    wspec = lambda shp: pl.BlockSpec(shp, lambda i: (0, 0, 0), pipeline_mode=pl.Buffered(1))
    return pl.pallas_call(
        functools.partial(_ffn_kernel, nf=nf),
        grid=(n // tm,),
        in_specs=[pl.BlockSpec((tm, d), lambda i: (i, 0)),
                  pl.BlockSpec((1, d), lambda i: (0, 0)),
                  wspec((nf, d, tf)), wspec((nf, d, tf)), wspec((nf, tf, d))],
        out_specs=pl.BlockSpec((tm, d), lambda i: (i, 0)),
        out_shape=jax.ShapeDtypeStruct((n, d), F32),
        scratch_shapes=[pltpu.VMEM((tm, d), BF16), pltpu.VMEM((tm, d), F32)],
        compiler_params=_params(("arbitrary",)),
        name="ffn",
    )(x, norm.reshape(1, d), wg, wu, wd)


def _rms_matmul_kernel(x_ref, g_ref, w_ref, o_ref):
    x = x_ref[...]
    h = _rms(x, g_ref[...], x.shape[-1]).astype(BF16)
    o_ref[...] = _dot(h, w_ref[...])


def _rms_matmul(x, norm, w, *, tm=512):
    n, d = x.shape
    c = w.shape[1]
    return pl.pallas_call(
        _rms_matmul_kernel,
        grid=(n // tm,),
        in_specs=[pl.BlockSpec((tm, d), lambda i: (i, 0)),
                  pl.BlockSpec((1, d), lambda i: (0, 0)),
                  pl.BlockSpec((d, c), lambda i: (0, 0), pipeline_mode=pl.Buffered(1))],
        out_specs=pl.BlockSpec((tm, c), lambda i: (i, 0)),
        out_shape=jax.ShapeDtypeStruct((n, c), F32),
        compiler_params=_params(("arbitrary",)),
        name="rms_matmul",
    )(x, norm.reshape(1, d), w)


def _outproj_kernel(*refs, n_a, n_b):
    x_ref = refs[0]
    a_refs = refs[1:1 + n_a]
    b_refs = refs[1 + n_a:1 + n_a + n_b]
    wa_ref, wb_ref, o_ref = refs[1 + n_a + n_b:]
    a = a_refs[0][...]
    for r in a_refs[1:]:
        a = a + r[...]
    b = b_refs[0][...]
    for r in b_refs[1:]:
        b = b + r[...]
    o_ref[...] = x_ref[...] + _dot(a.astype(BF16), wa_ref[...]) + _dot(b.astype(BF16), wb_ref[...])


def _outproj(x, a_list, b_list, w_out, *, tm=512):
    n, d = x.shape
    ca = a_list[0].shape[1]
    cb = b_list[0].shape[1]
    wa = w_out[:ca].astype(BF16)
    wb = w_out[ca:].astype(BF16)
    row = lambda c: pl.BlockSpec((tm, c), lambda i: (i, 0))
    return pl.pallas_call(
        functools.partial(_outproj_kernel, n_a=len(a_list), n_b=len(b_list)),
        grid=(n // tm,),
        in_specs=[row(d)] + [row(ca)] * len(a_list) + [row(cb)] * len(b_list)
                 + [pl.BlockSpec((ca, d), lambda i: (0, 0)), pl.BlockSpec((cb, d), lambda i: (0, 0))],
        out_specs=row(d),
        out_shape=jax.ShapeDtypeStruct((n, d), F32),
        compiler_params=_params(("arbitrary",)),
        name="outproj",
    )(x, *a_list, *b_list, wa, wb)


class _Job(NamedTuple):
    blk: int
    out: int
    col: int
    gain: Optional[int] = None
    rope: bool = False
    scale: float = 1.0
    mode: str = "plain"
    dst: tuple = (0, 0)
    aug: Optional[str] = None
    heads: tuple = (0, 0)


def _prep_kernel(y_ref, gain_ref, cos_ref, sin_ref, aux_ref, *o_refs, jobs, seq):
    tm = y_ref.shape[0]
    lane = _lane((tm, LANES))
    low = lane < HEAD_DIM
    zero = jnp.zeros((tm, LANES), F32)
    for job in jobs:
        x = y_ref[:, job.blk * LANES:(job.blk + 1) * LANES]
        if job.gain is not None:
            x2 = x * x
            s_lo = jnp.sum(jnp.where(low, x2, 0.0), axis=-1, keepdims=True)
            s_hi = jnp.sum(jnp.where(low, 0.0, x2), axis=-1, keepdims=True)
            ms = jnp.where(low, s_lo, s_hi) * (1.0 / HEAD_DIM)
            x = (x * lax.rsqrt(ms + RMS_EPS)) * gain_ref[job.gain]
        if job.rope:
            swapped = jnp.where((lane & (HEAD_DIM - 1)) < HEAD_DIM // 2,
                                pltpu.roll(x, LANES - HEAD_DIM // 2, 1), pltpu.roll(x, HEAD_DIM // 2, 1))
            x = x * cos_ref[...] + swapped * sin_ref[...]
        if job.scale != 1.0:
            x = x * job.scale
        if job.mode == "plain":
            pieces = [x]
        elif job.mode == "kaug":
            pos = (pl.program_id(0) * tm + lax.broadcasted_iota(jnp.int32, (tm, LANES), 0)) % seq
            pieces = [x, jnp.where(lane == pos // SLC_BLOCK, BIG, 0.0)]
        elif job.mode == "v":
            r = pltpu.roll(x, HEAD_DIM, 1)
            tail = jnp.where(lane == HEAD_DIM, 1.0, 0.0)
            pieces = [jnp.where(low, x, tail), jnp.where(low, r, tail)]
        else:
            r = pltpu.roll(x, HEAD_DIM, 1)
            h_even = jnp.where(low, x, zero) if job.dst[0] == 0 else jnp.where(low, zero, r)
            h_odd = jnp.where(low, r, zero) if job.dst[1] == 0 else jnp.where(low, zero, x)
            if job.aug is not None:
                tails = []
                for e in range(2):
                    if job.aug == "ones":
                        tails.append(jnp.where((lane >= HEAD_DIM) & (lane < HEAD_DIM + 3), 1.0, 0.0))
                    else:
                        d = jnp.sum(jnp.where(lane == job.heads[e], aux_ref[...], 0.0), axis=-1, keepdims=True)
                        hi = d.astype(BF16).astype(F32)
                        mid = (d - hi).astype(BF16).astype(F32)
                        lo = d - hi - mid
                        tails.append(jnp.where(lane == HEAD_DIM, -hi, jnp.where(lane == HEAD_DIM + 1, -mid,
                                     jnp.where(lane == HEAD_DIM + 2, -lo, 0.0))))
                h_even = jnp.where(low, h_even, tails[0])
                h_odd = jnp.where(low, h_odd, tails[1])
            pieces = [h_even, h_odd]
        o_ref = o_refs[job.out]
        for n, piece in enumerate(pieces):
            o_ref[:, (job.col + n) * LANES:(job.col + n + 1) * LANES] = piece.astype(o_ref.dtype)


def _prep(y, jobs, outs, gains, seq, cos_t, sin_t, *, aux=None, tm=512):
    n, c = y.shape
    aux_spec = pl.BlockSpec((tm, LANES), lambda i: (i, 0))
    if aux is None:
        aux, aux_spec = jnp.zeros((tm, LANES), F32), pl.BlockSpec((tm, LANES), lambda i: (0, 0))
    sblocks = seq // tm
    return pl.pallas_call(
        functools.partial(_prep_kernel, jobs=tuple(jobs), seq=seq),
        grid=(n // tm,),
        in_specs=[pl.BlockSpec((tm, c), lambda i: (i, 0)),
                  pl.BlockSpec(gains.shape, lambda i: (0, 0, 0)),
                  pl.BlockSpec((tm, LANES), lambda i: (i % sblocks, 0)),
                  pl.BlockSpec((tm, LANES), lambda i: (i % sblocks, 0)),
                  aux_spec],
        out_specs=[pl.BlockSpec((tm, w * LANES), lambda i: (i, 0)) for w, _ in outs],
        out_shape=[jax.ShapeDtypeStruct((n, w * LANES), dt) for w, dt in outs],
        compiler_params=_params(("arbitrary",)),
        name="head_prep",
    )(y, gains, cos_t, sin_t, aux)


def _pair_gain(g):
    return jnp.concatenate([g, g]).reshape(1, LANES).astype(F32)


def _rope_tables(seq):
    half = HEAD_DIM // 2
    inv_freq = 1.0 / (ROPE_THETA ** (jnp.arange(0, HEAD_DIM, 2, dtype=F32) / HEAD_DIM))
    ang = jnp.arange(seq, dtype=F32)[:, None] * inv_freq[None, :]
    cos, sin = jnp.cos(ang), jnp.sin(ang)
    cos_t = jnp.concatenate([cos, cos, cos, cos], axis=1)
    sin_t = jnp.concatenate([-sin, sin, -sin, sin], axis=1)
    del half
    return cos_t, sin_t


def _compress_kernel(ch_ref, ptop_ref, pbot_ref, w1t_ref, w1b_ref, w2_ref, gain_ref, o_ref, *, norm):
    ch = ch_ref[0]
    a = _dot((ch + ptop_ref[...]).astype(BF16), w1t_ref[...])
    b = _dot((ch + pbot_ref[...]).astype(BF16), w1b_ref[...])
    nc = a.shape[0]
    hid = a + pltpu.roll(b, nc - 1, 0)
    act = hid * (1.0 / (1.0 + jnp.exp(-hid)))
    out = _dot(act.astype(BF16), w2_ref[...])
    if norm:
        lane = _lane(out.shape)
        low = lane < HEAD_DIM
        o2 = out * out
        s_lo = jnp.sum(jnp.where(low, o2, 0.0), axis=-1, keepdims=True)
        s_hi = jnp.sum(jnp.where(low, 0.0, o2), axis=-1, keepdims=True)
        ms = jnp.where(low, s_lo, s_hi) * (1.0 / HEAD_DIM)
        out = (out * lax.rsqrt(ms + RMS_EPS)) * gain_ref[...]
    o_ref[0] = out.astype(o_ref.dtype)


def _compress(t_pair, pos_emb, w1, w2, gain):
    b, s, _ = t_pair.shape
    nc = s // CMP_STRIDE
    hid = w1.shape[1]
    ch = t_pair.reshape(b, nc, CMP_STRIDE * LANES)
    eye2 = jnp.eye(2, dtype=F32)
    w1r = w1.reshape(CMP_BLOCK, HEAD_DIM, hid)
    def expand_w1(w):
        return jnp.einsum('pdj,kl->pkdlj', w, eye2).reshape(CMP_STRIDE * LANES, 2 * hid).astype(BF16)
    w1t, w1b = expand_w1(w1r[:CMP_STRIDE]), expand_w1(w1r[CMP_STRIDE:])
    w2e = jnp.einsum('jd,kl->kjld', w2, eye2).reshape(2 * hid, LANES).astype(BF16)
    def expand_pos(p):
        return jnp.broadcast_to(p[:, None, :], (CMP_STRIDE, 2, HEAD_DIM)).reshape(1, CMP_STRIDE * LANES)
    ptop, pbot = expand_pos(pos_emb[:CMP_STRIDE]), expand_pos(pos_emb[CMP_STRIDE:])
    norm = gain is not None
    g = _pair_gain(gain) if norm else jnp.ones((1, LANES), F32)
    full = lambda shp: pl.BlockSpec(shp, lambda i: (0,) * len(shp))
    return pl.pallas_call(
        functools.partial(_compress_kernel, norm=norm),
        grid=(b,),
        in_specs=[pl.BlockSpec((1, nc, CMP_STRIDE * LANES), lambda i: (i, 0, 0)),
                  full((1, CMP_STRIDE * LANES)), full((1, CMP_STRIDE * LANES)),
                  full((CMP_STRIDE * LANES, 2 * hid)), full((CMP_STRIDE * LANES, 2 * hid)),
                  full((2 * hid, LANES)), full((1, LANES))],
        out_specs=pl.BlockSpec((1, nc, LANES), lambda i: (i, 0, 0)),
        out_shape=jax.ShapeDtypeStruct((b, nc, LANES), BF16),
        compiler_params=_params(("arbitrary",)),
        name="nsa_compress",
    )(ch, ptop, pbot, w1t, w1b, w2e, g)


def _stack_heads(q, n):
    return jnp.concatenate([q[:, g * LANES:(g + 1) * LANES] for g in range(n)], axis=0)


def _gate_column(gl, col):
    lane = _lane(gl.shape)
    return jnp.sum(jnp.where(lane == col, gl, 0.0), axis=-1, keepdims=True)


def _sigmoid(x):
    return 1.0 / (1.0 + jnp.exp(-x))


def _compact_group(heads, hk):
    tq = heads[0].shape[0]
    lane = _lane((tq, LANES))
    low = lane < HEAD_DIM
    outs = []
    for e in range(0, len(heads), 2):
        he, ho = heads[e], heads[e + 1]
        ho_r = pltpu.roll(ho, HEAD_DIM, 1)
        if hk is None:
            lo_part, hi_part = he, ho_r
        else:
            at_low = jnp.broadcast_to(hk, (tq, LANES)) == 0
            lo_part = jnp.where(at_low, he, pltpu.roll(he, HEAD_DIM, 1))
            hi_part = jnp.where(at_low, ho_r, ho)
        outs.append(jnp.where(low, lo_part, hi_part))
    return jnp.concatenate(outs, axis=1)


def _nsa_cmp_kernel(gb_ref, q_ref, k_ref, v_ref, ov_ref, gl_ref, o_ref, sel_ref, *, tq, group, n_sel, ns):
    hk = pl.program_id(1)
    i = pl.program_id(2)
    q4 = _stack_heads(q_ref[0], group)
    kc = k_ref[0]
    ncp = kc.shape[0]
    logits = _dot_nt(q4, kc).reshape(group, tq, ncp)
    t = i * tq + lax.broadcasted_iota(jnp.int32, (tq, ncp), 0)
    cmp_end = lax.broadcasted_iota(jnp.int32, (tq, ncp), 1) * CMP_STRIDE + (CMP_BLOCK - 1)
    logits = jnp.where((cmp_end <= t)[None], logits, NEG_INF)
    m = jnp.max(logits, axis=-1, keepdims=True)
    e = jnp.exp2(logits - m)
    t_row = i * tq + lax.broadcasted_iota(jnp.int32, (tq, 1), 0)
    seen = jnp.where(t_row >= CMP_BLOCK - 1, 1.0, 0.0)[None]
    inv = seen / jnp.maximum(jnp.sum(e, axis=-1, keepdims=True), 1e-30)
    p = e * inv
    o4 = _dot(p.reshape(group * tq, ncp).astype(BF16), v_ref[0])
    gl = gl_ref[0]
    heads = []
    for g in range(group):
        col = (hk * group + g) * 3
        gate = _sigmoid(_gate_column(gl, col) + gb_ref[col])
        heads.append(o4[g * tq:(g + 1) * tq] * gate)
    o_ref[0] = _compact_group(heads, hk)

    ps = jnp.sum(p, axis=0)
    ps_hi = ps.astype(BF16)
    ps_lo = (ps - ps_hi.astype(F32)).astype(BF16)
    imp = _dot(ps_hi, ov_ref[...]) + _dot(ps_lo, ov_ref[...])
    imp_t = imp.T
    blk = lax.broadcasted_iota(jnp.int32, (LANES, tq), 0)
    cur = (i * tq + lax.broadcasted_iota(jnp.int32, (LANES, tq), 1)) // SLC_BLOCK
    forced = (blk == 0) | (blk == cur) | (blk == cur - 1)
    score = jnp.where(forced, FORCE_SCORE, jnp.where(blk <= cur, imp_t, NEG_INF))
    score = jnp.where(blk < ns, score, BELOW_ALL)
    blk_f = blk.astype(F32)

    def pick(_, carry):
        sc, sel = carry
        mx = jnp.max(sc, axis=0, keepdims=True)
        first = jnp.min(jnp.where(sc == mx, blk_f, float(LANES)), axis=0, keepdims=True)
        hit = blk_f == first
        return jnp.where(hit, BELOW_ALL, sc), jnp.where(hit, 1.0, sel)

    _, sel = lax.fori_loop(0, n_sel, pick, (score, jnp.zeros((LANES, tq), F32)))
    sel_ref[0, 0] = jnp.where(blk <= cur, sel, 0.0).T.astype(sel_ref.dtype)


def _nsa_compressed(q_cmp, k_cmp, v_cmp, gates, gate_b, *, seq, tq=256):
    b = q_cmp.shape[0]
    group = NSA_HEADS // NSA_KV_HEADS
    ncp = k_cmp.shape[1]
    ns = seq // SLC_BLOCK
    n_sel = min(N_SELECT, ns)
    c_start = np.arange(ncp)[:, None] * CMP_STRIDE
    s_start = np.arange(LANES)[None, :] * SLC_BLOCK
    overlap = np.maximum(np.minimum(c_start + CMP_BLOCK, s_start + SLC_BLOCK) - np.maximum(c_start, s_start), 0)
    overlap = np.where((np.arange(LANES)[None, :] < ns) & (np.arange(ncp)[:, None] < ncp - 1), overlap, 0)
    overlap = jnp.asarray(overlap, BF16)
    gw = group * LANES
    tq = min(tq, seq)
    return pl.pallas_call(
        functools.partial(_nsa_cmp_kernel, tq=tq, group=group, n_sel=n_sel, ns=ns),
        grid=(b, NSA_KV_HEADS, seq // tq),
        in_specs=[pl.BlockSpec(memory_space=pltpu.SMEM),
                  pl.BlockSpec((1, tq, gw), lambda bi, h, i: (bi, i, h)),
                  pl.BlockSpec((1, ncp, LANES), lambda bi, h, i: (bi, 0, 0)),
                  pl.BlockSpec((1, ncp, LANES), lambda bi, h, i: (bi, 0, 0)),
                  pl.BlockSpec((ncp, LANES), lambda bi, h, i: (0, 0)),
                  pl.BlockSpec((1, tq, LANES), lambda bi, h, i: (bi, i, 0))],
        out_specs=[pl.BlockSpec((1, tq, group * HEAD_DIM), lambda bi, h, i: (bi, i, h)),
                   pl.BlockSpec((1, 1, tq, LANES), lambda bi, h, i: (bi, h, i, 0))],
        out_shape=[jax.ShapeDtypeStruct((b, seq, NSA_HEADS * HEAD_DIM), F32),
                   jax.ShapeDtypeStruct((b, NSA_KV_HEADS, seq, LANES), BF16)],
        compiler_params=_params(("arbitrary", "arbitrary", "arbitrary")),
        name="nsa_compressed_select",
    )(gate_b, q_cmp, k_cmp, v_cmp, overlap, gates)


def _nsa_slc_kernel(gb_ref, q_ref, k_ref, v_ref, sel_ref, gl_ref, o_ref, sa_ref, sb_ref, *, tq, tk, group):
    hk = pl.program_id(1)
    i = pl.program_id(2)
    q = q_ref[0]
    unsel = (sel_ref[0, 0].astype(F32) - 1.0).astype(BF16)
    lhs = [jnp.concatenate([jnp.concatenate([q[:, g * LANES:(g + 1) * LANES], unsel], axis=1)
                            for g in (2 * c, 2 * c + 1)], axis=0) for c in range(group // 2)]
    n_full = (i * tq) // tk
    chains = group // 2

    def logits_into(buf, j):
        off = pl.multiple_of(j * tk, tk)
        kt = k_ref[0, pl.ds(off, tk), :]
        for c in range(chains):
            buf[c] = _dot_nt(lhs[c], kt)

    def consume(buf, j, states, masked):
        off = pl.multiple_of(j * tk, tk)
        vt = v_ref[0, pl.ds(off, tk), :]
        out = []
        for c in range(chains):
            m, acc = states[c]
            s = buf[c]
            if masked:
                t = i * tq + lax.broadcasted_iota(jnp.int32, (tq, tk), 0)
                key = j * tk + lax.broadcasted_iota(jnp.int32, (tq, tk), 1)
                ok = key <= t
                s = jnp.where(jnp.concatenate([ok, ok], axis=0), s, NEG_INF)
            m_new = jnp.maximum(m, jnp.max(s, axis=-1, keepdims=True))
            p = jnp.exp2(s - m_new)
            out.append((m_new, jnp.exp2(m - m_new) * acc + _dot(p.astype(BF16), vt)))
        return tuple(out)

    def pair(jj, states):
        j = 2 * jj
        logits_into(sb_ref, j + 1)
        states = consume(sa_ref, j, states, False)
        logits_into(sa_ref, j + 2)
        return consume(sb_ref, j + 1, states, False)

    init = tuple((jnp.full((2 * tq, 1), NEG_INF, F32), jnp.zeros((2 * tq, LANES), F32)) for _ in range(chains))
    logits_into(sa_ref, 0)
    states = lax.fori_loop(0, n_full // 2, pair, init)
    r = 2 * (n_full // 2)

    def diagonal_is_next(states):
        return consume(sa_ref, r, states, True)

    def one_full_tile_left(states):
        logits_into(sb_ref, r + 1)
        return consume(sb_ref, r + 1, consume(sa_ref, r, states, False), True)

    carry = lax.cond(r == n_full, diagonal_is_next, one_full_tile_left, states)
    gl = gl_ref[0]
    heads = []
    for g in range(group):
        acc = carry[g // 2][1][(g % 2) * tq:(g % 2 + 1) * tq]
        colg = (hk * group + g) * 3 + 1
        gate = _sigmoid(_gate_column(gl, colg) + gb_ref[colg])
        heads.append(acc * (gate / acc[:, HEAD_DIM:HEAD_DIM + 1]))
    o_ref[0] = _compact_group(heads, None)


def _nsa_selected(q_rot, k_aug, v_exp, sel, gates, gate_b, *, seq, tq=128, tk=512):
    b = q_rot.shape[0]
    group = NSA_HEADS // NSA_KV_HEADS
    tk = min(tk, seq)
    gw = group * LANES
    return pl.pallas_call(
        functools.partial(_nsa_slc_kernel, tq=tq, tk=tk, group=group),
        grid=(b, NSA_KV_HEADS, seq // tq),
        in_specs=[pl.BlockSpec(memory_space=pltpu.SMEM),
                  pl.BlockSpec((1, tq, gw), lambda bi, h, i: (bi, i, h)),
                  pl.BlockSpec((1, seq, 2 * LANES), lambda bi, h, i: (bi, 0, 0)),
                  pl.BlockSpec((1, seq, LANES), lambda bi, h, i: (bi, 0, h)),
                  pl.BlockSpec((1, 1, tq, LANES), lambda bi, h, i: (bi, h, i, 0)),
                  pl.BlockSpec((1, tq, LANES), lambda bi, h, i: (bi, i, 0))],
        out_specs=pl.BlockSpec((1, tq, group * HEAD_DIM), lambda bi, h, i: (bi, i, h)),
        out_shape=jax.ShapeDtypeStruct((b, seq, NSA_HEADS * HEAD_DIM), F32),
        scratch_shapes=[pltpu.VMEM((group // 2, 2 * tq, tk), F32), pltpu.VMEM((group // 2, 2 * tq, tk), F32)],
        compiler_params=_params(("arbitrary", "arbitrary", "arbitrary")),
        name="nsa_selected",
    )(gate_b, q_rot, k_aug, v_exp, sel, gates)


def _window_kernel(sc_ref, q_ref, k_ref, v_ref, gl_ref, o_ref, *, tq, span, window, group, gated, sinks):
    hk = pl.program_id(1)
    i = pl.program_id(2)
    q4 = _stack_heads(q_ref[0], group)
    start = pl.multiple_of(jnp.maximum(i * tq + tq - span, 0), tq)
    kt = k_ref[0, pl.ds(start, span), :]
    vt = v_ref[0, pl.ds(start, span), :]
    s = _dot_nt(q4, kt).reshape(group, tq, span)
    t = i * tq + lax.broadcasted_iota(jnp.int32, (tq, span), 0)
    key = start + lax.broadcasted_iota(jnp.int32, (tq, span), 1)
    mask = ((key <= t) & (t - key < window))[None]
    s = jnp.where(mask, s, NEG_INF)
    m = jnp.max(s, axis=-1, keepdims=True)
    if sinks:
        sk = jnp.concatenate([jnp.full((1, 1, 1), sc_ref[hk * group + g] * LOG2E, F32) for g in range(group)],
                             axis=0)
        m = jnp.maximum(m, sk)
    e = jnp.where(mask, jnp.exp2(s - m), 0.0)
    denom = jnp.sum(e, axis=-1, keepdims=True)
    if sinks:
        denom = denom + jnp.exp2(sk - m)
    p = (e / denom).reshape(group * tq, span).astype(BF16)
    o4 = _dot(p, vt)
    heads = []
    for g in range(group):
        og = o4[g * tq:(g + 1) * tq]
        if gated:
            colg = (hk * group + g) * 3 + 2
            og = og * _sigmoid(_gate_column(gl_ref[0], colg) + sc_ref[colg])
        heads.append(og)
    o_ref[0] = _compact_group(heads, hk)


def _window_attention(q_rot, q_blk0, k_pairs, v_pairs, kv_blk, scalars, gates, *, seq, window, gated, sinks, tq=128):
    b = q_rot.shape[0]
    group = 4
    span = min(window + tq, seq)
    gw = group * LANES
    qb = q_blk0 // group
    return pl.pallas_call(
        functools.partial(_window_kernel, tq=tq, span=span, window=window, group=group, gated=gated, sinks=sinks),
        grid=(b, 2, seq // tq),
        in_specs=[pl.BlockSpec(memory_space=pltpu.SMEM),
                  pl.BlockSpec((1, tq, gw), lambda bi, h, i: (bi, i, qb + h)),
                  pl.BlockSpec((1, seq, LANES), lambda bi, h, i: (bi, 0, kv_blk)),
                  pl.BlockSpec((1, seq, LANES), lambda bi, h, i: (bi, 0, kv_blk)),
                  pl.BlockSpec((1, tq, LANES), lambda bi, h, i: (bi, i, 0))],
        out_specs=pl.BlockSpec((1, tq, group * HEAD_DIM), lambda bi, h, i: (bi, i, h)),
        out_shape=jax.ShapeDtypeStruct((b, seq, 8 * HEAD_DIM), F32),
        compiler_params=_params(("arbitrary", "arbitrary", "arbitrary")),
        name="window_attention",
    )(scalars, q_rot, k_pairs, v_pairs, gates)


def _dense_kernel(q_ref, k_ref, v_ref, o_ref, sa_ref, sb_ref, *, tq, tk, nh):
    i = pl.program_id(2)
    t0 = i * tq
    n_full = t0 // tk
    qs = [q_ref[0][:, e * LANES:(e + 1) * LANES] for e in range(nh)]

    def logits_into(buf, j):
        off = pl.multiple_of(j * tk, tk)
        for e in range(nh):
            buf[e] = _dot_nt(qs[e], k_ref[0, pl.ds(off, tk), e * LANES:(e + 1) * LANES])

    def consume(buf, j, states, masked):
        off = pl.multiple_of(j * tk, tk)
        out = []
        for e in range(nh):
            m, acc = states[e]
            s = buf[e]
            if masked:
                t = t0 + lax.broadcasted_iota(jnp.int32, (tq, tk), 0)
                key = j * tk + lax.broadcasted_iota(jnp.int32, (tq, tk), 1)
                s = jnp.where(key <= t, s, NEG_INF)
            vt = v_ref[0, pl.ds(off, tk), e * LANES:(e + 1) * LANES]
            m_new = jnp.maximum(m, jnp.max(s, axis=-1, keepdims=True))
            p = jnp.exp2(s - m_new)
            out.append((m_new, jnp.exp2(m - m_new) * acc + _dot(p.astype(BF16), vt)))
        return tuple(out)

    def pair(jj, states):
        j = 2 * jj
        logits_into(sb_ref, j + 1)
        states = consume(sa_ref, j, states, False)
        logits_into(sa_ref, j + 2)
        return consume(sb_ref, j + 1, states, False)

    init = tuple((jnp.full((tq, 1), NEG_INF, F32), jnp.zeros((tq, LANES), F32)) for _ in range(nh))
    logits_into(sa_ref, 0)
    states = lax.fori_loop(0, n_full // 2, pair, init)
    r = 2 * (n_full // 2)

    def diagonal_is_next(states):
        return consume(sa_ref, r, states, True)

    def one_full_tile_left(states):
        logits_into(sb_ref, r + 1)
        return consume(sb_ref, r + 1, consume(sa_ref, r, states, False), True)

    states = lax.cond(r == n_full, diagonal_is_next, one_full_tile_left, states)
    outs = [acc * (1.0 / acc[:, HEAD_DIM:HEAD_DIM + 1]) for _, acc in states]
    lane = _lane((tq, LANES))
    o_ref[0] = jnp.concatenate([jnp.where(lane < HEAD_DIM, outs[e], pltpu.roll(outs[e + 1], HEAD_DIM, 1))
                                for e in range(0, nh, 2)], axis=1)


def _dense_attention(q, k, v, *, seq, tq=256, tk=512, nh=2):
    b = q.shape[0]
    heads = q.shape[2] // LANES
    tk = min(tk, seq)
    tq = min(tq, tk)
    return pl.pallas_call(
        functools.partial(_dense_kernel, tq=tq, tk=tk, nh=nh),
        grid=(b, heads // nh, seq // tq),
        in_specs=[pl.BlockSpec((1, tq, nh * LANES), lambda bi, p, i: (bi, i, p)),
                  pl.BlockSpec((1, seq, nh * LANES), lambda bi, p, i: (bi, 0, p)),
                  pl.BlockSpec((1, seq, nh * LANES), lambda bi, p, i: (bi, 0, p))],
        out_specs=pl.BlockSpec((1, tq, nh * HEAD_DIM), lambda bi, p, i: (bi, i, p)),
        out_shape=jax.ShapeDtypeStruct((b, seq, heads * HEAD_DIM), F32),
        scratch_shapes=[pltpu.VMEM((nh, tq, tk), F32), pltpu.VMEM((nh, tq, tk), F32)],
        compiler_params=_params(("arbitrary", "arbitrary", "arbitrary")),
        name="dense_causal_attention",
    )(q, k, v)


def _decay_kernel(f_ref, b_ref, o_ref):
    x = f_ref[0] + b_ref[...]
    lf = jnp.minimum(x, 0.0) - jnp.log1p(jnp.exp(-jnp.abs(x)))
    n = lf.shape[-1]
    lane = _lane(lf.shape)
    d = 1
    while d < n:
        lf = lf + jnp.where(lane >= d, pltpu.roll(lf, d, 1), 0.0)
        d *= 2
    o_ref[0] = lf * LOG2E


def _decay_cumsum(f_t, bias):
    b, h, s = f_t.shape
    return pl.pallas_call(
        _decay_kernel,
        grid=(b,),
        in_specs=[pl.BlockSpec((1, h, s), lambda i: (i, 0, 0)), pl.BlockSpec((h, 1), lambda i: (0, 0))],
        out_specs=pl.BlockSpec((1, h, s), lambda i: (i, 0, 0)),
        out_shape=jax.ShapeDtypeStruct((b, h, s), F32),
        compiler_params=_params(("arbitrary",)),
        name="fox_decay_cumsum",
    )(f_t, bias.reshape(h, 1).astype(F32))


def _mla_prep_kernel(cq_ref, ckv_ref, misc_ref, gqa_ref, gkva_ref, wq_ref, wk_ref, wv_ref, gq_ref, gk_ref,
                     cos_ref, sin_ref, q_ref, k_ref, v_ref, *, scale):
    tm = cq_ref.shape[0]
    lane = _lane((tm, LANES))
    in_rope = (lane >= MLA_NOPE_DIM) & (lane < MLA_QK_DIM)
    first = lane < MLA_NOPE_DIM + MLA_ROPE_DIM // 2
    cos, sin = cos_ref[...], sin_ref[...]

    def rope_tail(x):
        sw = jnp.where(first, pltpu.roll(x, LANES - MLA_ROPE_DIM // 2, 1), pltpu.roll(x, MLA_ROPE_DIM // 2, 1))
        return x * cos + jnp.where(in_rope, sw, 0.0) * sin

    cq = _rms(cq_ref[...], gqa_ref[...], MLA_Q_RANK).astype(BF16)
    ckv = _rms(ckv_ref[...], gkva_ref[...], MLA_KV_RANK).astype(BF16)
    qa = _dot(cq, wq_ref[...])
    ka = _dot(ckv, wk_ref[...])
    k_rope = jnp.where(in_rope, misc_ref[...], 0.0)
    for h in range(MLA_HEADS):
        qh = _rms(qa[:, h * LANES:(h + 1) * LANES], gq_ref[...], MLA_QK_DIM)
        q_ref[:, h * LANES:(h + 1) * LANES] = (rope_tail(qh) * scale).astype(q_ref.dtype)
        kh = _rms(ka[:, h * LANES:(h + 1) * LANES] + k_rope, gk_ref[...], MLA_QK_DIM)
        k_ref[:, h * LANES:(h + 1) * LANES] = rope_tail(kh).astype(k_ref.dtype)
    v = _dot(ckv, wv_ref[...])
    ones_col = (_lane(v.shape) & (LANES - 1)) == MLA_V_DIM
    v_ref[...] = jnp.where(ones_col, 1.0, v).astype(v_ref.dtype)


def _mla_prep(y, cq_blk, ckv_blk, misc_blk, q_a_norm, w_q_b, kv_a_norm, w_kv_b, q_norm, k_norm, *, seq, tm=512):
    n = y.shape[0]
    h = MLA_HEADS
    pad = LANES - MLA_QK_DIM
    wq = jnp.pad(w_q_b.reshape(MLA_Q_RANK, h, MLA_QK_DIM), ((0, 0), (0, 0), (0, pad)))
    wq = wq.reshape(MLA_Q_RANK, h * LANES).astype(BF16)
    wkv = w_kv_b.reshape(MLA_KV_RANK, h, MLA_NOPE_DIM + MLA_V_DIM)
    wk = jnp.pad(wkv[:, :, :MLA_NOPE_DIM], ((0, 0), (0, 0), (0, LANES - MLA_NOPE_DIM)))
    wk = wk.reshape(MLA_KV_RANK, h * LANES).astype(BF16)
    wv = jnp.pad(wkv[:, :, MLA_NOPE_DIM:], ((0, 0), (0, 0), (0, LANES - MLA_V_DIM)))
    wv = wv.reshape(MLA_KV_RANK, h * LANES).astype(BF16)
    gq = jnp.pad(q_norm, (0, pad)).reshape(1, LANES)
    gk = jnp.pad(k_norm, (0, pad)).reshape(1, LANES)
    half = MLA_ROPE_DIM // 2
    inv_freq = 1.0 / (ROPE_THETA ** (jnp.arange(0, MLA_ROPE_DIM, 2, dtype=F32) / MLA_ROPE_DIM))
    ang = jnp.arange(seq, dtype=F32)[:, None] * inv_freq[None, :]
    cos, sin = jnp.cos(ang), jnp.sin(ang)
    ones = jnp.ones((seq, MLA_NOPE_DIM), F32)
    zeros = jnp.zeros((seq, MLA_NOPE_DIM), F32)
    cos_t = jnp.concatenate([ones, cos, cos, ones[:, :pad]], axis=1)
    sin_t = jnp.concatenate([zeros, -sin, sin, zeros[:, :pad]], axis=1)
    del half
    sblocks = seq // tm
    full = lambda shp: pl.BlockSpec(shp, lambda i: (0,) * len(shp))
    q, k, v = pl.pallas_call(
        functools.partial(_mla_prep_kernel, scale=MLA_QK_DIM ** -0.5 * LOG2E),
        grid=(n // tm,),
        in_specs=[pl.BlockSpec((tm, MLA_Q_RANK), lambda i: (i, cq_blk // 2)),
                  pl.BlockSpec((tm, LANES), lambda i: (i, ckv_blk)),
                  pl.BlockSpec((tm, LANES), lambda i: (i, misc_blk)),
                  full((1, MLA_Q_RANK)), full((1, MLA_KV_RANK)),
                  full((MLA_Q_RANK, h * LANES)), full((MLA_KV_RANK, h * LANES)), full((MLA_KV_RANK, h * LANES)),
                  full((1, LANES)), full((1, LANES)),
                  pl.BlockSpec((tm, LANES), lambda i: (i % sblocks, 0)),
                  pl.BlockSpec((tm, LANES), lambda i: (i % sblocks, 0))],
        out_specs=[pl.BlockSpec((tm, h * LANES), lambda i: (i, 0)),
                   pl.BlockSpec((tm, h * LANES), lambda i: (i, 0)),
                   pl.BlockSpec((tm, h * LANES), lambda i: (i, 0))],
        out_shape=[jax.ShapeDtypeStruct((n, h * LANES), BF16),
                   jax.ShapeDtypeStruct((n, h * LANES), BF16),
                   jax.ShapeDtypeStruct((n, h * LANES), BF16)],
        compiler_params=_params(("arbitrary",)),
        name="mla_prep",
    )(y, y, y, q_a_norm.reshape(1, -1), kv_a_norm.reshape(1, -1), wq, wk, wv, gq, gk, cos_t, sin_t)
    return q, k, v


def _cols(w, a, b):
    return w[:, a:b]


def _nsa_swa_mixer(x, batch, seq, mix_norm, w_in, nsa_gate_b, nsa_q_norm, nsa_kc_norm, nsa_ks_norm, nsa_kw_norm,
                   cmp_pos_k, cmp_pos_v, cmpk_w1, cmpk_w2, cmpv_w1, cmpv_w2,
                   swa_q_norm, swa_k_norm, swa_sinks, w_out):
    n = batch * seq
    o = np.cumsum([0, 512, 128, 128, 128, 128, 128, 128, 24, 512, 128, 128])
    seg = lambda j: _cols(w_in, o[j], o[j + 1])
    q_a, kc, vc, ks, vs, kw, vw, gl, q_b, k_b, v_b = [seg(j) for j in range(11)]
    gl = jnp.pad(gl, ((0, 0), (0, LANES - gl.shape[1])))
    w = jnp.concatenate([q_a, q_b, ks, kw, k_b, kc, vc, vs, vw, v_b, gl], axis=1).astype(BF16)
    y = _rms_matmul(x, mix_norm, w)
    cos_t, sin_t = _rope_tables(seq)

    s_q = HEAD_DIM ** -0.5 * LOG2E
    gains = jnp.stack([_pair_gain(g) for g in (nsa_q_norm, swa_q_norm, nsa_ks_norm, nsa_kw_norm, swa_k_norm)])
    jobs = [_Job(blk=c, out=0, col=2 * c, gain=c // 4, rope=True, scale=s_q, mode="q", dst=((c % 4) // 2,) * 2)
            for c in range(8)]
    jobs += [_Job(blk=c, out=1, col=2 * c, gain=0, scale=s_q, mode="q", dst=(c // 2,) * 2)
             for c in range(4)]
    jobs += [_Job(blk=8, out=2, col=0, gain=2, rope=True, mode="kaug"),
             _Job(blk=9, out=3, col=0, gain=3, rope=True), _Job(blk=10, out=3, col=1, gain=4, rope=True),
             _Job(blk=13, out=4, col=0, mode="v"),
             _Job(blk=14, out=5, col=0), _Job(blk=15, out=5, col=1),
             _Job(blk=11, out=6, col=0), _Job(blk=12, out=7, col=0), _Job(blk=16, out=8, col=0)]
    outs = [(16, BF16), (8, BF16), (2, BF16), (2, BF16), (2, BF16), (2, BF16), (1, F32), (1, F32), (1, F32)]
    q_rot, q_cmp, k_aug, k_ws, v_slc, v_ws, kc_raw, vc_raw, gates = (
        a.reshape(batch, seq, -1) for a in _prep(y, jobs, outs, gains, seq, cos_t, sin_t))

    k_cmp = _compress(kc_raw, cmp_pos_k, cmpk_w1, cmpk_w2, nsa_kc_norm)
    v_cmp = _compress(vc_raw, cmp_pos_v, cmpv_w1, cmpv_w2, None)
    gate_b = nsa_gate_b.astype(F32)

    o_cmp, sel = _nsa_compressed(q_cmp, k_cmp, v_cmp, gates, gate_b, seq=seq)
    o_slc = _nsa_selected(q_rot, k_aug, v_slc, sel, gates, gate_b, seq=seq)
    o_win = _window_attention(q_rot, 0, k_ws, v_ws, 0, gate_b, gates, seq=seq, window=NSA_WINDOW,
                              gated=True, sinks=False)
    o_swa = _window_attention(q_rot, 8, k_ws, v_ws, 1, swa_sinks.astype(F32), gates, seq=seq, window=SWA_WINDOW,
                              gated=False, sinks=True)
    flat = lambda a: a.reshape(n, -1)
    return _outproj(x, [flat(o_cmp), flat(o_slc), flat(o_win)], [flat(o_swa)], w_out)


def _fox_mla_mixer(x, batch, seq, mix_norm, w_in, fox_f_bias, fox_q_norm, fox_k_norm, mla_q_a_norm, mla_w_q_b,
                   mla_kv_a_norm, mla_w_kv_b, mla_q_norm, mla_k_norm, w_out):
    n = batch * seq
    o = np.cumsum([0, 512, 512, 512, 8, 256, 128, 32])
    seg = lambda j: _cols(w_in, o[j], o[j + 1])
    q_c, k_c, v_c, f_c, c_q, c_kv, k_r = [seg(j) for j in range(7)]
    d = w_in.shape[0]
    misc = jnp.concatenate([f_c, jnp.zeros((d, MLA_NOPE_DIM - 8), w_in.dtype), k_r,
                            jnp.zeros((d, LANES - MLA_QK_DIM), w_in.dtype)], axis=1)
    w = jnp.concatenate([q_c, k_c, v_c, c_q, c_kv, misc], axis=1).astype(BF16)
    y = _rms_matmul(x, mix_norm, w)
    cos_t, sin_t = _rope_tables(seq)

    y3 = y.reshape(batch, seq, y.shape[1])
    f_t = y3[:, :, 15 * LANES:15 * LANES + FOX_HEADS].transpose(0, 2, 1)
    dc = _decay_cumsum(f_t, fox_f_bias)
    dc_tok = jnp.pad(dc.transpose(0, 2, 1).reshape(n, FOX_HEADS), ((0, 0), (0, LANES - FOX_HEADS)))
    gains = jnp.stack([_pair_gain(fox_q_norm), _pair_gain(fox_k_norm)])
    jobs = [_Job(blk=c, out=0, col=2 * c, gain=0, scale=HEAD_DIM ** -0.5 * LOG2E, mode="q", aug="ones")
            for c in range(4)]
    jobs += [_Job(blk=8 + c, out=1, col=2 * c, mode="v") for c in range(4)]
    q_f, v_f = _prep(y, jobs, [(8, BF16), (8, BF16)], gains, seq, cos_t, sin_t)
    kjobs = [_Job(blk=4 + c, out=0, col=2 * c, gain=1, mode="q", aug="decay", heads=(2 * c, 2 * c + 1))
             for c in range(4)]
    (k_f,) = _prep(y, kjobs, [(8, BF16)], gains, seq, cos_t, sin_t, aux=dc_tok)
    b3 = lambda a: a.reshape(batch, seq, -1)
    o_fox = _dense_attention(b3(q_f), b3(k_f), b3(v_f), seq=seq)

    q_m, k_m, v_m = _mla_prep(y, 12, 14, 15, mla_q_a_norm, mla_w_q_b, mla_kv_a_norm, mla_w_kv_b,
                              mla_q_norm, mla_k_norm, seq=seq)
    o_mla = _dense_attention(b3(q_m), b3(k_m), b3(v_m), seq=seq)
    flat = lambda a: a.reshape(n, -1)
    return _outproj(x, [flat(o_fox)], [flat(o_mla)], w_out)


def kernel(x, l0_ffn1_norm, l0_ffn1_w_gate, l0_ffn1_w_up, l0_ffn1_w_down, l0_mix_norm, l0_w_in, l0_nsa_gate_b, l0_nsa_q_norm, l0_nsa_kc_norm, l0_nsa_ks_norm, l0_nsa_kw_norm, l0_cmp_pos_k, l0_cmp_pos_v, l0_cmpk_w1, l0_cmpk_w2, l0_cmpv_w1, l0_cmpv_w2, l0_swa_q_norm, l0_swa_k_norm, l0_swa_sinks, l0_w_out, l0_ffn2_norm, l0_ffn2_w_gate, l0_ffn2_w_up, l0_ffn2_w_down, l1_ffn1_norm, l1_ffn1_w_gate, l1_ffn1_w_up, l1_ffn1_w_down, l1_mix_norm, l1_w_in, l1_fox_f_bias, l1_fox_q_norm, l1_fox_k_norm, l1_mla_q_a_norm, l1_mla_w_q_b, l1_mla_kv_a_norm, l1_mla_w_kv_b, l1_mla_q_norm, l1_mla_k_norm, l1_w_out, l1_ffn2_norm, l1_ffn2_w_gate, l1_ffn2_w_up, l1_ffn2_w_down):
    batch, seq, d = x.shape
    h = x.reshape(batch * seq, d)
    h = _ffn(h, l0_ffn1_norm, l0_ffn1_w_gate, l0_ffn1_w_up, l0_ffn1_w_down)
    h = _nsa_swa_mixer(h, batch, seq, l0_mix_norm, l0_w_in, l0_nsa_gate_b, l0_nsa_q_norm, l0_nsa_kc_norm,
                       l0_nsa_ks_norm, l0_nsa_kw_norm, l0_cmp_pos_k, l0_cmp_pos_v, l0_cmpk_w1, l0_cmpk_w2,
                       l0_cmpv_w1, l0_cmpv_w2, l0_swa_q_norm, l0_swa_k_norm, l0_swa_sinks, l0_w_out)
    h = _ffn(h, l0_ffn2_norm, l0_ffn2_w_gate, l0_ffn2_w_up, l0_ffn2_w_down)
    h = _ffn(h, l1_ffn1_norm, l1_ffn1_w_gate, l1_ffn1_w_up, l1_ffn1_w_down)
    h = _fox_mla_mixer(h, batch, seq, l1_mix_norm, l1_w_in, l1_fox_f_bias, l1_fox_q_norm, l1_fox_k_norm,
                       l1_mla_q_a_norm, l1_mla_w_q_b, l1_mla_kv_a_norm, l1_mla_w_kv_b, l1_mla_q_norm,
                       l1_mla_k_norm, l1_w_out)
    h = _ffn(h, l1_ffn2_norm, l1_ffn2_w_gate, l1_ffn2_w_up, l1_ffn2_w_down)
    return h.reshape(batch, seq, d)
```

```python
import functools
from typing import NamedTuple, Optional

import numpy as np
import jax
import jax.numpy as jnp
from jax import lax
from jax.experimental import pallas as pl
from jax.experimental.pallas import tpu as pltpu

F32 = jnp.float32
BF16 = jnp.bfloat16

HEAD_DIM = 64
LANES = 128
ROPE_THETA = 10000.0
RMS_EPS = 1e-6
NEG_INF = -1e30
FORCE_SCORE = 1e9
BELOW_ALL = -3e38
LOG2E = 1.4426950408889634
BIG = 1e30
VT_ROWS = 80

NSA_HEADS = 8
NSA_KV_HEADS = 2
CMP_BLOCK = 32
CMP_STRIDE = 16
CMP_HIDDEN = 256
SLC_BLOCK = 64
N_SELECT = 16
NSA_WINDOW = 512
SWA_HEADS = 8
SWA_KV_HEADS = 2
SWA_WINDOW = 128
FOX_HEADS = 8
MLA_HEADS = 8
MLA_Q_RANK = 256
MLA_KV_RANK = 128
MLA_NOPE_DIM = 64
MLA_ROPE_DIM = 32
MLA_V_DIM = 64
MLA_QK_DIM = MLA_NOPE_DIM + MLA_ROPE_DIM

VMEM_LIMIT = 48 * 1024 * 1024

NT_DIMS = (((1,), (1,)), ((), ()))


def _params(sem):
    return pltpu.CompilerParams(dimension_semantics=sem, vmem_limit_bytes=VMEM_LIMIT)


def _dot(a, b):
    return jnp.dot(a, b, preferred_element_type=F32)


def _dot_nt(a, b):
    return lax.dot_general(a, b, NT_DIMS, preferred_element_type=F32)


def _rms(x, gain, n):
    ms = jnp.sum(x * x, axis=-1, keepdims=True) * (1.0 / n)
    return (x * lax.rsqrt(ms + RMS_EPS)) * gain


def _lane(shape):
    return lax.broadcasted_iota(jnp.int32, shape, len(shape) - 1)


def _ffn_kernel(x_ref, g_ref, wg_ref, wu_ref, wd_ref, o_ref, h_sc, acc_sc, *, tf):
    x = x_ref[...]
    h_sc[...] = _rms(x, g_ref[...], x.shape[-1]).astype(BF16)
    acc_sc[...] = jnp.zeros_like(acc_sc)
    f = wg_ref.shape[1]
    for c0 in range(0, f, tf):
        cols = slice(c0, min(c0 + tf, f))
        h = h_sc[...]
        g = _dot(h, wg_ref[:, cols])
        u = _dot(h, wu_ref[:, cols])
        a = (g * (1.0 / (1.0 + jnp.exp(-g)))) * u
        acc_sc[...] += _dot(a.astype(BF16), wd_ref[cols, :])
    o_ref[...] = x + 0.5 * acc_sc[...]


def _ffn(x, norm, w_gate, w_up, w_down, *, tm=512, tf=256):
    n, d = x.shape
    f = w_gate.shape[1]
    wg, wu, wd = w_gate.astype(BF16), w_up.astype(BF16), w_down.astype(BF16)
    wspec = lambda shp: pl.BlockSpec(shp, lambda i: (0, 0), pipeline_mode=pl.Buffered(1))
    return pl.pallas_call(
        functools.partial(_ffn_kernel, tf=tf),
        grid=(n // tm,),
        in_specs=[pl.BlockSpec((tm, d), lambda i: (i, 0)),
                  pl.BlockSpec((1, d), lambda i: (0, 0)),
                  wspec((d, f)), wspec((d, f)), wspec((f, d))],
        out_specs=pl.BlockSpec((tm, d), lambda i: (i, 0)),
        out_shape=jax.ShapeDtypeStruct((n, d), F32),
        scratch_shapes=[pltpu.VMEM((tm, d), BF16), pltpu.VMEM((tm, d), F32)],
        compiler_params=_params(("arbitrary",)),
        name="ffn",
    )(x, norm.reshape(1, d), wg, wu, wd)


def _rms_matmul_kernel(x_ref, g_ref, w_ref, o_ref):
    x = x_ref[...]
    h = _rms(x, g_ref[...], x.shape[-1]).astype(BF16)
    o_ref[...] = _dot(h, w_ref[...])


def _rms_matmul(x, norm, w, *, tm=512):
    n, d = x.shape
    c = w.shape[1]
    return pl.pallas_call(
        _rms_matmul_kernel,
        grid=(n // tm,),
        in_specs=[pl.BlockSpec((tm, d), lambda i: (i, 0)),
                  pl.BlockSpec((1, d), lambda i: (0, 0)),
                  pl.BlockSpec((d, c), lambda i: (0, 0), pipeline_mode=pl.Buffered(1))],
        out_specs=pl.BlockSpec((tm, c), lambda i: (i, 0)),
        out_shape=jax.ShapeDtypeStruct((n, c), F32),
        compiler_params=_params(("arbitrary",)),
        name="rms_matmul",
    )(x, norm.reshape(1, d), w)


def _outproj_kernel(*refs, n_a, n_b):
    x_ref = refs[0]
    a_refs = refs[1:1 + n_a]
    b_refs = refs[1 + n_a:1 + n_a + n_b]
    wa_ref, wb_ref, o_ref = refs[1 + n_a + n_b:]
    a = a_refs[0][...]
    for r in a_refs[1:]:
        a = a + r[...]
    b = b_refs[0][...]
    for r in b_refs[1:]:
        b = b + r[...]
    o_ref[...] = x_ref[...] + _dot(a.astype(BF16), wa_ref[...]) + _dot(b.astype(BF16), wb_ref[...])


def _outproj(x, a_list, b_list, w_out, *, tm=512):
    n, d = x.shape
    ca = a_list[0].shape[1]
    cb = b_list[0].shape[1]
    wa = w_out[:ca].astype(BF16)
    wb = w_out[ca:].astype(BF16)
    row = lambda c: pl.BlockSpec((tm, c), lambda i: (i, 0))
    return pl.pallas_call(
        functools.partial(_outproj_kernel, n_a=len(a_list), n_b=len(b_list)),
        grid=(n // tm,),
        in_specs=[row(d)] + [row(ca)] * len(a_list) + [row(cb)] * len(b_list)
                 + [pl.BlockSpec((ca, d), lambda i: (0, 0)), pl.BlockSpec((cb, d), lambda i: (0, 0))],
        out_specs=row(d),
        out_shape=jax.ShapeDtypeStruct((n, d), F32),
        compiler_params=_params(("arbitrary",)),
        name="outproj",
    )(x, *a_list, *b_list, wa, wb)


class _Job(NamedTuple):
    blk: int
    out: int
    col: int
    gain: Optional[int] = None
    rope: bool = False
    scale: float = 1.0
    mode: str = "plain"
    dst: tuple = (0, 0)
    aug: Optional[str] = None
    heads: tuple = (0, 0)


def _prep_kernel(y_ref, gain_ref, cos_ref, sin_ref, aux_ref, *o_refs, jobs, seq):
    tm = y_ref.shape[0]
    lane = _lane((tm, LANES))
    low = lane < HEAD_DIM
    zero = jnp.zeros((tm, LANES), F32)
    for job in jobs:
        x = y_ref[:, job.blk * LANES:(job.blk + 1) * LANES]
        if job.gain is not None:
            x2 = x * x
            s_lo = jnp.sum(jnp.where(low, x2, 0.0), axis=-1, keepdims=True)
            s_hi = jnp.sum(jnp.where(low, 0.0, x2), axis=-1, keepdims=True)
            ms = jnp.where(low, s_lo, s_hi) * (1.0 / HEAD_DIM)
            x = (x * lax.rsqrt(ms + RMS_EPS)) * gain_ref[job.gain]
        if job.rope:
            swapped = jnp.where((lane & (HEAD_DIM - 1)) < HEAD_DIM // 2,
                                pltpu.roll(x, LANES - HEAD_DIM // 2, 1), pltpu.roll(x, HEAD_DIM // 2, 1))
            x = x * cos_ref[...] + swapped * sin_ref[...]
        if job.scale != 1.0:
            x = x * job.scale
        if job.mode == "plain":
            pieces = [x]
        elif job.mode == "kaug":
            pos = (pl.program_id(0) * tm + lax.broadcasted_iota(jnp.int32, (tm, LANES), 0)) % seq
            pieces = [x, jnp.where(lane == pos // SLC_BLOCK, BIG, 0.0)]
        elif job.mode == "v":
            r = pltpu.roll(x, HEAD_DIM, 1)
            tail = jnp.where(lane == HEAD_DIM, 1.0, 0.0)
            pieces = [jnp.where(low, x, tail), jnp.where(low, r, tail)]
        else:
            r = pltpu.roll(x, HEAD_DIM, 1)
            h_even = jnp.where(low, x, zero) if job.dst[0] == 0 else jnp.where(low, zero, r)
            h_odd = jnp.where(low, r, zero) if job.dst[1] == 0 else jnp.where(low, zero, x)
            if job.aug is not None:
                tails = []
                for e in range(2):
                    if job.aug == "ones":
                        tails.append(jnp.where((lane >= HEAD_DIM) & (lane < HEAD_DIM + 3), 1.0, 0.0))
                    else:
                        d = jnp.sum(jnp.where(lane == job.heads[e], aux_ref[...], 0.0), axis=-1, keepdims=True)
                        hi = d.astype(BF16).astype(F32)
                        mid = (d - hi).astype(BF16).astype(F32)
                        lo = d - hi - mid
                        tails.append(jnp.where(lane == HEAD_DIM, -hi, jnp.where(lane == HEAD_DIM + 1, -mid,
                                     jnp.where(lane == HEAD_DIM + 2, -lo, 0.0))))
                h_even = jnp.where(low, h_even, tails[0])
                h_odd = jnp.where(low, h_odd, tails[1])
            pieces = [h_even, h_odd]
        o_ref = o_refs[job.out]
        for n, piece in enumerate(pieces):
            o_ref[:, (job.col + n) * LANES:(job.col + n + 1) * LANES] = piece.astype(o_ref.dtype)


def _prep(y, jobs, outs, gains, seq, cos_t, sin_t, *, aux=None, tm=512):
    n, c = y.shape
    aux_spec = pl.BlockSpec((tm, LANES), lambda i: (i, 0))
    if aux is None:
        aux, aux_spec = jnp.zeros((tm, LANES), F32), pl.BlockSpec((tm, LANES), lambda i: (0, 0))
    sblocks = seq // tm
    return pl.pallas_call(
        functools.partial(_prep_kernel, jobs=tuple(jobs), seq=seq),
        grid=(n // tm,),
        in_specs=[pl.BlockSpec((tm, c), lambda i: (i, 0)),
                  pl.BlockSpec(gains.shape, lambda i: (0, 0, 0)),
                  pl.BlockSpec((tm, LANES), lambda i: (i % sblocks, 0)),
                  pl.BlockSpec((tm, LANES), lambda i: (i % sblocks, 0)),
                  aux_spec],
        out_specs=[pl.BlockSpec((tm, w * LANES), lambda i: (i, 0)) for w, _ in outs],
        out_shape=[jax.ShapeDtypeStruct((n, w * LANES), dt) for w, dt in outs],
        compiler_params=_params(("arbitrary",)),
        name="head_prep",
    )(y, gains, cos_t, sin_t, aux)


def _pair_gain(g):
    return jnp.concatenate([g, g]).reshape(1, LANES).astype(F32)


def _rope_tables(seq):
    half = HEAD_DIM // 2
    inv_freq = 1.0 / (ROPE_THETA ** (jnp.arange(0, HEAD_DIM, 2, dtype=F32) / HEAD_DIM))
    ang = jnp.arange(seq, dtype=F32)[:, None] * inv_freq[None, :]
    cos, sin = jnp.cos(ang), jnp.sin(ang)
    cos_t = jnp.concatenate([cos, cos, cos, cos], axis=1)
    sin_t = jnp.concatenate([-sin, sin, -sin, sin], axis=1)
    del half
    return cos_t, sin_t


def _compress_kernel(ch_ref, ptop_ref, pbot_ref, w1t_ref, w1b_ref, w2_ref, gain_ref, o_ref, *, norm):
    ch = ch_ref[0]
    a = _dot((ch + ptop_ref[...]).astype(BF16), w1t_ref[...])
    b = _dot((ch + pbot_ref[...]).astype(BF16), w1b_ref[...])
    nc = a.shape[0]
    hid = a + pltpu.roll(b, nc - 1, 0)
    act = hid * (1.0 / (1.0 + jnp.exp(-hid)))
    out = _dot(act.astype(BF16), w2_ref[...])
    if norm:
        lane = _lane(out.shape)
        low = lane < HEAD_DIM
        o2 = out * out
        s_lo = jnp.sum(jnp.where(low, o2, 0.0), axis=-1, keepdims=True)
        s_hi = jnp.sum(jnp.where(low, 0.0, o2), axis=-1, keepdims=True)
        ms = jnp.where(low, s_lo, s_hi) * (1.0 / HEAD_DIM)
        out = (out * lax.rsqrt(ms + RMS_EPS)) * gain_ref[...]
    o_ref[0] = out.astype(o_ref.dtype)


def _compress(t_pair, pos_emb, w1, w2, gain):
    b, s, _ = t_pair.shape
    nc = s // CMP_STRIDE
    hid = w1.shape[1]
    ch = t_pair.reshape(b, nc, CMP_STRIDE * LANES)
    eye2 = jnp.eye(2, dtype=F32)
    w1r = w1.reshape(CMP_BLOCK, HEAD_DIM, hid)
    def expand_w1(w):
        return jnp.einsum('pdj,kl->pkdlj', w, eye2).reshape(CMP_STRIDE * LANES, 2 * hid).astype(BF16)
    w1t, w1b = expand_w1(w1r[:CMP_STRIDE]), expand_w1(w1r[CMP_STRIDE:])
    w2e = jnp.einsum('jd,kl->kjld', w2, eye2).reshape(2 * hid, LANES).astype(BF16)
    def expand_pos(p):
        return jnp.broadcast_to(p[:, None, :], (CMP_STRIDE, 2, HEAD_DIM)).reshape(1, CMP_STRIDE * LANES)
    ptop, pbot = expand_pos(pos_emb[:CMP_STRIDE]), expand_pos(pos_emb[CMP_STRIDE:])
    norm = gain is not None
    g = _pair_gain(gain) if norm else jnp.ones((1, LANES), F32)
    full = lambda shp: pl.BlockSpec(shp, lambda i: (0,) * len(shp))
    return pl.pallas_call(
        functools.partial(_compress_kernel, norm=norm),
        grid=(b,),
        in_specs=[pl.BlockSpec((1, nc, CMP_STRIDE * LANES), lambda i: (i, 0, 0)),
                  full((1, CMP_STRIDE * LANES)), full((1, CMP_STRIDE * LANES)),
                  full((CMP_STRIDE * LANES, 2 * hid)), full((CMP_STRIDE * LANES, 2 * hid)),
                  full((2 * hid, LANES)), full((1, LANES))],
        out_specs=pl.BlockSpec((1, nc, LANES), lambda i: (i, 0, 0)),
        out_shape=jax.ShapeDtypeStruct((b, nc, LANES), BF16),
        compiler_params=_params(("arbitrary",)),
        name="nsa_compress",
    )(ch, ptop, pbot, w1t, w1b, w2e, g)


def _stack_heads(q, n):
    return jnp.concatenate([q[:, g * LANES:(g + 1) * LANES] for g in range(n)], axis=0)


def _gate_column(gl, col):
    lane = _lane(gl.shape)
    return jnp.sum(jnp.where(lane == col, gl, 0.0), axis=-1, keepdims=True)


def _sigmoid(x):
    return 1.0 / (1.0 + jnp.exp(-x))


def _compact_group(heads, hk):
    tq = heads[0].shape[0]
    lane = _lane((tq, LANES))
    low = lane < HEAD_DIM
    outs = []
    for e in range(0, len(heads), 2):
        he, ho = heads[e], heads[e + 1]
        ho_r = pltpu.roll(ho, HEAD_DIM, 1)
        if hk is None:
            lo_part, hi_part = he, ho_r
        else:
            at_low = jnp.broadcast_to(hk, (tq, LANES)) == 0
            lo_part = jnp.where(at_low, he, pltpu.roll(he, HEAD_DIM, 1))
            hi_part = jnp.where(at_low, ho_r, ho)
        outs.append(jnp.where(low, lo_part, hi_part))
    return jnp.concatenate(outs, axis=1)


def _nsa_cmp_kernel(gb_ref, q_ref, k_ref, v_ref, ov_ref, gl_ref, o_ref, sel_ref, *, tq, group, n_sel, ns):
    hk = pl.program_id(1)
    i = pl.program_id(2)
    q4 = _stack_heads(q_ref[0], group)
    kc = k_ref[0]
    ncp = kc.shape[0]
    logits = _dot_nt(q4, kc).reshape(group, tq, ncp)
    t = i * tq + lax.broadcasted_iota(jnp.int32, (tq, ncp), 0)
    cmp_end = lax.broadcasted_iota(jnp.int32, (tq, ncp), 1) * CMP_STRIDE + (CMP_BLOCK - 1)
    logits = jnp.where((cmp_end <= t)[None], logits, NEG_INF)
    m = jnp.max(logits, axis=-1, keepdims=True)
    e = jnp.exp2(logits - m)
    t_row = i * tq + lax.broadcasted_iota(jnp.int32, (tq, 1), 0)
    seen = jnp.where(t_row >= CMP_BLOCK - 1, 1.0, 0.0)[None]
    inv = seen / jnp.maximum(jnp.sum(e, axis=-1, keepdims=True), 1e-30)
    p = e * inv
    o4 = _dot(p.reshape(group * tq, ncp).astype(BF16), v_ref[0])
    gl = gl_ref[0]
    heads = []
    for g in range(group):
        col = (hk * group + g) * 3
        gate = _sigmoid(_gate_column(gl, col) + gb_ref[col])
        heads.append(o4[g * tq:(g + 1) * tq] * gate)
    o_ref[0] = _compact_group(heads, hk)

    ps = jnp.sum(p, axis=0)
    ps_hi = ps.astype(BF16)
    ps_lo = (ps - ps_hi.astype(F32)).astype(BF16)
    imp = _dot(ps_hi, ov_ref[...]) + _dot(ps_lo, ov_ref[...])
    imp_t = imp.T
    blk = lax.broadcasted_iota(jnp.int32, (LANES, tq), 0)
    cur = (i * tq + lax.broadcasted_iota(jnp.int32, (LANES, tq), 1)) // SLC_BLOCK
    forced = (blk == 0) | (blk == cur) | (blk == cur - 1)
    score = jnp.where(forced, FORCE_SCORE, jnp.where(blk <= cur, imp_t, NEG_INF))
    score = jnp.where(blk < ns, score, BELOW_ALL)
    blk_f = blk.astype(F32)

    def pick(_, carry):
        sc, sel = carry
        mx = jnp.max(sc, axis=0, keepdims=True)
        first = jnp.min(jnp.where(sc == mx, blk_f, float(LANES)), axis=0, keepdims=True)
        hit = blk_f == first
        return jnp.where(hit, BELOW_ALL, sc), jnp.where(hit, 1.0, sel)

    _, sel = lax.fori_loop(0, n_sel, pick, (score, jnp.zeros((LANES, tq), F32)))
    sel_ref[0, 0] = jnp.where(blk <= cur, sel, 0.0).T.astype(sel_ref.dtype)


def _nsa_compressed(q_cmp, k_cmp, v_cmp, gates, gate_b, *, seq, tq=256):
    b = q_cmp.shape[0]
    group = NSA_HEADS // NSA_KV_HEADS
    ncp = k_cmp.shape[1]
    ns = seq // SLC_BLOCK
    n_sel = min(N_SELECT, ns)
    c_start = np.arange(ncp)[:, None] * CMP_STRIDE
    s_start = np.arange(LANES)[None, :] * SLC_BLOCK
    overlap = np.maximum(np.minimum(c_start + CMP_BLOCK, s_start + SLC_BLOCK) - np.maximum(c_start, s_start), 0)
    overlap = np.where((np.arange(LANES)[None, :] < ns) & (np.arange(ncp)[:, None] < ncp - 1), overlap, 0)
    overlap = jnp.asarray(overlap, BF16)
    gw = group * LANES
    tq = min(tq, seq)
    return pl.pallas_call(
        functools.partial(_nsa_cmp_kernel, tq=tq, group=group, n_sel=n_sel, ns=ns),
        grid=(b, NSA_KV_HEADS, seq // tq),
        in_specs=[pl.BlockSpec(memory_space=pltpu.SMEM),
                  pl.BlockSpec((1, tq, gw), lambda bi, h, i: (bi, i, h)),
                  pl.BlockSpec((1, ncp, LANES), lambda bi, h, i: (bi, 0, 0)),
                  pl.BlockSpec((1, ncp, LANES), lambda bi, h, i: (bi, 0, 0)),
                  pl.BlockSpec((ncp, LANES), lambda bi, h, i: (0, 0)),
                  pl.BlockSpec((1, tq, LANES), lambda bi, h, i: (bi, i, 0))],
        out_specs=[pl.BlockSpec((1, tq, group * HEAD_DIM), lambda bi, h, i: (bi, i, h)),
                   pl.BlockSpec((1, 1, tq, LANES), lambda bi, h, i: (bi, h, i, 0))],
        out_shape=[jax.ShapeDtypeStruct((b, seq, NSA_HEADS * HEAD_DIM), F32),
                   jax.ShapeDtypeStruct((b, NSA_KV_HEADS, seq, LANES), BF16)],
        compiler_params=_params(("arbitrary", "arbitrary", "arbitrary")),
        name="nsa_compressed_select",
    )(gate_b, q_cmp, k_cmp, v_cmp, overlap, gates)


def _nsa_slc_kernel(gb_ref, q_ref, k_ref, v_ref, sel_ref, gl_ref, o_ref, sa_ref, sb_ref, *, tq, tk, group):
    hk = pl.program_id(1)
    i = pl.program_id(2)
    q = q_ref[0]
    unsel = (sel_ref[0, 0].astype(F32) - 1.0).astype(BF16)
    lhs = [jnp.concatenate([jnp.concatenate([q[:, g * LANES:(g + 1) * LANES], unsel], axis=1)
                            for g in (2 * c, 2 * c + 1)], axis=0) for c in range(group // 2)]
    n_full = (i * tq) // tk
    chains = group // 2

    def logits_into(buf, j):
        off = pl.multiple_of(j * tk, tk)
        kt = k_ref[0, pl.ds(off, tk), :]
        for c in range(chains):
            buf[c] = _dot_nt(lhs[c], kt)

    def consume(buf, j, states, masked):
        off = pl.multiple_of(j * tk, tk)
        vt = v_ref[0, pl.ds(off, tk), :]
        out = []
        for c in range(chains):
            m, acc = states[c]
            s = buf[c]
            if masked:
                t = i * tq + lax.broadcasted_iota(jnp.int32, (tq, tk), 0)
                key = j * tk + lax.broadcasted_iota(jnp.int32, (tq, tk), 1)
                ok = key <= t
                s = jnp.where(jnp.concatenate([ok, ok], axis=0), s, NEG_INF)
            m_new = jnp.maximum(m, jnp.max(s, axis=-1, keepdims=True))
            p = jnp.exp2(s - m_new)
            out.append((m_new, jnp.exp2(m - m_new) * acc + _dot(p.astype(BF16), vt)))
        return tuple(out)

    def pair(jj, states):
        j = 2 * jj
        logits_into(sb_ref, j + 1)
        states = consume(sa_ref, j, states, False)
        logits_into(sa_ref, j + 2)
        return consume(sb_ref, j + 1, states, False)

    init = tuple((jnp.full((2 * tq, 1), NEG_INF, F32), jnp.zeros((2 * tq, LANES), F32)) for _ in range(chains))
    logits_into(sa_ref, 0)
    states = lax.fori_loop(0, n_full // 2, pair, init)
    r = 2 * (n_full // 2)

    def diagonal_is_next(states):
        return consume(sa_ref, r, states, True)

    def one_full_tile_left(states):
        logits_into(sb_ref, r + 1)
        return consume(sb_ref, r + 1, consume(sa_ref, r, states, False), True)

    carry = lax.cond(r == n_full, diagonal_is_next, one_full_tile_left, states)
    gl = gl_ref[0]
    heads = []
    for g in range(group):
        acc = carry[g // 2][1][(g % 2) * tq:(g % 2 + 1) * tq]
        colg = (hk * group + g) * 3 + 1
        gate = _sigmoid(_gate_column(gl, colg) + gb_ref[colg])
        heads.append(acc * (gate / acc[:, HEAD_DIM:HEAD_DIM + 1]))
    o_ref[0] = _compact_group(heads, None)


def _nsa_selected(q_rot, k_aug, v_exp, sel, gates, gate_b, *, seq, tq=128, tk=512):
    b = q_rot.shape[0]
    group = NSA_HEADS // NSA_KV_HEADS
    tk = min(tk, seq)
    gw = group * LANES
    return pl.pallas_call(
        functools.partial(_nsa_slc_kernel, tq=tq, tk=tk, group=group),
        grid=(b, NSA_KV_HEADS, seq // tq),
        in_specs=[pl.BlockSpec(memory_space=pltpu.SMEM),
                  pl.BlockSpec((1, tq, gw), lambda bi, h, i: (bi, i, h)),
                  pl.BlockSpec((1, seq, 2 * LANES), lambda bi, h, i: (bi, 0, 0)),
                  pl.BlockSpec((1, seq, LANES), lambda bi, h, i: (bi, 0, h)),
                  pl.BlockSpec((1, 1, tq, LANES), lambda bi, h, i: (bi, h, i, 0)),
                  pl.BlockSpec((1, tq, LANES), lambda bi, h, i: (bi, i, 0))],
        out_specs=pl.BlockSpec((1, tq, group * HEAD_DIM), lambda bi, h, i: (bi, i, h)),
        out_shape=jax.ShapeDtypeStruct((b, seq, NSA_HEADS * HEAD_DIM), F32),
        scratch_shapes=[pltpu.VMEM((group // 2, 2 * tq, tk), F32), pltpu.VMEM((group // 2, 2 * tq, tk), F32)],
        compiler_params=_params(("arbitrary", "arbitrary", "arbitrary")),
        name="nsa_selected",
    )(gate_b, q_rot, k_aug, v_exp, sel, gates)


def _window_kernel(sc_ref, q_ref, k_ref, v_ref, gl_ref, o_ref, sa_ref, sb_ref, *, tq, nt, span, window, group,
                   gated, sinks):
    hk = pl.program_id(1)
    sblk = pl.program_id(2)
    chains = group // 2

    def tile(it):
        gi = sblk * nt + it
        start = pl.multiple_of(jnp.maximum(gi * tq + tq - span, 0), tq)
        return gi, start, pl.ds(pl.multiple_of(it * tq, tq), tq)

    def logits_into(buf, it):
        _, start, rows = tile(it)
        kt = k_ref[0, pl.ds(start, span), :]
        for c in range(chains):
            lhs = jnp.concatenate([q_ref[0, rows, g * LANES:(g + 1) * LANES] for g in (2 * c, 2 * c + 1)], axis=0)
            buf[c] = _dot_nt(lhs, kt)

    def consume(buf, it):
        gi, start, rows = tile(it)
        vt = v_ref[0, pl.ds(start, span), :]
        t = gi * tq + lax.broadcasted_iota(jnp.int32, (tq, span), 0)
        key = start + lax.broadcasted_iota(jnp.int32, (tq, span), 1)
        ok = (key <= t) & (t - key < window)
        ok = jnp.concatenate([ok, ok], axis=0)
        gl = gl_ref[0, rows, :]
        heads = []
        for c in range(chains):
            s = jnp.where(ok, buf[c], NEG_INF)
            m = jnp.max(s, axis=-1, keepdims=True)
            if sinks:
                sk = jnp.concatenate([jnp.full((tq, 1), sc_ref[hk * group + g] * LOG2E, F32)
                                      for g in (2 * c, 2 * c + 1)], axis=0)
                m = jnp.maximum(m, sk)
            acc = _dot(jnp.exp2(s - m).astype(BF16), vt)
            denom = acc[:, HEAD_DIM:HEAD_DIM + 1]
            if sinks:
                denom = denom + jnp.exp2(sk - m)
            for r in range(2):
                g = 2 * c + r
                scale = 1.0 / denom[r * tq:(r + 1) * tq]
                if gated:
                    colg = (hk * group + g) * 3 + 2
                    scale = scale * _sigmoid(_gate_column(gl, colg) + sc_ref[colg])
                heads.append(acc[r * tq:(r + 1) * tq] * scale)
        o_ref[0, rows, :] = _compact_group(heads, None)

    logits_into(sa_ref, 0)

    def pair(jj, carry):
        it = 2 * jj
        logits_into(sb_ref, it + 1)
        consume(sa_ref, it)
        logits_into(sa_ref, jnp.minimum(it + 2, nt - 1))
        consume(sb_ref, it + 1)
        return carry

    lax.fori_loop(0, nt // 2, pair, 0)


def _window_attention(q_rot, q_blk0, k_pairs, kv_blk, v_heads, scalars, gates, *, seq, window, gated, sinks,
                      tq=128, rows=2048):
    b = q_rot.shape[0]
    group = 4
    span = min(window + tq, seq)
    rows = min(rows, seq)
    nt = rows // tq
    assert nt % 2 == 0
    gw = group * LANES
    qb = q_blk0 // group
    return pl.pallas_call(
        functools.partial(_window_kernel, tq=tq, nt=nt, span=span, window=window, group=group, gated=gated,
                          sinks=sinks),
        grid=(b, 2, seq // rows),
        in_specs=[pl.BlockSpec(memory_space=pltpu.SMEM),
                  pl.BlockSpec((1, rows, gw), lambda bi, h, i: (bi, i, qb + h)),
                  pl.BlockSpec((1, seq, LANES), lambda bi, h, i: (bi, 0, kv_blk)),
                  pl.BlockSpec((1, seq, LANES), lambda bi, h, i: (bi, 0, 2 * kv_blk + h)),
                  pl.BlockSpec((1, rows, LANES), lambda bi, h, i: (bi, i, 0))],
        out_specs=pl.BlockSpec((1, rows, group * HEAD_DIM), lambda bi, h, i: (bi, i, h)),
        out_shape=jax.ShapeDtypeStruct((b, seq, 8 * HEAD_DIM), F32),
        scratch_shapes=[pltpu.VMEM((group // 2, 2 * tq, span), F32), pltpu.VMEM((group // 2, 2 * tq, span), F32)],
        compiler_params=_params(("arbitrary", "arbitrary", "arbitrary")),
        name="window_attention",
    )(scalars, q_rot, k_pairs, v_heads, gates)


def _dense_kernel(q_ref, k_ref, v_ref, o_ref, sa_ref, sb_ref, *, tq, tk, nh):
    i = pl.program_id(2)
    t0 = i * tq
    n_full = t0 // tk
    qs = [q_ref[0][:, e * LANES:(e + 1) * LANES] for e in range(nh)]

    def logits_into(buf, j):
        off = pl.multiple_of(j * tk, tk)
        for e in range(nh):
            buf[e] = _dot_nt(qs[e], k_ref[0, pl.ds(off, tk), e * LANES:(e + 1) * LANES])

    def consume(buf, j, states, masked):
        off = pl.multiple_of(j * tk, tk)
        out = []
        for e in range(nh):
            m, acc = states[e]
            s = buf[e]
            if masked:
                t = t0 + lax.broadcasted_iota(jnp.int32, (tq, tk), 0)
                key = j * tk + lax.broadcasted_iota(jnp.int32, (tq, tk), 1)
                s = jnp.where(key <= t, s, NEG_INF)
            vt = v_ref[0, pl.ds(off, tk), e * LANES:(e + 1) * LANES]
            m_new = jnp.maximum(m, jnp.max(s, axis=-1, keepdims=True))
            p = jnp.exp2(s - m_new)
            out.append((m_new, jnp.exp2(m - m_new) * acc + _dot(p.astype(BF16), vt)))
        return tuple(out)

    def pair(jj, states):
        j = 2 * jj
        logits_into(sb_ref, j + 1)
        states = consume(sa_ref, j, states, False)
        logits_into(sa_ref, j + 2)
        return consume(sb_ref, j + 1, states, False)

    init = tuple((jnp.full((tq, 1), NEG_INF, F32), jnp.zeros((tq, LANES), F32)) for _ in range(nh))
    logits_into(sa_ref, 0)
    states = lax.fori_loop(0, n_full // 2, pair, init)
    r = 2 * (n_full // 2)

    def diagonal_is_next(states):
        return consume(sa_ref, r, states, True)

    def one_full_tile_left(states):
        logits_into(sb_ref, r + 1)
        return consume(sb_ref, r + 1, consume(sa_ref, r, states, False), True)

    states = lax.cond(r == n_full, diagonal_is_next, one_full_tile_left, states)
    outs = [acc * (1.0 / acc[:, HEAD_DIM:HEAD_DIM + 1]) for _, acc in states]
    lane = _lane((tq, LANES))
    o_ref[0] = jnp.concatenate([jnp.where(lane < HEAD_DIM, outs[e], pltpu.roll(outs[e + 1], HEAD_DIM, 1))
                                for e in range(0, nh, 2)], axis=1)


def _dense_attention(q, k, v, *, seq, tq=256, tk=512, nh=2):
    b = q.shape[0]
    heads = q.shape[2] // LANES
    tk = min(tk, seq)
    tq = min(tq, tk)
    return pl.pallas_call(
        functools.partial(_dense_kernel, tq=tq, tk=tk, nh=nh),
        grid=(b, heads // nh, seq // tq),
        in_specs=[pl.BlockSpec((1, tq, nh * LANES), lambda bi, p, i: (bi, i, p)),
                  pl.BlockSpec((1, seq, nh * LANES), lambda bi, p, i: (bi, 0, p)),
                  pl.BlockSpec((1, seq, nh * LANES), lambda bi, p, i: (bi, 0, p))],
        out_specs=pl.BlockSpec((1, tq, nh * HEAD_DIM), lambda bi, p, i: (bi, i, p)),
        out_shape=jax.ShapeDtypeStruct((b, seq, heads * HEAD_DIM), F32),
        scratch_shapes=[pltpu.VMEM((nh, tq, tk), F32), pltpu.VMEM((nh, tq, tk), F32)],
        compiler_params=_params(("arbitrary", "arbitrary", "arbitrary")),
        name="dense_causal_attention",
    )(q, k, v)


def _decay_kernel(f_ref, b_ref, o_ref):
    x = f_ref[0] + b_ref[...]
    lf = jnp.minimum(x, 0.0) - jnp.log1p(jnp.exp(-jnp.abs(x)))
    n = lf.shape[-1]
    lane = _lane(lf.shape)
    d = 1
    while d < n:
        lf = lf + jnp.where(lane >= d, pltpu.roll(lf, d, 1), 0.0)
        d *= 2
    o_ref[0] = lf * LOG2E


def _decay_cumsum(f_t, bias):
    b, h, s = f_t.shape
    return pl.pallas_call(
        _decay_kernel,
        grid=(b,),
        in_specs=[pl.BlockSpec((1, h, s), lambda i: (i, 0, 0)), pl.BlockSpec((h, 1), lambda i: (0, 0))],
        out_specs=pl.BlockSpec((1, h, s), lambda i: (i, 0, 0)),
        out_shape=jax.ShapeDtypeStruct((b, h, s), F32),
        compiler_params=_params(("arbitrary",)),
        name="fox_decay_cumsum",
    )(f_t, bias.reshape(h, 1).astype(F32))


def _mla_prep_kernel(cq_ref, ckv_ref, misc_ref, gqa_ref, gkva_ref, wq_ref, wk_ref, wv_ref, gq_ref, gk_ref,
                     cos_ref, sin_ref, q_ref, k_ref, v_ref, *, scale):
    tm = cq_ref.shape[0]
    lane = _lane((tm, LANES))
    in_rope = (lane >= MLA_NOPE_DIM) & (lane < MLA_QK_DIM)
    first = lane < MLA_NOPE_DIM + MLA_ROPE_DIM // 2
    cos, sin = cos_ref[...], sin_ref[...]

    def rope_tail(x):
        sw = jnp.where(first, pltpu.roll(x, LANES - MLA_ROPE_DIM // 2, 1), pltpu.roll(x, MLA_ROPE_DIM // 2, 1))
        return x * cos + jnp.where(in_rope, sw, 0.0) * sin

    cq = _rms(cq_ref[...], gqa_ref[...], MLA_Q_RANK).astype(BF16)
    ckv = _rms(ckv_ref[...], gkva_ref[...], MLA_KV_RANK).astype(BF16)
    qa = _dot(cq, wq_ref[...])
    ka = _dot(ckv, wk_ref[...])
    k_rope = jnp.where(in_rope, misc_ref[...], 0.0)
    for h in range(MLA_HEADS):
        qh = _rms(qa[:, h * LANES:(h + 1) * LANES], gq_ref[...], MLA_QK_DIM)
        q_ref[:, h * LANES:(h + 1) * LANES] = (rope_tail(qh) * scale).astype(q_ref.dtype)
        kh = _rms(ka[:, h * LANES:(h + 1) * LANES] + k_rope, gk_ref[...], MLA_QK_DIM)
        k_ref[:, h * LANES:(h + 1) * LANES] = rope_tail(kh).astype(k_ref.dtype)
    v = _dot(ckv, wv_ref[...])
    ones_col = (_lane(v.shape) & (LANES - 1)) == MLA_V_DIM
    v_ref[...] = jnp.where(ones_col, 1.0, v).astype(v_ref.dtype)


def _mla_prep(y, cq_blk, ckv_blk, misc_blk, q_a_norm, w_q_b, kv_a_norm, w_kv_b, q_norm, k_norm, *, seq, tm=512):
    n = y.shape[0]
    h = MLA_HEADS
    pad = LANES - MLA_QK_DIM
    wq = jnp.pad(w_q_b.reshape(MLA_Q_RANK, h, MLA_QK_DIM), ((0, 0), (0, 0), (0, pad)))
    wq = wq.reshape(MLA_Q_RANK, h * LANES).astype(BF16)
    wkv = w_kv_b.reshape(MLA_KV_RANK, h, MLA_NOPE_DIM + MLA_V_DIM)
    wk = jnp.pad(wkv[:, :, :MLA_NOPE_DIM], ((0, 0), (0, 0), (0, LANES - MLA_NOPE_DIM)))
    wk = wk.reshape(MLA_KV_RANK, h * LANES).astype(BF16)
    wv = jnp.pad(wkv[:, :, MLA_NOPE_DIM:], ((0, 0), (0, 0), (0, LANES - MLA_V_DIM)))
    wv = wv.reshape(MLA_KV_RANK, h * LANES).astype(BF16)
    gq = jnp.pad(q_norm, (0, pad)).reshape(1, LANES)
    gk = jnp.pad(k_norm, (0, pad)).reshape(1, LANES)
    half = MLA_ROPE_DIM // 2
    inv_freq = 1.0 / (ROPE_THETA ** (jnp.arange(0, MLA_ROPE_DIM, 2, dtype=F32) / MLA_ROPE_DIM))
    ang = jnp.arange(seq, dtype=F32)[:, None] * inv_freq[None, :]
    cos, sin = jnp.cos(ang), jnp.sin(ang)
    ones = jnp.ones((seq, MLA_NOPE_DIM), F32)
    zeros = jnp.zeros((seq, MLA_NOPE_DIM), F32)
    cos_t = jnp.concatenate([ones, cos, cos, ones[:, :pad]], axis=1)
    sin_t = jnp.concatenate([zeros, -sin, sin, zeros[:, :pad]], axis=1)
    del half
    sblocks = seq // tm
    full = lambda shp: pl.BlockSpec(shp, lambda i: (0,) * len(shp))
    q, k, v = pl.pallas_call(
        functools.partial(_mla_prep_kernel, scale=MLA_QK_DIM ** -0.5 * LOG2E),
        grid=(n // tm,),
        in_specs=[pl.BlockSpec((tm, MLA_Q_RANK), lambda i: (i, cq_blk // 2)),
                  pl.BlockSpec((tm, LANES), lambda i: (i, ckv_blk)),
                  pl.BlockSpec((tm, LANES), lambda i: (i, misc_blk)),
                  full((1, MLA_Q_RANK)), full((1, MLA_KV_RANK)),
                  full((MLA_Q_RANK, h * LANES)), full((MLA_KV_RANK, h * LANES)), full((MLA_KV_RANK, h * LANES)),
                  full((1, LANES)), full((1, LANES)),
                  pl.BlockSpec((tm, LANES), lambda i: (i % sblocks, 0)),
                  pl.BlockSpec((tm, LANES), lambda i: (i % sblocks, 0))],
        out_specs=[pl.BlockSpec((tm, h * LANES), lambda i: (i, 0)),
                   pl.BlockSpec((tm, h * LANES), lambda i: (i, 0)),
                   pl.BlockSpec((tm, h * LANES), lambda i: (i, 0))],
        out_shape=[jax.ShapeDtypeStruct((n, h * LANES), BF16),
                   jax.ShapeDtypeStruct((n, h * LANES), BF16),
                   jax.ShapeDtypeStruct((n, h * LANES), BF16)],
        compiler_params=_params(("arbitrary",)),
        name="mla_prep",
    )(y, y, y, q_a_norm.reshape(1, -1), kv_a_norm.reshape(1, -1), wq, wk, wv, gq, gk, cos_t, sin_t)
    return q, k, v


def _cols(w, a, b):
    return w[:, a:b]


def _nsa_swa_mixer(x, batch, seq, mix_norm, w_in, nsa_gate_b, nsa_q_norm, nsa_kc_norm, nsa_ks_norm, nsa_kw_norm,
                   cmp_pos_k, cmp_pos_v, cmpk_w1, cmpk_w2, cmpv_w1, cmpv_w2,
                   swa_q_norm, swa_k_norm, swa_sinks, w_out):
    n = batch * seq
    o = np.cumsum([0, 512, 128, 128, 128, 128, 128, 128, 24, 512, 128, 128])
    seg = lambda j: _cols(w_in, o[j], o[j + 1])
    q_a, kc, vc, ks, vs, kw, vw, gl, q_b, k_b, v_b = [seg(j) for j in range(11)]
    gl = jnp.pad(gl, ((0, 0), (0, LANES - gl.shape[1])))
    w = jnp.concatenate([q_a, q_b, ks, kw, k_b, kc, vc, vs, vw, v_b, gl], axis=1).astype(BF16)
    y = _rms_matmul(x, mix_norm, w)
    cos_t, sin_t = _rope_tables(seq)

    s_q = HEAD_DIM ** -0.5 * LOG2E
    gains = jnp.stack([_pair_gain(g) for g in (nsa_q_norm, swa_q_norm, nsa_ks_norm, nsa_kw_norm, swa_k_norm)])
    jobs = [_Job(blk=c, out=0, col=2 * c, gain=c // 4, rope=True, scale=s_q, mode="q", dst=((c % 4) // 2,) * 2)
            for c in range(8)]
    jobs += [_Job(blk=c, out=1, col=2 * c, gain=0, scale=s_q, mode="q", dst=(c // 2,) * 2)
             for c in range(4)]
    jobs += [_Job(blk=8, out=2, col=0, gain=2, rope=True, mode="kaug"),
             _Job(blk=9, out=3, col=0, gain=3, rope=True), _Job(blk=10, out=3, col=1, gain=4, rope=True),
             _Job(blk=13, out=4, col=0, mode="v"),
             _Job(blk=14, out=5, col=0, mode="v"), _Job(blk=15, out=5, col=2, mode="v"),
             _Job(blk=11, out=6, col=0), _Job(blk=12, out=7, col=0), _Job(blk=16, out=8, col=0)]
    outs = [(16, BF16), (8, BF16), (2, BF16), (2, BF16), (2, BF16), (4, BF16), (1, F32), (1, F32), (1, F32)]
    q_rot, q_cmp, k_aug, k_ws, v_slc, v_ws, kc_raw, vc_raw, gates = (
        a.reshape(batch, seq, -1) for a in _prep(y, jobs, outs, gains, seq, cos_t, sin_t))

    k_cmp = _compress(kc_raw, cmp_pos_k, cmpk_w1, cmpk_w2, nsa_kc_norm)
    v_cmp = _compress(vc_raw, cmp_pos_v, cmpv_w1, cmpv_w2, None)
    gate_b = nsa_gate_b.astype(F32)

    o_cmp, sel = _nsa_compressed(q_cmp, k_cmp, v_cmp, gates, gate_b, seq=seq)
    o_slc = _nsa_selected(q_rot, k_aug, v_slc, sel, gates, gate_b, seq=seq)
    o_win = _window_attention(q_rot, 0, k_ws, 0, v_ws, gate_b, gates, seq=seq, window=NSA_WINDOW,
                              gated=True, sinks=False)
    o_swa = _window_attention(q_rot, 8, k_ws, 1, v_ws, swa_sinks.astype(F32), gates, seq=seq, window=SWA_WINDOW,
                              gated=False, sinks=True)
    flat = lambda a: a.reshape(n, -1)
    return _outproj(x, [flat(o_cmp), flat(o_slc), flat(o_win)], [flat(o_swa)], w_out)


def _fox_mla_mixer(x, batch, seq, mix_norm, w_in, fox_f_bias, fox_q_norm, fox_k_norm, mla_q_a_norm, mla_w_q_b,
                   mla_kv_a_norm, mla_w_kv_b, mla_q_norm, mla_k_norm, w_out):
    n = batch * seq
    o = np.cumsum([0, 512, 512, 512, 8, 256, 128, 32])
    seg = lambda j: _cols(w_in, o[j], o[j + 1])
    q_c, k_c, v_c, f_c, c_q, c_kv, k_r = [seg(j) for j in range(7)]
    d = w_in.shape[0]
    misc = jnp.concatenate([f_c, jnp.zeros((d, MLA_NOPE_DIM - 8), w_in.dtype), k_r,
                            jnp.zeros((d, LANES - MLA_QK_DIM), w_in.dtype)], axis=1)
    w = jnp.concatenate([q_c, k_c, v_c, c_q, c_kv, misc], axis=1).astype(BF16)
    y = _rms_matmul(x, mix_norm, w)
    cos_t, sin_t = _rope_tables(seq)

    y3 = y.reshape(batch, seq, y.shape[1])
    f_t = y3[:, :, 15 * LANES:15 * LANES + FOX_HEADS].transpose(0, 2, 1)
    dc = _decay_cumsum(f_t, fox_f_bias)
    dc_tok = jnp.pad(dc.transpose(0, 2, 1).reshape(n, FOX_HEADS), ((0, 0), (0, LANES - FOX_HEADS)))
    gains = jnp.stack([_pair_gain(fox_q_norm), _pair_gain(fox_k_norm)])
    jobs = [_Job(blk=c, out=0, col=2 * c, gain=0, scale=HEAD_DIM ** -0.5 * LOG2E, mode="q", aug="ones")
            for c in range(4)]
    jobs += [_Job(blk=8 + c, out=1, col=2 * c, mode="v") for c in range(4)]
    q_f, v_f = _prep(y, jobs, [(8, BF16), (8, BF16)], gains, seq, cos_t, sin_t)
    kjobs = [_Job(blk=4 + c, out=0, col=2 * c, gain=1, mode="q", aug="decay", heads=(2 * c, 2 * c + 1))
             for c in range(4)]
    (k_f,) = _prep(y, kjobs, [(8, BF16)], gains, seq, cos_t, sin_t, aux=dc_tok)
    b3 = lambda a: a.reshape(batch, seq, -1)
    o_fox = _dense_attention(b3(q_f), b3(k_f), b3(v_f), seq=seq)

    q_m, k_m, v_m = _mla_prep(y, 12, 14, 15, mla_q_a_norm, mla_w_q_b, mla_kv_a_norm, mla_w_kv_b,
                              mla_q_norm, mla_k_norm, seq=seq)
    o_mla = _dense_attention(b3(q_m), b3(k_m), b3(v_m), seq=seq)
    flat = lambda a: a.reshape(n, -1)
    return _outproj(x, [flat(o_fox)], [flat(o_mla)], w_out)


def kernel(x, l0_ffn1_norm, l0_ffn1_w_gate, l0_ffn1_w_up, l0_ffn1_w_down, l0_mix_norm, l0_w_in, l0_nsa_gate_b, l0_nsa_q_norm, l0_nsa_kc_norm, l0_nsa_ks_norm, l0_nsa_kw_norm, l0_cmp_pos_k, l0_cmp_pos_v, l0_cmpk_w1, l0_cmpk_w2, l0_cmpv_w1, l0_cmpv_w2, l0_swa_q_norm, l0_swa_k_norm, l0_swa_sinks, l0_w_out, l0_ffn2_norm, l0_ffn2_w_gate, l0_ffn2_w_up, l0_ffn2_w_down, l1_ffn1_norm, l1_ffn1_w_gate, l1_ffn1_w_up, l1_ffn1_w_down, l1_mix_norm, l1_w_in, l1_fox_f_bias, l1_fox_q_norm, l1_fox_k_norm, l1_mla_q_a_norm, l1_mla_w_q_b, l1_mla_kv_a_norm, l1_mla_w_kv_b, l1_mla_q_norm, l1_mla_k_norm, l1_w_out, l1_ffn2_norm, l1_ffn2_w_gate, l1_ffn2_w_up, l1_ffn2_w_down):
    batch, seq, d = x.shape
    h = x.reshape(batch * seq, d)
    h = _ffn(h, l0_ffn1_norm, l0_ffn1_w_gate, l0_ffn1_w_up, l0_ffn1_w_down)
    h = _nsa_swa_mixer(h, batch, seq, l0_mix_norm, l0_w_in, l0_nsa_gate_b, l0_nsa_q_norm, l0_nsa_kc_norm,
                       l0_nsa_ks_norm, l0_nsa_kw_norm, l0_cmp_pos_k, l0_cmp_pos_v, l0_cmpk_w1, l0_cmpk_w2,
                       l0_cmpv_w1, l0_cmpv_w2, l0_swa_q_norm, l0_swa_k_norm, l0_swa_sinks, l0_w_out)
    h = _ffn(h, l0_ffn2_norm, l0_ffn2_w_gate, l0_ffn2_w_up, l0_ffn2_w_down)
    h = _ffn(h, l1_ffn1_norm, l1_ffn1_w_gate, l1_ffn1_w_up, l1_ffn1_w_down)
    h = _fox_mla_mixer(h, batch, seq, l1_mix_norm, l1_w_in, l1_fox_f_bias, l1_fox_q_norm, l1_fox_k_norm,
                       l1_mla_q_a_norm, l1_mla_w_q_b, l1_mla_kv_a_norm, l1_mla_w_kv_b, l1_mla_q_norm,
                       l1_mla_k_norm, l1_w_out)
    h = _ffn(h, l1_ffn2_norm, l1_ffn2_w_gate, l1_ffn2_w_up, l1_ffn2_w_down)
    return h.reshape(batch, seq, d)
```

```python
import functools
from typing import NamedTuple, Optional

import numpy as np
import jax
import jax.numpy as jnp
from jax import lax
from jax.experimental import pallas as pl
from jax.experimental.pallas import tpu as pltpu

F32 = jnp.float32
BF16 = jnp.bfloat16

HEAD_DIM = 64
LANES = 128
ROPE_THETA = 10000.0
RMS_EPS = 1e-6
NEG_INF = -1e30
FORCE_SCORE = 1e9
BELOW_ALL = -3e38
LOG2E = 1.4426950408889634
BIG = 1e30
VT_ROWS = 80

NSA_HEADS = 8
NSA_KV_HEADS = 2
CMP_BLOCK = 32
CMP_STRIDE = 16
CMP_HIDDEN = 256
SLC_BLOCK = 64
N_SELECT = 16
NSA_WINDOW = 512
SWA_HEADS = 8
SWA_KV_HEADS = 2
SWA_WINDOW = 128
FOX_HEADS = 8
MLA_HEADS = 8
MLA_Q_RANK = 256
MLA_KV_RANK = 128
MLA_NOPE_DIM = 64
MLA_ROPE_DIM = 32
MLA_V_DIM = 64
MLA_QK_DIM = MLA_NOPE_DIM + MLA_ROPE_DIM

VMEM_LIMIT = 48 * 1024 * 1024

NT_DIMS = (((1,), (1,)), ((), ()))


def _params(sem):
    return pltpu.CompilerParams(dimension_semantics=sem, vmem_limit_bytes=VMEM_LIMIT)


def _dot(a, b):
    return jnp.dot(a, b, preferred_element_type=F32)


def _dot_nt(a, b):
    return lax.dot_general(a, b, NT_DIMS, preferred_element_type=F32)


def _rms(x, gain, n):
    ms = jnp.sum(x * x, axis=-1, keepdims=True) * (1.0 / n)
    return (x * lax.rsqrt(ms + RMS_EPS)) * gain


def _lane(shape):
    return lax.broadcasted_iota(jnp.int32, shape, len(shape) - 1)


def _split_dot(x, m):
    hi = x.astype(BF16)
    lo = (x - hi.astype(F32)).astype(BF16)
    return _dot(hi, m) + _dot(lo, m)


def _group_sum_matrix(width):
    lane = np.arange(LANES)
    return jnp.asarray(lane[:, None] // width == lane[None, :] // width, BF16)


def _ffn_kernel(x_ref, g_ref, wg_ref, wu_ref, wd_ref, o_ref, h_sc, acc_sc, *, tf):
    x = x_ref[...]
    h_sc[...] = _rms(x, g_ref[...], x.shape[-1]).astype(BF16)
    acc_sc[...] = jnp.zeros_like(acc_sc)
    f = wg_ref.shape[1]
    for c0 in range(0, f, tf):
        cols = slice(c0, min(c0 + tf, f))
        h = h_sc[...]
        g = _dot(h, wg_ref[:, cols])
        u = _dot(h, wu_ref[:, cols])
        a = (g * (1.0 / (1.0 + jnp.exp(-g)))) * u
        acc_sc[...] += _dot(a.astype(BF16), wd_ref[cols, :])
    o_ref[...] = x + 0.5 * acc_sc[...]


def _ffn(x, norm, w_gate, w_up, w_down, *, tm=512, tf=256):
    n, d = x.shape
    f = w_gate.shape[1]
    wg, wu, wd = w_gate.astype(BF16), w_up.astype(BF16), w_down.astype(BF16)
    wspec = lambda shp: pl.BlockSpec(shp, lambda i: (0, 0), pipeline_mode=pl.Buffered(1))
    return pl.pallas_call(
        functools.partial(_ffn_kernel, tf=tf),
        grid=(n // tm,),
        in_specs=[pl.BlockSpec((tm, d), lambda i: (i, 0)),
                  pl.BlockSpec((1, d), lambda i: (0, 0)),
                  wspec((d, f)), wspec((d, f)), wspec((f, d))],
        out_specs=pl.BlockSpec((tm, d), lambda i: (i, 0)),
        out_shape=jax.ShapeDtypeStruct((n, d), F32),
        scratch_shapes=[pltpu.VMEM((tm, d), BF16), pltpu.VMEM((tm, d), F32)],
        compiler_params=_params(("arbitrary",)),
        name="ffn",
    )(x, norm.reshape(1, d), wg, wu, wd)


def _rms_matmul_kernel(x_ref, g_ref, w_ref, o_ref):
    x = x_ref[...]
    h = _rms(x, g_ref[...], x.shape[-1]).astype(BF16)
    o_ref[...] = _dot(h, w_ref[...])


def _rms_matmul(x, norm, w, *, tm=512):
    n, d = x.shape
    c = w.shape[1]
    return pl.pallas_call(
        _rms_matmul_kernel,
        grid=(n // tm,),
        in_specs=[pl.BlockSpec((tm, d), lambda i: (i, 0)),
                  pl.BlockSpec((1, d), lambda i: (0, 0)),
                  pl.BlockSpec((d, c), lambda i: (0, 0), pipeline_mode=pl.Buffered(1))],
        out_specs=pl.BlockSpec((tm, c), lambda i: (i, 0)),
        out_shape=jax.ShapeDtypeStruct((n, c), F32),
        compiler_params=_params(("arbitrary",)),
        name="rms_matmul",
    )(x, norm.reshape(1, d), w)


def _outproj_kernel(*refs, n_a, n_b):
    x_ref = refs[0]
    a_refs = refs[1:1 + n_a]
    b_refs = refs[1 + n_a:1 + n_a + n_b]
    wa_ref, wb_ref, o_ref = refs[1 + n_a + n_b:]
    a = a_refs[0][...]
    for r in a_refs[1:]:
        a = a + r[...]
    b = b_refs[0][...]
    for r in b_refs[1:]:
        b = b + r[...]
    o_ref[...] = x_ref[...] + _dot(a.astype(BF16), wa_ref[...]) + _dot(b.astype(BF16), wb_ref[...])


def _outproj(x, a_list, b_list, w_out, *, tm=512):
    n, d = x.shape
    ca = a_list[0].shape[1]
    cb = b_list[0].shape[1]
    wa = w_out[:ca].astype(BF16)
    wb = w_out[ca:].astype(BF16)
    row = lambda c: pl.BlockSpec((tm, c), lambda i: (i, 0))
    return pl.pallas_call(
        functools.partial(_outproj_kernel, n_a=len(a_list), n_b=len(b_list)),
        grid=(n // tm,),
        in_specs=[row(d)] + [row(ca)] * len(a_list) + [row(cb)] * len(b_list)
                 + [pl.BlockSpec((ca, d), lambda i: (0, 0)), pl.BlockSpec((cb, d), lambda i: (0, 0))],
        out_specs=row(d),
        out_shape=jax.ShapeDtypeStruct((n, d), F32),
        compiler_params=_params(("arbitrary",)),
        name="outproj",
    )(x, *a_list, *b_list, wa, wb)


class _Job(NamedTuple):
    blk: int
    out: int
    col: int
    gain: Optional[int] = None
    rope: bool = False
    scale: float = 1.0
    mode: str = "plain"
    dst: tuple = (0, 0)
    aug: Optional[str] = None
    heads: tuple = (0, 0)


def _prep_kernel(y_ref, gain_ref, cos_ref, sin_ref, aux_ref, hsum_ref, *o_refs, jobs, seq):
    tm = y_ref.shape[0]
    lane = _lane((tm, LANES))
    low = lane < HEAD_DIM
    zero = jnp.zeros((tm, LANES), F32)
    for job in jobs:
        x = y_ref[:, job.blk * LANES:(job.blk + 1) * LANES]
        if job.gain is not None:
            ms = _split_dot(x * x, hsum_ref[...]) * (1.0 / HEAD_DIM)
            x = (x * lax.rsqrt(ms + RMS_EPS)) * gain_ref[job.gain]
        if job.rope:
            swapped = jnp.where((lane & (HEAD_DIM - 1)) < HEAD_DIM // 2,
                                pltpu.roll(x, LANES - HEAD_DIM // 2, 1), pltpu.roll(x, HEAD_DIM // 2, 1))
            x = x * cos_ref[...] + swapped * sin_ref[...]
        if job.scale != 1.0:
            x = x * job.scale
        if job.mode == "plain":
            pieces = [x]
        elif job.mode == "kaug":
            pos = (pl.program_id(0) * tm + lax.broadcasted_iota(jnp.int32, (tm, LANES), 0)) % seq
            pieces = [x, jnp.where(lane == pos // SLC_BLOCK, BIG, 0.0)]
        elif job.mode == "v":
            r = pltpu.roll(x, HEAD_DIM, 1)
            tail = jnp.where(lane == HEAD_DIM, 1.0, 0.0)
            pieces = [jnp.where(low, x, tail), jnp.where(low, r, tail)]
        else:
            r = pltpu.roll(x, HEAD_DIM, 1)
            h_even = jnp.where(low, x, zero) if job.dst[0] == 0 else jnp.where(low, zero, r)
            h_odd = jnp.where(low, r, zero) if job.dst[1] == 0 else jnp.where(low, zero, x)
            if job.aug is not None:
                tails = []
                for e in range(2):
                    if job.aug == "ones":
                        tails.append(jnp.where((lane >= HEAD_DIM) & (lane < HEAD_DIM + 3), 1.0, 0.0))
                    else:
                        d = jnp.sum(jnp.where(lane == job.heads[e], aux_ref[...], 0.0), axis=-1, keepdims=True)
                        hi = d.astype(BF16).astype(F32)
                        mid = (d - hi).astype(BF16).astype(F32)
                        lo = d - hi - mid
                        tails.append(jnp.where(lane == HEAD_DIM, -hi, jnp.where(lane == HEAD_DIM + 1, -mid,
                                     jnp.where(lane == HEAD_DIM + 2, -lo, 0.0))))
                h_even = jnp.where(low, h_even, tails[0])
                h_odd = jnp.where(low, h_odd, tails[1])
            pieces = [h_even, h_odd]
        o_ref = o_refs[job.out]
        for n, piece in enumerate(pieces):
            o_ref[:, (job.col + n) * LANES:(job.col + n + 1) * LANES] = piece.astype(o_ref.dtype)


def _prep(y, jobs, outs, gains, seq, cos_t, sin_t, *, aux=None, tm=512):
    n, c = y.shape
    aux_spec = pl.BlockSpec((tm, LANES), lambda i: (i, 0))
    if aux is None:
        aux, aux_spec = jnp.zeros((tm, LANES), F32), pl.BlockSpec((tm, LANES), lambda i: (0, 0))
    sblocks = seq // tm
    return pl.pallas_call(
        functools.partial(_prep_kernel, jobs=tuple(jobs), seq=seq),
        grid=(n // tm,),
        in_specs=[pl.BlockSpec((tm, c), lambda i: (i, 0)),
                  pl.BlockSpec(gains.shape, lambda i: (0, 0, 0)),
                  pl.BlockSpec((tm, LANES), lambda i: (i % sblocks, 0)),
                  pl.BlockSpec((tm, LANES), lambda i: (i % sblocks, 0)),
                  aux_spec,
                  pl.BlockSpec((LANES, LANES), lambda i: (0, 0))],
        out_specs=[pl.BlockSpec((tm, w * LANES), lambda i: (i, 0)) for w, _ in outs],
        out_shape=[jax.ShapeDtypeStruct((n, w * LANES), dt) for w, dt in outs],
        compiler_params=_params(("arbitrary",)),
        name="head_prep",
    )(y, gains, cos_t, sin_t, aux, _group_sum_matrix(HEAD_DIM))


def _pair_gain(g):
    return jnp.concatenate([g, g]).reshape(1, LANES).astype(F32)


def _rope_tables(seq):
    half = HEAD_DIM // 2
    inv_freq = 1.0 / (ROPE_THETA ** (jnp.arange(0, HEAD_DIM, 2, dtype=F32) / HEAD_DIM))
    ang = jnp.arange(seq, dtype=F32)[:, None] * inv_freq[None, :]
    cos, sin = jnp.cos(ang), jnp.sin(ang)
    cos_t = jnp.concatenate([cos, cos, cos, cos], axis=1)
    sin_t = jnp.concatenate([-sin, sin, -sin, sin], axis=1)
    del half
    return cos_t, sin_t


def _compress_kernel(ch_ref, ptop_ref, pbot_ref, w1t_ref, w1b_ref, w2_ref, gain_ref, o_ref, *, norm):
    ch = ch_ref[0]
    a = _dot((ch + ptop_ref[...]).astype(BF16), w1t_ref[...])
    b = _dot((ch + pbot_ref[...]).astype(BF16), w1b_ref[...])
    nc = a.shape[0]
    hid = a + pltpu.roll(b, nc - 1, 0)
    act = hid * (1.0 / (1.0 + jnp.exp(-hid)))
    out = _dot(act.astype(BF16), w2_ref[...])
    if norm:
        lane = _lane(out.shape)
        low = lane < HEAD_DIM
        o2 = out * out
        s_lo = jnp.sum(jnp.where(low, o2, 0.0), axis=-1, keepdims=True)
        s_hi = jnp.sum(jnp.where(low, 0.0, o2), axis=-1, keepdims=True)
        ms = jnp.where(low, s_lo, s_hi) * (1.0 / HEAD_DIM)
        out = (out * lax.rsqrt(ms + RMS_EPS)) * gain_ref[...]
    o_ref[0] = out.astype(o_ref.dtype)


def _compress(t_pair, pos_emb, w1, w2, gain):
    b, s, _ = t_pair.shape
    nc = s // CMP_STRIDE
    hid = w1.shape[1]
    ch = t_pair.reshape(b, nc, CMP_STRIDE * LANES)
    eye2 = jnp.eye(2, dtype=F32)
    w1r = w1.reshape(CMP_BLOCK, HEAD_DIM, hid)
    def expand_w1(w):
        return jnp.einsum('pdj,kl->pkdlj', w, eye2).reshape(CMP_STRIDE * LANES, 2 * hid).astype(BF16)
    w1t, w1b = expand_w1(w1r[:CMP_STRIDE]), expand_w1(w1r[CMP_STRIDE:])
    w2e = jnp.einsum('jd,kl->kjld', w2, eye2).reshape(2 * hid, LANES).astype(BF16)
    def expand_pos(p):
        return jnp.broadcast_to(p[:, None, :], (CMP_STRIDE, 2, HEAD_DIM)).reshape(1, CMP_STRIDE * LANES)
    ptop, pbot = expand_pos(pos_emb[:CMP_STRIDE]), expand_pos(pos_emb[CMP_STRIDE:])
    norm = gain is not None
    g = _pair_gain(gain) if norm else jnp.ones((1, LANES), F32)
    full = lambda shp: pl.BlockSpec(shp, lambda i: (0,) * len(shp))
    return pl.pallas_call(
        functools.partial(_compress_kernel, norm=norm),
        grid=(b,),
        in_specs=[pl.BlockSpec((1, nc, CMP_STRIDE * LANES), lambda i: (i, 0, 0)),
                  full((1, CMP_STRIDE * LANES)), full((1, CMP_STRIDE * LANES)),
                  full((CMP_STRIDE * LANES, 2 * hid)), full((CMP_STRIDE * LANES, 2 * hid)),
                  full((2 * hid, LANES)), full((1, LANES))],
        out_specs=pl.BlockSpec((1, nc, LANES), lambda i: (i, 0, 0)),
        out_shape=jax.ShapeDtypeStruct((b, nc, LANES), BF16),
        compiler_params=_params(("arbitrary",)),
        name="nsa_compress",
    )(ch, ptop, pbot, w1t, w1b, w2e, g)


def _stack_heads(q, n):
    return jnp.concatenate([q[:, g * LANES:(g + 1) * LANES] for g in range(n)], axis=0)


def _gate_column(gl, col):
    lane = _lane(gl.shape)
    return jnp.sum(jnp.where(lane == col, gl, 0.0), axis=-1, keepdims=True)


def _sigmoid(x):
    return 1.0 / (1.0 + jnp.exp(-x))


def _compact_group(heads, hk):
    tq = heads[0].shape[0]
    lane = _lane((tq, LANES))
    low = lane < HEAD_DIM
    outs = []
    for e in range(0, len(heads), 2):
        he, ho = heads[e], heads[e + 1]
        ho_r = pltpu.roll(ho, HEAD_DIM, 1)
        if hk is None:
            lo_part, hi_part = he, ho_r
        else:
            at_low = jnp.broadcast_to(hk, (tq, LANES)) == 0
            lo_part = jnp.where(at_low, he, pltpu.roll(he, HEAD_DIM, 1))
            hi_part = jnp.where(at_low, ho_r, ho)
        outs.append(jnp.where(low, lo_part, hi_part))
    return jnp.concatenate(outs, axis=1)


def _nsa_cmp_kernel(gb_ref, q_ref, k_ref, v_ref, ov_ref, gl_ref, o_ref, sel_ref, *, tq, group, n_sel, ns):
    hk = pl.program_id(1)
    i = pl.program_id(2)
    q4 = _stack_heads(q_ref[0], group)
    kc = k_ref[0]
    ncp = kc.shape[0]
    logits = _dot_nt(q4, kc).reshape(group, tq, ncp)
    t = i * tq + lax.broadcasted_iota(jnp.int32, (tq, ncp), 0)
    cmp_end = lax.broadcasted_iota(jnp.int32, (tq, ncp), 1) * CMP_STRIDE + (CMP_BLOCK - 1)
    logits = jnp.where((cmp_end <= t)[None], logits, NEG_INF)
    m = jnp.max(logits, axis=-1, keepdims=True)
    e = jnp.exp2(logits - m)
    t_row = i * tq + lax.broadcasted_iota(jnp.int32, (tq, 1), 0)
    seen = jnp.where(t_row >= CMP_BLOCK - 1, 1.0, 0.0)[None]
    inv = seen / jnp.maximum(jnp.sum(e, axis=-1, keepdims=True), 1e-30)
    p = e * inv
    o4 = _dot(p.reshape(group * tq, ncp).astype(BF16), v_ref[0])
    gl = gl_ref[0]
    heads = []
    for g in range(group):
        col = (hk * group + g) * 3
        gate = _sigmoid(_gate_column(gl, col) + gb_ref[col])
        heads.append(o4[g * tq:(g + 1) * tq] * gate)
    o_ref[0] = _compact_group(heads, hk)

    ps = jnp.sum(p, axis=0)
    ps_hi = ps.astype(BF16)
    ps_lo = (ps - ps_hi.astype(F32)).astype(BF16)
    imp = _dot(ps_hi, ov_ref[...]) + _dot(ps_lo, ov_ref[...])
    imp_t = imp.T
    blk = lax.broadcasted_iota(jnp.int32, (LANES, tq), 0)
    cur = (i * tq + lax.broadcasted_iota(jnp.int32, (LANES, tq), 1)) // SLC_BLOCK
    forced = (blk == 0) | (blk == cur) | (blk == cur - 1)
    score = jnp.where(forced, FORCE_SCORE, jnp.where(blk <= cur, imp_t, NEG_INF))
    score = jnp.where(blk < ns, score, BELOW_ALL)
    blk_f = blk.astype(F32)

    def pick(_, carry):
        sc, sel = carry
        mx = jnp.max(sc, axis=0, keepdims=True)
        first = jnp.min(jnp.where(sc == mx, blk_f, float(LANES)), axis=0, keepdims=True)
        hit = blk_f == first
        return jnp.where(hit, BELOW_ALL, sc), jnp.where(hit, 1.0, sel)

    _, sel = lax.fori_loop(0, n_sel, pick, (score, jnp.zeros((LANES, tq), F32)))
    sel_ref[0, 0] = jnp.where(blk <= cur, sel, 0.0).T.astype(sel_ref.dtype)


def _nsa_compressed(q_cmp, k_cmp, v_cmp, gates, gate_b, *, seq, tq=256):
    b = q_cmp.shape[0]
    group = NSA_HEADS // NSA_KV_HEADS
    ncp = k_cmp.shape[1]
    ns = seq // SLC_BLOCK
    n_sel = min(N_SELECT, ns)
    c_start = np.arange(ncp)[:, None] * CMP_STRIDE
    s_start = np.arange(LANES)[None, :] * SLC_BLOCK
    overlap = np.maximum(np.minimum(c_start + CMP_BLOCK, s_start + SLC_BLOCK) - np.maximum(c_start, s_start), 0)
    overlap = np.where((np.arange(LANES)[None, :] < ns) & (np.arange(ncp)[:, None] < ncp - 1), overlap, 0)
    overlap = jnp.asarray(overlap, BF16)
    gw = group * LANES
    tq = min(tq, seq)
    return pl.pallas_call(
        functools.partial(_nsa_cmp_kernel, tq=tq, group=group, n_sel=n_sel, ns=ns),
        grid=(b, NSA_KV_HEADS, seq // tq),
        in_specs=[pl.BlockSpec(memory_space=pltpu.SMEM),
                  pl.BlockSpec((1, tq, gw), lambda bi, h, i: (bi, i, h)),
                  pl.BlockSpec((1, ncp, LANES), lambda bi, h, i: (bi, 0, 0)),
                  pl.BlockSpec((1, ncp, LANES), lambda bi, h, i: (bi, 0, 0)),
                  pl.BlockSpec((ncp, LANES), lambda bi, h, i: (0, 0)),
                  pl.BlockSpec((1, tq, LANES), lambda bi, h, i: (bi, i, 0))],
        out_specs=[pl.BlockSpec((1, tq, group * HEAD_DIM), lambda bi, h, i: (bi, i, h)),
                   pl.BlockSpec((1, 1, tq, LANES), lambda bi, h, i: (bi, h, i, 0))],
        out_shape=[jax.ShapeDtypeStruct((b, seq, NSA_HEADS * HEAD_DIM), F32),
                   jax.ShapeDtypeStruct((b, NSA_KV_HEADS, seq, LANES), BF16)],
        compiler_params=_params(("arbitrary", "arbitrary", "arbitrary")),
        name="nsa_compressed_select",
    )(gate_b, q_cmp, k_cmp, v_cmp, overlap, gates)


def _nsa_slc_kernel(gb_ref, q_ref, k_ref, v_ref, sel_ref, gl_ref, o_ref, sa_ref, sb_ref, *, tq, tk, group):
    hk = pl.program_id(1)
    i = pl.program_id(2)
    q = q_ref[0]
    unsel = (sel_ref[0, 0].astype(F32) - 1.0).astype(BF16)
    lhs = [jnp.concatenate([jnp.concatenate([q[:, g * LANES:(g + 1) * LANES], unsel], axis=1)
                            for g in (2 * c, 2 * c + 1)], axis=0) for c in range(group // 2)]
    n_full = (i * tq) // tk
    chains = group // 2

    def logits_into(buf, j):
        off = pl.multiple_of(j * tk, tk)
        kt = k_ref[0, pl.ds(off, tk), :]
        for c in range(chains):
            buf[c] = _dot_nt(lhs[c], kt)

    def consume(buf, j, states, masked):
        off = pl.multiple_of(j * tk, tk)
        vt = v_ref[0, pl.ds(off, tk), :]
        out = []
        for c in range(chains):
            m, acc = states[c]
            s = buf[c]
            if masked:
                t = i * tq + lax.broadcasted_iota(jnp.int32, (tq, tk), 0)
                key = j * tk + lax.broadcasted_iota(jnp.int32, (tq, tk), 1)
                ok = key <= t
                s = jnp.where(jnp.concatenate([ok, ok], axis=0), s, NEG_INF)
            m_new = jnp.maximum(m, jnp.max(s, axis=-1, keepdims=True))
            p = jnp.exp2(s - m_new)
            out.append((m_new, jnp.exp2(m - m_new) * acc + _dot(p.astype(BF16), vt)))
        return tuple(out)

    def run(j, states, steps):
        bufs = (sa_ref, sb_ref)
        for n in range(steps):
            logits_into(bufs[(n + 1) % 2], j + n + 1)
            states = consume(bufs[n % 2], j + n, states, False)
        return states

    init = tuple((jnp.full((2 * tq, 1), NEG_INF, F32), jnp.zeros((2 * tq, LANES), F32)) for _ in range(chains))
    logits_into(sa_ref, 0)
    states = lax.fori_loop(0, n_full // 4, lambda jj, st: run(4 * jj, st, 4), init)
    states = lax.fori_loop(0, (n_full % 4) // 2, lambda jj, st: run(4 * (n_full // 4), st, 2), states)
    r = 2 * (n_full // 2)

    def diagonal_is_next(states):
        return consume(sa_ref, r, states, True)

    def one_full_tile_left(states):
        logits_into(sb_ref, r + 1)
        return consume(sb_ref, r + 1, consume(sa_ref, r, states, False), True)

    carry = lax.cond(r == n_full, diagonal_is_next, one_full_tile_left, states)
    gl = gl_ref[0]
    heads = []
    for g in range(group):
        acc = carry[g // 2][1][(g % 2) * tq:(g % 2 + 1) * tq]
        colg = (hk * group + g) * 3 + 1
        gate = _sigmoid(_gate_column(gl, colg) + gb_ref[colg])
        heads.append(acc * (gate / acc[:, HEAD_DIM:HEAD_DIM + 1]))
    o_ref[0] = _compact_group(heads, None)


def _nsa_selected(q_rot, k_aug, v_exp, sel, gates, gate_b, *, seq, tq=128, tk=512):
    b = q_rot.shape[0]
    group = NSA_HEADS // NSA_KV_HEADS
    tk = min(tk, seq)
    gw = group * LANES
    return pl.pallas_call(
        functools.partial(_nsa_slc_kernel, tq=tq, tk=tk, group=group),
        grid=(b, NSA_KV_HEADS, seq // tq),
        in_specs=[pl.BlockSpec(memory_space=pltpu.SMEM),
                  pl.BlockSpec((1, tq, gw), lambda bi, h, i: (bi, i, h)),
                  pl.BlockSpec((1, seq, 2 * LANES), lambda bi, h, i: (bi, 0, 0)),
                  pl.BlockSpec((1, seq, LANES), lambda bi, h, i: (bi, 0, h)),
                  pl.BlockSpec((1, 1, tq, LANES), lambda bi, h, i: (bi, h, i, 0)),
                  pl.BlockSpec((1, tq, LANES), lambda bi, h, i: (bi, i, 0))],
        out_specs=pl.BlockSpec((1, tq, group * HEAD_DIM), lambda bi, h, i: (bi, i, h)),
        out_shape=jax.ShapeDtypeStruct((b, seq, NSA_HEADS * HEAD_DIM), F32),
        scratch_shapes=[pltpu.VMEM((group // 2, 2 * tq, tk), F32), pltpu.VMEM((group // 2, 2 * tq, tk), F32)],
        compiler_params=_params(("arbitrary", "arbitrary", "arbitrary")),
        name="nsa_selected",
    )(gate_b, q_rot, k_aug, v_exp, sel, gates)


def _window_kernel(sc_ref, q_ref, k_ref, v_ref, gl_ref, o_ref, sa_ref, sb_ref, *, tq, nt, span, window, group,
                   gated, sinks):
    hk = pl.program_id(1)
    sblk = pl.program_id(2)
    chains = group // 2

    def tile(it):
        gi = sblk * nt + it
        start = pl.multiple_of(jnp.maximum(gi * tq + tq - span, 0), tq)
        return gi, start, pl.ds(pl.multiple_of(it * tq, tq), tq)

    def logits_into(buf, it):
        _, start, rows = tile(it)
        kt = k_ref[0, pl.ds(start, span), :]
        for c in range(chains):
            lhs = jnp.concatenate([q_ref[0, rows, g * LANES:(g + 1) * LANES] for g in (2 * c, 2 * c + 1)], axis=0)
            buf[c] = _dot_nt(lhs, kt)

    def consume(buf, it):
        gi, start, rows = tile(it)
        vt = v_ref[0, pl.ds(start, span), :]
        t = gi * tq + lax.broadcasted_iota(jnp.int32, (tq, span), 0)
        key = start + lax.broadcasted_iota(jnp.int32, (tq, span), 1)
        ok = (key <= t) & (t - key < window)
        ok = jnp.concatenate([ok, ok], axis=0)
        gl = gl_ref[0, rows, :]
        heads = []
        for c in range(chains):
            s = jnp.where(ok, buf[c], NEG_INF)
            m = jnp.max(s, axis=-1, keepdims=True)
            if sinks:
                sk = jnp.concatenate([jnp.full((tq, 1), sc_ref[hk * group + g] * LOG2E, F32)
                                      for g in (2 * c, 2 * c + 1)], axis=0)
                m = jnp.maximum(m, sk)
            acc = _dot(jnp.exp2(s - m).astype(BF16), vt)
            denom = acc[:, HEAD_DIM:HEAD_DIM + 1]
            if sinks:
                denom = denom + jnp.exp2(sk - m)
            for r in range(2):
                g = 2 * c + r
                scale = 1.0 / denom[r * tq:(r + 1) * tq]
                if gated:
                    colg = (hk * group + g) * 3 + 2
                    scale = scale * _sigmoid(_gate_column(gl, colg) + sc_ref[colg])
                heads.append(acc[r * tq:(r + 1) * tq] * scale)
        o_ref[0, rows, :] = _compact_group(heads, None)

    logits_into(sa_ref, 0)

    def quad(jj, carry):
        bufs = (sa_ref, sb_ref)
        for n in range(4):
            it = 4 * jj + n
            logits_into(bufs[(n + 1) % 2], jnp.minimum(it + 1, nt - 1))
            consume(bufs[n % 2], it)
        return carry

    lax.fori_loop(0, nt // 4, quad, 0)


def _window_attention(q_rot, q_blk0, k_pairs, kv_blk, v_heads, scalars, gates, *, seq, window, gated, sinks,
                      tq=128, rows=2048):
    b = q_rot.shape[0]
    group = 4
    span = min(window + tq, seq)
    rows = min(rows, seq)
    nt = rows // tq
    assert nt % 4 == 0
    gw = group * LANES
    qb = q_blk0 // group
    return pl.pallas_call(
        functools.partial(_window_kernel, tq=tq, nt=nt, span=span, window=window, group=group, gated=gated,
                          sinks=sinks),
        grid=(b, 2, seq // rows),
        in_specs=[pl.BlockSpec(memory_space=pltpu.SMEM),
                  pl.BlockSpec((1, rows, gw), lambda bi, h, i: (bi, i, qb + h)),
                  pl.BlockSpec((1, seq, LANES), lambda bi, h, i: (bi, 0, kv_blk)),
                  pl.BlockSpec((1, seq, LANES), lambda bi, h, i: (bi, 0, 2 * kv_blk + h)),
                  pl.BlockSpec((1, rows, LANES), lambda bi, h, i: (bi, i, 0))],
        out_specs=pl.BlockSpec((1, rows, group * HEAD_DIM), lambda bi, h, i: (bi, i, h)),
        out_shape=jax.ShapeDtypeStruct((b, seq, 8 * HEAD_DIM), F32),
        scratch_shapes=[pltpu.VMEM((group // 2, 2 * tq, span), F32), pltpu.VMEM((group // 2, 2 * tq, span), F32)],
        compiler_params=_params(("arbitrary", "arbitrary", "arbitrary")),
        name="window_attention",
    )(scalars, q_rot, k_pairs, v_heads, gates)


def _dense_kernel(q_ref, k_ref, v_ref, o_ref, sa_ref, sb_ref, *, tq, tk, nh):
    i = pl.program_id(2)
    t0 = i * tq
    n_full = t0 // tk
    qs = [q_ref[0][:, e * LANES:(e + 1) * LANES] for e in range(nh)]

    def logits_into(buf, j):
        off = pl.multiple_of(j * tk, tk)
        for e in range(nh):
            buf[e] = _dot_nt(qs[e], k_ref[0, pl.ds(off, tk), e * LANES:(e + 1) * LANES])

    def consume(buf, j, states, masked):
        off = pl.multiple_of(j * tk, tk)
        out = []
        for e in range(nh):
            m, acc = states[e]
            s = buf[e]
            if masked:
                t = t0 + lax.broadcasted_iota(jnp.int32, (tq, tk), 0)
                key = j * tk + lax.broadcasted_iota(jnp.int32, (tq, tk), 1)
                s = jnp.where(key <= t, s, NEG_INF)
            vt = v_ref[0, pl.ds(off, tk), e * LANES:(e + 1) * LANES]
            m_new = jnp.maximum(m, jnp.max(s, axis=-1, keepdims=True))
            p = jnp.exp2(s - m_new)
            out.append((m_new, jnp.exp2(m - m_new) * acc + _dot(p.astype(BF16), vt)))
        return tuple(out)

    def run(j, states, steps):
        bufs = (sa_ref, sb_ref)
        for n in range(steps):
            logits_into(bufs[(n + 1) % 2], j + n + 1)
            states = consume(bufs[n % 2], j + n, states, False)
        return states

    init = tuple((jnp.full((tq, 1), NEG_INF, F32), jnp.zeros((tq, LANES), F32)) for _ in range(nh))
    logits_into(sa_ref, 0)
    states = lax.fori_loop(0, n_full // 4, lambda jj, st: run(4 * jj, st, 4), init)
    states = lax.fori_loop(0, (n_full % 4) // 2, lambda jj, st: run(4 * (n_full // 4), st, 2), states)
    r = 2 * (n_full // 2)

    def diagonal_is_next(states):
        return consume(sa_ref, r, states, True)

    def one_full_tile_left(states):
        logits_into(sb_ref, r + 1)
        return consume(sb_ref, r + 1, consume(sa_ref, r, states, False), True)

    states = lax.cond(r == n_full, diagonal_is_next, one_full_tile_left, states)
    outs = [acc * (1.0 / acc[:, HEAD_DIM:HEAD_DIM + 1]) for _, acc in states]
    lane = _lane((tq, LANES))
    o_ref[0] = jnp.concatenate([jnp.where(lane < HEAD_DIM, outs[e], pltpu.roll(outs[e + 1], HEAD_DIM, 1))
                                for e in range(0, nh, 2)], axis=1)


def _dense_attention(q, k, v, *, seq, tq=256, tk=512, nh=2):
    b = q.shape[0]
    heads = q.shape[2] // LANES
    tk = min(tk, seq)
    tq = min(tq, tk)
    return pl.pallas_call(
        functools.partial(_dense_kernel, tq=tq, tk=tk, nh=nh),
        grid=(b, heads // nh, seq // tq),
        in_specs=[pl.BlockSpec((1, tq, nh * LANES), lambda bi, p, i: (bi, i, p)),
                  pl.BlockSpec((1, seq, nh * LANES), lambda bi, p, i: (bi, 0, p)),
                  pl.BlockSpec((1, seq, nh * LANES), lambda bi, p, i: (bi, 0, p))],
        out_specs=pl.BlockSpec((1, tq, nh * HEAD_DIM), lambda bi, p, i: (bi, i, p)),
        out_shape=jax.ShapeDtypeStruct((b, seq, heads * HEAD_DIM), F32),
        scratch_shapes=[pltpu.VMEM((nh, tq, tk), F32), pltpu.VMEM((nh, tq, tk), F32)],
        compiler_params=_params(("arbitrary", "arbitrary", "arbitrary")),
        name="dense_causal_attention",
    )(q, k, v)


def _decay_kernel(f_ref, b_ref, o_ref):
    x = f_ref[0] + b_ref[...]
    lf = jnp.minimum(x, 0.0) - jnp.log1p(jnp.exp(-jnp.abs(x)))
    n = lf.shape[-1]
    lane = _lane(lf.shape)
    d = 1
    while d < n:
        lf = lf + jnp.where(lane >= d, pltpu.roll(lf, d, 1), 0.0)
        d *= 2
    o_ref[0] = lf * LOG2E


def _decay_cumsum(f_t, bias):
    b, h, s = f_t.shape
    return pl.pallas_call(
        _decay_kernel,
        grid=(b,),
        in_specs=[pl.BlockSpec((1, h, s), lambda i: (i, 0, 0)), pl.BlockSpec((h, 1), lambda i: (0, 0))],
        out_specs=pl.BlockSpec((1, h, s), lambda i: (i, 0, 0)),
        out_shape=jax.ShapeDtypeStruct((b, h, s), F32),
        compiler_params=_params(("arbitrary",)),
        name="fox_decay_cumsum",
    )(f_t, bias.reshape(h, 1).astype(F32))


def _mla_prep_kernel(cq_ref, ckv_ref, misc_ref, gqa_ref, gkva_ref, wq_ref, wk_ref, wv_ref, gq_ref, gk_ref,
                     cos_ref, sin_ref, ones_ref, q_ref, k_ref, v_ref, *, scale):
    tm = cq_ref.shape[0]
    lane = _lane((tm, LANES))
    in_rope = (lane >= MLA_NOPE_DIM) & (lane < MLA_QK_DIM)
    first = lane < MLA_NOPE_DIM + MLA_ROPE_DIM // 2
    cos, sin = cos_ref[...], sin_ref[...]

    def rope_tail(x):
        sw = jnp.where(first, pltpu.roll(x, LANES - MLA_ROPE_DIM // 2, 1), pltpu.roll(x, MLA_ROPE_DIM // 2, 1))
        return x * cos + jnp.where(in_rope, sw, 0.0) * sin

    cq = _rms(cq_ref[...], gqa_ref[...], MLA_Q_RANK).astype(BF16)
    ckv = _rms(ckv_ref[...], gkva_ref[...], MLA_KV_RANK).astype(BF16)
    qa = _dot(cq, wq_ref[...])
    ka = _dot(ckv, wk_ref[...])
    k_rope = jnp.where(in_rope, misc_ref[...], 0.0)
    def head_rms(x, gain):
        ms = _split_dot(x * x, ones_ref[...]) * (1.0 / MLA_QK_DIM)
        return (x * lax.rsqrt(ms + RMS_EPS)) * gain

    for h in range(MLA_HEADS):
        qh = head_rms(qa[:, h * LANES:(h + 1) * LANES], gq_ref[...])
        q_ref[:, h * LANES:(h + 1) * LANES] = (rope_tail(qh) * scale).astype(q_ref.dtype)
        kh = head_rms(ka[:, h * LANES:(h + 1) * LANES] + k_rope, gk_ref[...])
        k_ref[:, h * LANES:(h + 1) * LANES] = rope_tail(kh).astype(k_ref.dtype)
    v = _dot(ckv, wv_ref[...])
    ones_col = (_lane(v.shape) & (LANES - 1)) == MLA_V_DIM
    v_ref[...] = jnp.where(ones_col, 1.0, v).astype(v_ref.dtype)


def _mla_prep(y, cq_blk, ckv_blk, misc_blk, q_a_norm, w_q_b, kv_a_norm, w_kv_b, q_norm, k_norm, *, seq, tm=512):
    n = y.shape[0]
    h = MLA_HEADS
    pad = LANES - MLA_QK_DIM
    wq = jnp.pad(w_q_b.reshape(MLA_Q_RANK, h, MLA_QK_DIM), ((0, 0), (0, 0), (0, pad)))
    wq = wq.reshape(MLA_Q_RANK, h * LANES).astype(BF16)
    wkv = w_kv_b.reshape(MLA_KV_RANK, h, MLA_NOPE_DIM + MLA_V_DIM)
    wk = jnp.pad(wkv[:, :, :MLA_NOPE_DIM], ((0, 0), (0, 0), (0, LANES - MLA_NOPE_DIM)))
    wk = wk.reshape(MLA_KV_RANK, h * LANES).astype(BF16)
    wv = jnp.pad(wkv[:, :, MLA_NOPE_DIM:], ((0, 0), (0, 0), (0, LANES - MLA_V_DIM)))
    wv = wv.reshape(MLA_KV_RANK, h * LANES).astype(BF16)
    gq = jnp.pad(q_norm, (0, pad)).reshape(1, LANES)
    gk = jnp.pad(k_norm, (0, pad)).reshape(1, LANES)
    half = MLA_ROPE_DIM // 2
    inv_freq = 1.0 / (ROPE_THETA ** (jnp.arange(0, MLA_ROPE_DIM, 2, dtype=F32) / MLA_ROPE_DIM))
    ang = jnp.arange(seq, dtype=F32)[:, None] * inv_freq[None, :]
    cos, sin = jnp.cos(ang), jnp.sin(ang)
    ones = jnp.ones((seq, MLA_NOPE_DIM), F32)
    zeros = jnp.zeros((seq, MLA_NOPE_DIM), F32)
    cos_t = jnp.concatenate([ones, cos, cos, ones[:, :pad]], axis=1)
    sin_t = jnp.concatenate([zeros, -sin, sin, zeros[:, :pad]], axis=1)
    del half
    sblocks = seq // tm
    full = lambda shp: pl.BlockSpec(shp, lambda i: (0,) * len(shp))
    q, k, v = pl.pallas_call(
        functools.partial(_mla_prep_kernel, scale=MLA_QK_DIM ** -0.5 * LOG2E),
        grid=(n // tm,),
        in_specs=[pl.BlockSpec((tm, MLA_Q_RANK), lambda i: (i, cq_blk // 2)),
                  pl.BlockSpec((tm, LANES), lambda i: (i, ckv_blk)),
                  pl.BlockSpec((tm, LANES), lambda i: (i, misc_blk)),
                  full((1, MLA_Q_RANK)), full((1, MLA_KV_RANK)),
                  full((MLA_Q_RANK, h * LANES)), full((MLA_KV_RANK, h * LANES)), full((MLA_KV_RANK, h * LANES)),
                  full((1, LANES)), full((1, LANES)),
                  pl.BlockSpec((tm, LANES), lambda i: (i % sblocks, 0)),
                  pl.BlockSpec((tm, LANES), lambda i: (i % sblocks, 0)),
                  full((LANES, LANES))],
        out_specs=[pl.BlockSpec((tm, h * LANES), lambda i: (i, 0)),
                   pl.BlockSpec((tm, h * LANES), lambda i: (i, 0)),
                   pl.BlockSpec((tm, h * LANES), lambda i: (i, 0))],
        out_shape=[jax.ShapeDtypeStruct((n, h * LANES), BF16),
                   jax.ShapeDtypeStruct((n, h * LANES), BF16),
                   jax.ShapeDtypeStruct((n, h * LANES), BF16)],
        compiler_params=_params(("arbitrary",)),
        name="mla_prep",
    )(y, y, y, q_a_norm.reshape(1, -1), kv_a_norm.reshape(1, -1), wq, wk, wv, gq, gk, cos_t, sin_t,
      _group_sum_matrix(LANES))
    return q, k, v


def _cols(w, a, b):
    return w[:, a:b]


def _nsa_swa_mixer(x, batch, seq, mix_norm, w_in, nsa_gate_b, nsa_q_norm, nsa_kc_norm, nsa_ks_norm, nsa_kw_norm,
                   cmp_pos_k, cmp_pos_v, cmpk_w1, cmpk_w2, cmpv_w1, cmpv_w2,
                   swa_q_norm, swa_k_norm, swa_sinks, w_out):
    n = batch * seq
    o = np.cumsum([0, 512, 128, 128, 128, 128, 128, 128, 24, 512, 128, 128])
    seg = lambda j: _cols(w_in, o[j], o[j + 1])
    q_a, kc, vc, ks, vs, kw, vw, gl, q_b, k_b, v_b = [seg(j) for j in range(11)]
    gl = jnp.pad(gl, ((0, 0), (0, LANES - gl.shape[1])))
    w = jnp.concatenate([q_a, q_b, ks, kw, k_b, kc, vc, vs, vw, v_b, gl], axis=1).astype(BF16)
    y = _rms_matmul(x, mix_norm, w)
    cos_t, sin_t = _rope_tables(seq)

    s_q = HEAD_DIM ** -0.5 * LOG2E
    gains = jnp.stack([_pair_gain(g) for g in (nsa_q_norm, swa_q_norm, nsa_ks_norm, nsa_kw_norm, swa_k_norm)])
    jobs = [_Job(blk=c, out=0, col=2 * c, gain=c // 4, rope=True, scale=s_q, mode="q", dst=((c % 4) // 2,) * 2)
            for c in range(8)]
    jobs += [_Job(blk=c, out=1, col=2 * c, gain=0, scale=s_q, mode="q", dst=(c // 2,) * 2)
             for c in range(4)]
    jobs += [_Job(blk=8, out=2, col=0, gain=2, rope=True, mode="kaug"),
             _Job(blk=9, out=3, col=0, gain=3, rope=True), _Job(blk=10, out=3, col=1, gain=4, rope=True),
             _Job(blk=13, out=4, col=0, mode="v"),
             _Job(blk=14, out=5, col=0, mode="v"), _Job(blk=15, out=5, col=2, mode="v"),
             _Job(blk=11, out=6, col=0), _Job(blk=12, out=7, col=0), _Job(blk=16, out=8, col=0)]
    outs = [(16, BF16), (8, BF16), (2, BF16), (2, BF16), (2, BF16), (4, BF16), (1, F32), (1, F32), (1, F32)]
    q_rot, q_cmp, k_aug, k_ws, v_slc, v_ws, kc_raw, vc_raw, gates = (
        a.reshape(batch, seq, -1) for a in _prep(y, jobs, outs, gains, seq, cos_t, sin_t))

    k_cmp = _compress(kc_raw, cmp_pos_k, cmpk_w1, cmpk_w2, nsa_kc_norm)
    v_cmp = _compress(vc_raw, cmp_pos_v, cmpv_w1, cmpv_w2, None)
    gate_b = nsa_gate_b.astype(F32)

    o_cmp, sel = _nsa_compressed(q_cmp, k_cmp, v_cmp, gates, gate_b, seq=seq)
    o_slc = _nsa_selected(q_rot, k_aug, v_slc, sel, gates, gate_b, seq=seq)
    o_win = _window_attention(q_rot, 0, k_ws, 0, v_ws, gate_b, gates, seq=seq, window=NSA_WINDOW,
                              gated=True, sinks=False)
    o_swa = _window_attention(q_rot, 8, k_ws, 1, v_ws, swa_sinks.astype(F32), gates, seq=seq, window=SWA_WINDOW,
                              gated=False, sinks=True)
    flat = lambda a: a.reshape(n, -1)
    return _outproj(x, [flat(o_cmp), flat(o_slc), flat(o_win)], [flat(o_swa)], w_out)


def _fox_mla_mixer(x, batch, seq, mix_norm, w_in, fox_f_bias, fox_q_norm, fox_k_norm, mla_q_a_norm, mla_w_q_b,
                   mla_kv_a_norm, mla_w_kv_b, mla_q_norm, mla_k_norm, w_out):
    n = batch * seq
    o = np.cumsum([0, 512, 512, 512, 8, 256, 128, 32])
    seg = lambda j: _cols(w_in, o[j], o[j + 1])
    q_c, k_c, v_c, f_c, c_q, c_kv, k_r = [seg(j) for j in range(7)]
    d = w_in.shape[0]
    misc = jnp.concatenate([f_c, jnp.zeros((d, MLA_NOPE_DIM - 8), w_in.dtype), k_r,
                            jnp.zeros((d, LANES - MLA_QK_DIM), w_in.dtype)], axis=1)
    w = jnp.concatenate([q_c, k_c, v_c, c_q, c_kv, misc], axis=1).astype(BF16)
    y = _rms_matmul(x, mix_norm, w)
    cos_t, sin_t = _rope_tables(seq)

    y3 = y.reshape(batch, seq, y.shape[1])
    f_t = y3[:, :, 15 * LANES:15 * LANES + FOX_HEADS].transpose(0, 2, 1)
    dc = _decay_cumsum(f_t, fox_f_bias)
    dc_tok = jnp.pad(dc.transpose(0, 2, 1).reshape(n, FOX_HEADS), ((0, 0), (0, LANES - FOX_HEADS)))
    gains = jnp.stack([_pair_gain(fox_q_norm), _pair_gain(fox_k_norm)])
    jobs = [_Job(blk=c, out=0, col=2 * c, gain=0, scale=HEAD_DIM ** -0.5 * LOG2E, mode="q", aug="ones")
            for c in range(4)]
    jobs += [_Job(blk=8 + c, out=1, col=2 * c, mode="v") for c in range(4)]
    q_f, v_f = _prep(y, jobs, [(8, BF16), (8, BF16)], gains, seq, cos_t, sin_t)
    kjobs = [_Job(blk=4 + c, out=0, col=2 * c, gain=1, mode="q", aug="decay", heads=(2 * c, 2 * c + 1))
             for c in range(4)]
    (k_f,) = _prep(y, kjobs, [(8, BF16)], gains, seq, cos_t, sin_t, aux=dc_tok)
    b3 = lambda a: a.reshape(batch, seq, -1)
    o_fox = _dense_attention(b3(q_f), b3(k_f), b3(v_f), seq=seq)

    q_m, k_m, v_m = _mla_prep(y, 12, 14, 15, mla_q_a_norm, mla_w_q_b, mla_kv_a_norm, mla_w_kv_b,
                              mla_q_norm, mla_k_norm, seq=seq)
    o_mla = _dense_attention(b3(q_m), b3(k_m), b3(v_m), seq=seq)
    flat = lambda a: a.reshape(n, -1)
    return _outproj(x, [flat(o_fox)], [flat(o_mla)], w_out)


def kernel(x, l0_ffn1_norm, l0_ffn1_w_gate, l0_ffn1_w_up, l0_ffn1_w_down, l0_mix_norm, l0_w_in, l0_nsa_gate_b, l0_nsa_q_norm, l0_nsa_kc_norm, l0_nsa_ks_norm, l0_nsa_kw_norm, l0_cmp_pos_k, l0_cmp_pos_v, l0_cmpk_w1, l0_cmpk_w2, l0_cmpv_w1, l0_cmpv_w2, l0_swa_q_norm, l0_swa_k_norm, l0_swa_sinks, l0_w_out, l0_ffn2_norm, l0_ffn2_w_gate, l0_ffn2_w_up, l0_ffn2_w_down, l1_ffn1_norm, l1_ffn1_w_gate, l1_ffn1_w_up, l1_ffn1_w_down, l1_mix_norm, l1_w_in, l1_fox_f_bias, l1_fox_q_norm, l1_fox_k_norm, l1_mla_q_a_norm, l1_mla_w_q_b, l1_mla_kv_a_norm, l1_mla_w_kv_b, l1_mla_q_norm, l1_mla_k_norm, l1_w_out, l1_ffn2_norm, l1_ffn2_w_gate, l1_ffn2_w_up, l1_ffn2_w_down):
    batch, seq, d = x.shape
    h = x.reshape(batch * seq, d)
    h = _ffn(h, l0_ffn1_norm, l0_ffn1_w_gate, l0_ffn1_w_up, l0_ffn1_w_down)
    h = _nsa_swa_mixer(h, batch, seq, l0_mix_norm, l0_w_in, l0_nsa_gate_b, l0_nsa_q_norm, l0_nsa_kc_norm,
                       l0_nsa_ks_norm, l0_nsa_kw_norm, l0_cmp_pos_k, l0_cmp_pos_v, l0_cmpk_w1, l0_cmpk_w2,
                       l0_cmpv_w1, l0_cmpv_w2, l0_swa_q_norm, l0_swa_k_norm, l0_swa_sinks, l0_w_out)
    h = _ffn(h, l0_ffn2_norm, l0_ffn2_w_gate, l0_ffn2_w_up, l0_ffn2_w_down)
    h = _ffn(h, l1_ffn1_norm, l1_ffn1_w_gate, l1_ffn1_w_up, l1_ffn1_w_down)
    h = _fox_mla_mixer(h, batch, seq, l1_mix_norm, l1_w_in, l1_fox_f_bias, l1_fox_q_norm, l1_fox_k_norm,
                       l1_mla_q_a_norm, l1_mla_w_q_b, l1_mla_kv_a_norm, l1_mla_w_kv_b, l1_mla_q_norm,
                       l1_mla_k_norm, l1_w_out)
    h = _ffn(h, l1_ffn2_norm, l1_ffn2_w_gate, l1_ffn2_w_up, l1_ffn2_w_down)
    return h.reshape(batch, seq, d)
```

```python
import functools
from typing import NamedTuple, Optional

import numpy as np
import jax
import jax.numpy as jnp
from jax import lax
from jax.experimental import pallas as pl
from jax.experimental.pallas import tpu as pltpu

F32 = jnp.float32
BF16 = jnp.bfloat16

HEAD_DIM = 64
LANES = 128
ROPE_THETA = 10000.0
RMS_EPS = 1e-6
NEG_INF = -1e30
FORCE_SCORE = 1e9
BELOW_ALL = -3e38
LOG2E = 1.4426950408889634
BIG = 1e30
VT_ROWS = 80

NSA_HEADS = 8
NSA_KV_HEADS = 2
CMP_BLOCK = 32
CMP_STRIDE = 16
CMP_HIDDEN = 256
SLC_BLOCK = 64
N_SELECT = 16
NSA_WINDOW = 512
SWA_HEADS = 8
SWA_KV_HEADS = 2
SWA_WINDOW = 128
FOX_HEADS = 8
MLA_HEADS = 8
MLA_Q_RANK = 256
MLA_KV_RANK = 128
MLA_NOPE_DIM = 64
MLA_ROPE_DIM = 32
MLA_V_DIM = 64
MLA_QK_DIM = MLA_NOPE_DIM + MLA_ROPE_DIM

VMEM_LIMIT = 48 * 1024 * 1024

NT_DIMS = (((1,), (1,)), ((), ()))


def _params(sem):
    return pltpu.CompilerParams(dimension_semantics=sem, vmem_limit_bytes=VMEM_LIMIT)


def _dot(a, b):
    return jnp.dot(a, b, preferred_element_type=F32)


def _dot_nt(a, b):
    return lax.dot_general(a, b, NT_DIMS, preferred_element_type=F32)


def _rms(x, gain, n):
    ms = jnp.sum(x * x, axis=-1, keepdims=True) * (1.0 / n)
    return (x * lax.rsqrt(ms + RMS_EPS)) * gain


def _lane(shape):
    return lax.broadcasted_iota(jnp.int32, shape, len(shape) - 1)


def _split_dot(x, m):
    hi = x.astype(BF16)
    lo = (x - hi.astype(F32)).astype(BF16)
    return _dot(hi, m) + _dot(lo, m)


def _group_sum_matrix(width):
    lane = np.arange(LANES)
    return jnp.asarray(lane[:, None] // width == lane[None, :] // width, BF16)


def _ffn_kernel(x_ref, g_ref, wg_ref, wu_ref, wd_ref, o_ref, h_sc, acc_sc, *, tf):
    x = x_ref[...]
    h_sc[...] = _rms(x, g_ref[...], x.shape[-1]).astype(BF16)
    acc_sc[...] = jnp.zeros_like(acc_sc)
    f = wg_ref.shape[1]
    for c0 in range(0, f, tf):
        cols = slice(c0, min(c0 + tf, f))
        h = h_sc[...]
        g = _dot(h, wg_ref[:, cols])
        u = _dot(h, wu_ref[:, cols])
        a = (g * (1.0 / (1.0 + jnp.exp(-g)))) * u
        acc_sc[...] += _dot(a.astype(BF16), wd_ref[cols, :])
    o_ref[...] = x + 0.5 * acc_sc[...]


def _ffn(x, norm, w_gate, w_up, w_down, *, tm=512, tf=256):
    n, d = x.shape
    f = w_gate.shape[1]
    wg, wu, wd = w_gate.astype(BF16), w_up.astype(BF16), w_down.astype(BF16)
    wspec = lambda shp: pl.BlockSpec(shp, lambda i: (0, 0), pipeline_mode=pl.Buffered(1))
    return pl.pallas_call(
        functools.partial(_ffn_kernel, tf=tf),
        grid=(n // tm,),
        in_specs=[pl.BlockSpec((tm, d), lambda i: (i, 0)),
                  pl.BlockSpec((1, d), lambda i: (0, 0)),
                  wspec((d, f)), wspec((d, f)), wspec((f, d))],
        out_specs=pl.BlockSpec((tm, d), lambda i: (i, 0)),
        out_shape=jax.ShapeDtypeStruct((n, d), F32),
        scratch_shapes=[pltpu.VMEM((tm, d), BF16), pltpu.VMEM((tm, d), F32)],
        compiler_params=_params(("arbitrary",)),
        name="ffn",
    )(x, norm.reshape(1, d), wg, wu, wd)


def _outproj_kernel(*refs, n_a, n_b):
    x_ref = refs[0]
    a_refs = refs[1:1 + n_a]
    b_refs = refs[1 + n_a:1 + n_a + n_b]
    wa_ref, wb_ref, o_ref = refs[1 + n_a + n_b:]
    a = a_refs[0][...]
    for r in a_refs[1:]:
        a = a + r[...]
    b = b_refs[0][...]
    for r in b_refs[1:]:
        b = b + r[...]
    o_ref[...] = x_ref[...] + _dot(a.astype(BF16), wa_ref[...]) + _dot(b.astype(BF16), wb_ref[...])


def _outproj(x, a_list, b_list, w_out, *, tm=512):
    n, d = x.shape
    ca = a_list[0].shape[1]
    cb = b_list[0].shape[1]
    wa = w_out[:ca].astype(BF16)
    wb = w_out[ca:].astype(BF16)
    row = lambda c: pl.BlockSpec((tm, c), lambda i: (i, 0))
    return pl.pallas_call(
        functools.partial(_outproj_kernel, n_a=len(a_list), n_b=len(b_list)),
        grid=(n // tm,),
        in_specs=[row(d)] + [row(ca)] * len(a_list) + [row(cb)] * len(b_list)
                 + [pl.BlockSpec((ca, d), lambda i: (0, 0)), pl.BlockSpec((cb, d), lambda i: (0, 0))],
        out_specs=row(d),
        out_shape=jax.ShapeDtypeStruct((n, d), F32),
        compiler_params=_params(("arbitrary",)),
        name="outproj",
    )(x, *a_list, *b_list, wa, wb)


class _Job(NamedTuple):
    blk: int
    out: int
    col: int
    gain: Optional[int] = None
    rope: bool = False
    scale: float = 1.0
    mode: str = "plain"
    dst: tuple = (0, 0)
    aug: Optional[str] = None
    heads: tuple = (0, 0)


def _prep_kernel(y_ref, gain_ref, cos_ref, sin_ref, aux_ref, hsum_ref, *o_refs, jobs, seq):
    tm = y_ref.shape[0]
    lane = _lane((tm, LANES))
    low = lane < HEAD_DIM
    zero = jnp.zeros((tm, LANES), F32)
    for job in jobs:
        x = y_ref[:, job.blk * LANES:(job.blk + 1) * LANES]
        if job.gain is not None:
            ms = _split_dot(x * x, hsum_ref[...]) * (1.0 / HEAD_DIM)
            x = (x * lax.rsqrt(ms + RMS_EPS)) * gain_ref[job.gain]
        if job.rope:
            swapped = jnp.where((lane & (HEAD_DIM - 1)) < HEAD_DIM // 2,
                                pltpu.roll(x, LANES - HEAD_DIM // 2, 1), pltpu.roll(x, HEAD_DIM // 2, 1))
            x = x * cos_ref[...] + swapped * sin_ref[...]
        if job.scale != 1.0:
            x = x * job.scale
        if job.mode == "plain":
            pieces = [x]
        elif job.mode == "kaug":
            pos = (pl.program_id(0) * tm + lax.broadcasted_iota(jnp.int32, (tm, LANES), 0)) % seq
            pieces = [x, jnp.where(lane == pos // SLC_BLOCK, BIG, 0.0)]
        elif job.mode == "v":
            r = pltpu.roll(x, HEAD_DIM, 1)
            tail = jnp.where(lane == HEAD_DIM, 1.0, 0.0)
            pieces = [jnp.where(low, x, tail), jnp.where(low, r, tail)]
        else:
            r = pltpu.roll(x, HEAD_DIM, 1)
            h_even = jnp.where(low, x, zero) if job.dst[0] == 0 else jnp.where(low, zero, r)
            h_odd = jnp.where(low, r, zero) if job.dst[1] == 0 else jnp.where(low, zero, x)
            if job.aug is not None:
                tails = []
                for e in range(2):
                    if job.aug == "ones":
                        tails.append(jnp.where((lane >= HEAD_DIM) & (lane < HEAD_DIM + 3), 1.0, 0.0))
                    else:
                        d = jnp.sum(jnp.where(lane == job.heads[e], aux_ref[...], 0.0), axis=-1, keepdims=True)
                        hi = d.astype(BF16).astype(F32)
                        mid = (d - hi).astype(BF16).astype(F32)
                        lo = d - hi - mid
                        tails.append(jnp.where(lane == HEAD_DIM, -hi, jnp.where(lane == HEAD_DIM + 1, -mid,
                                     jnp.where(lane == HEAD_DIM + 2, -lo, 0.0))))
                h_even = jnp.where(low, h_even, tails[0])
                h_odd = jnp.where(low, h_odd, tails[1])
            pieces = [h_even, h_odd]
        o_ref = o_refs[job.out]
        for n, piece in enumerate(pieces):
            o_ref[:, (job.col + n) * LANES:(job.col + n + 1) * LANES] = piece.astype(o_ref.dtype)


def _prep(y, jobs, outs, gains, seq, cos_t, sin_t, *, aux=None, tm=512):
    n, c = y.shape
    aux_spec = pl.BlockSpec((tm, LANES), lambda i: (i, 0))
    if aux is None:
        aux, aux_spec = jnp.zeros((tm, LANES), F32), pl.BlockSpec((tm, LANES), lambda i: (0, 0))
    sblocks = seq // tm
    return pl.pallas_call(
        functools.partial(_prep_kernel, jobs=tuple(jobs), seq=seq),
        grid=(n // tm,),
        in_specs=[pl.BlockSpec((tm, c), lambda i: (i, 0)),
                  pl.BlockSpec(gains.shape, lambda i: (0, 0, 0)),
                  pl.BlockSpec((tm, LANES), lambda i: (i % sblocks, 0)),
                  pl.BlockSpec((tm, LANES), lambda i: (i % sblocks, 0)),
                  aux_spec,
                  pl.BlockSpec((LANES, LANES), lambda i: (0, 0))],
        out_specs=[pl.BlockSpec((tm, w * LANES), lambda i: (i, 0)) for w, _ in outs],
        out_shape=[jax.ShapeDtypeStruct((n, w * LANES), dt) for w, dt in outs],
        compiler_params=_params(("arbitrary",)),
        name="head_prep",
    )(y, gains, cos_t, sin_t, aux, _group_sum_matrix(HEAD_DIM))


def _proj_prep_kernel(x_ref, g_ref, w_ref, gain_ref, cos_ref, sin_ref, aux_ref, hsum_ref, *rest, jobs, seq):
    *o_refs, y_sc = rest
    x = x_ref[...]
    y_sc[...] = _dot(_rms(x, g_ref[...], x.shape[-1]).astype(BF16), w_ref[...])
    _prep_kernel(y_sc, gain_ref, cos_ref, sin_ref, aux_ref, hsum_ref, *o_refs, jobs=jobs, seq=seq)


def _proj_prep(x, norm, w, jobs, outs, gains, seq, cos_t, sin_t, *, tm=512):
    n, d = x.shape
    c = w.shape[1]
    sblocks = seq // tm
    return pl.pallas_call(
        functools.partial(_proj_prep_kernel, jobs=tuple(jobs), seq=seq),
        grid=(n // tm,),
        in_specs=[pl.BlockSpec((tm, d), lambda i: (i, 0)),
                  pl.BlockSpec((1, d), lambda i: (0, 0)),
                  pl.BlockSpec((d, c), lambda i: (0, 0), pipeline_mode=pl.Buffered(1)),
                  pl.BlockSpec(gains.shape, lambda i: (0, 0, 0)),
                  pl.BlockSpec((tm, LANES), lambda i: (i % sblocks, 0)),
                  pl.BlockSpec((tm, LANES), lambda i: (i % sblocks, 0)),
                  pl.BlockSpec((tm, LANES), lambda i: (0, 0)),
                  pl.BlockSpec((LANES, LANES), lambda i: (0, 0))],
        out_specs=[pl.BlockSpec((tm, wd * LANES), lambda i: (i, 0)) for wd, _ in outs],
        out_shape=[jax.ShapeDtypeStruct((n, wd * LANES), dt) for wd, dt in outs],
        scratch_shapes=[pltpu.VMEM((tm, c), F32)],
        compiler_params=_params(("arbitrary",)),
        name="proj_prep",
    )(x, norm.reshape(1, d), w, gains, cos_t, sin_t, jnp.zeros((tm, LANES), F32), _group_sum_matrix(HEAD_DIM))


def _pair_gain(g):
    return jnp.concatenate([g, g]).reshape(1, LANES).astype(F32)


def _rope_tables(seq):
    half = HEAD_DIM // 2
    inv_freq = 1.0 / (ROPE_THETA ** (jnp.arange(0, HEAD_DIM, 2, dtype=F32) / HEAD_DIM))
    ang = jnp.arange(seq, dtype=F32)[:, None] * inv_freq[None, :]
    cos, sin = jnp.cos(ang), jnp.sin(ang)
    cos_t = jnp.concatenate([cos, cos, cos, cos], axis=1)
    sin_t = jnp.concatenate([-sin, sin, -sin, sin], axis=1)
    del half
    return cos_t, sin_t


def _compress_kernel(ch_ref, ptop_ref, pbot_ref, w1t_ref, w1b_ref, w2_ref, gain_ref, o_ref, *, norm):
    ch = ch_ref[0]
    a = _dot((ch + ptop_ref[...]).astype(BF16), w1t_ref[...])
    b = _dot((ch + pbot_ref[...]).astype(BF16), w1b_ref[...])
    nc = a.shape[0]
    hid = a + pltpu.roll(b, nc - 1, 0)
    act = hid * (1.0 / (1.0 + jnp.exp(-hid)))
    out = _dot(act.astype(BF16), w2_ref[...])
    if norm:
        lane = _lane(out.shape)
        low = lane < HEAD_DIM
        o2 = out * out
        s_lo = jnp.sum(jnp.where(low, o2, 0.0), axis=-1, keepdims=True)
        s_hi = jnp.sum(jnp.where(low, 0.0, o2), axis=-1, keepdims=True)
        ms = jnp.where(low, s_lo, s_hi) * (1.0 / HEAD_DIM)
        out = (out * lax.rsqrt(ms + RMS_EPS)) * gain_ref[...]
    o_ref[0] = out.astype(o_ref.dtype)


def _compress(t_pair, pos_emb, w1, w2, gain):
    b, s, _ = t_pair.shape
    nc = s // CMP_STRIDE
    hid = w1.shape[1]
    ch = t_pair.reshape(b, nc, CMP_STRIDE * LANES)
    eye2 = jnp.eye(2, dtype=F32)
    w1r = w1.reshape(CMP_BLOCK, HEAD_DIM, hid)
    def expand_w1(w):
        return jnp.einsum('pdj,kl->pkdlj', w, eye2).reshape(CMP_STRIDE * LANES, 2 * hid).astype(BF16)
    w1t, w1b = expand_w1(w1r[:CMP_STRIDE]), expand_w1(w1r[CMP_STRIDE:])
    w2e = jnp.einsum('jd,kl->kjld', w2, eye2).reshape(2 * hid, LANES).astype(BF16)
    def expand_pos(p):
        return jnp.broadcast_to(p[:, None, :], (CMP_STRIDE, 2, HEAD_DIM)).reshape(1, CMP_STRIDE * LANES)
    ptop, pbot = expand_pos(pos_emb[:CMP_STRIDE]), expand_pos(pos_emb[CMP_STRIDE:])
    norm = gain is not None
    g = _pair_gain(gain) if norm else jnp.ones((1, LANES), F32)
    full = lambda shp: pl.BlockSpec(shp, lambda i: (0,) * len(shp))
    return pl.pallas_call(
        functools.partial(_compress_kernel, norm=norm),
        grid=(b,),
        in_specs=[pl.BlockSpec((1, nc, CMP_STRIDE * LANES), lambda i: (i, 0, 0)),
                  full((1, CMP_STRIDE * LANES)), full((1, CMP_STRIDE * LANES)),
                  full((CMP_STRIDE * LANES, 2 * hid)), full((CMP_STRIDE * LANES, 2 * hid)),
                  full((2 * hid, LANES)), full((1, LANES))],
        out_specs=pl.BlockSpec((1, nc, LANES), lambda i: (i, 0, 0)),
        out_shape=jax.ShapeDtypeStruct((b, nc, LANES), BF16),
        compiler_params=_params(("arbitrary",)),
        name="nsa_compress",
    )(ch, ptop, pbot, w1t, w1b, w2e, g)


def _stack_heads(q, n):
    return jnp.concatenate([q[:, g * LANES:(g + 1) * LANES] for g in range(n)], axis=0)


def _gate_column(gl, col):
    lane = _lane(gl.shape)
    return jnp.sum(jnp.where(lane == col, gl, 0.0), axis=-1, keepdims=True)


def _sigmoid(x):
    return 1.0 / (1.0 + jnp.exp(-x))


def _compact_group(heads, hk):
    tq = heads[0].shape[0]
    lane = _lane((tq, LANES))
    low = lane < HEAD_DIM
    outs = []
    for e in range(0, len(heads), 2):
        he, ho = heads[e], heads[e + 1]
        ho_r = pltpu.roll(ho, HEAD_DIM, 1)
        if hk is None:
            lo_part, hi_part = he, ho_r
        else:
            at_low = jnp.broadcast_to(hk, (tq, LANES)) == 0
            lo_part = jnp.where(at_low, he, pltpu.roll(he, HEAD_DIM, 1))
            hi_part = jnp.where(at_low, ho_r, ho)
        outs.append(jnp.where(low, lo_part, hi_part))
    return jnp.concatenate(outs, axis=1)


def _nsa_cmp_kernel(gb_ref, q_ref, k_ref, v_ref, ov_ref, gl_ref, o_ref, sel_ref, *, tq, group, n_sel, ns, q_off):
    hk = pl.program_id(1)
    i = pl.program_id(2) + q_off
    q4 = _stack_heads(q_ref[0], group)
    kc = k_ref[0]
    ncp = kc.shape[0]
    logits = _dot_nt(q4, kc).reshape(group, tq, ncp)
    t = i * tq + lax.broadcasted_iota(jnp.int32, (tq, ncp), 0)
    cmp_end = lax.broadcasted_iota(jnp.int32, (tq, ncp), 1) * CMP_STRIDE + (CMP_BLOCK - 1)
    logits = jnp.where((cmp_end <= t)[None], logits, NEG_INF)
    m = jnp.max(logits, axis=-1, keepdims=True)
    e = jnp.exp2(logits - m)
    t_row = i * tq + lax.broadcasted_iota(jnp.int32, (tq, 1), 0)
    seen = jnp.where(t_row >= CMP_BLOCK - 1, 1.0, 0.0)[None]
    inv = seen / jnp.maximum(jnp.sum(e, axis=-1, keepdims=True), 1e-30)
    p = e * inv
    o4 = _dot(p.reshape(group * tq, ncp).astype(BF16), v_ref[0])
    gl = gl_ref[0]
    heads = []
    for g in range(group):
        col = (hk * group + g) * 3
        gate = _sigmoid(_gate_column(gl, col) + gb_ref[col])
        heads.append(o4[g * tq:(g + 1) * tq] * gate)
    o_ref[0] = _compact_group(heads, hk)

    ps = jnp.sum(p, axis=0)
    ps_hi = ps.astype(BF16)
    ps_lo = (ps - ps_hi.astype(F32)).astype(BF16)
    imp = _dot(ps_hi, ov_ref[...]) + _dot(ps_lo, ov_ref[...])
    imp_t = imp.T
    blk = lax.broadcasted_iota(jnp.int32, (LANES, tq), 0)
    cur = (i * tq + lax.broadcasted_iota(jnp.int32, (LANES, tq), 1)) // SLC_BLOCK
    forced = (blk == 0) | (blk == cur) | (blk == cur - 1)
    score = jnp.where(forced, FORCE_SCORE, jnp.where(blk <= cur, imp_t, NEG_INF))
    score = jnp.where(blk < ns, score, BELOW_ALL)
    blk_f = blk.astype(F32)

    def pick(_, carry):
        sc, sel = carry
        mx = jnp.max(sc, axis=0, keepdims=True)
        first = jnp.min(jnp.where(sc == mx, blk_f, float(LANES)), axis=0, keepdims=True)
        hit = blk_f == first
        return jnp.where(hit, BELOW_ALL, sc), jnp.where(hit, 1.0, sel)

    _, sel = lax.fori_loop(0, n_sel, pick, (score, jnp.zeros((LANES, tq), F32)))
    sel_ref[0, 0] = jnp.where(blk <= cur, sel, 0.0).T.astype(sel_ref.dtype)


def _nsa_compressed(q_cmp, k_cmp, v_cmp, gates, gate_b, *, seq, tq=256):
    b = q_cmp.shape[0]
    group = NSA_HEADS // NSA_KV_HEADS
    ncp = k_cmp.shape[1]
    ns = seq // SLC_BLOCK
    n_sel = min(N_SELECT, ns)
    c_start = np.arange(ncp)[:, None] * CMP_STRIDE
    s_start = np.arange(LANES)[None, :] * SLC_BLOCK
    overlap = np.maximum(np.minimum(c_start + CMP_BLOCK, s_start + SLC_BLOCK) - np.maximum(c_start, s_start), 0)
    overlap = np.where((np.arange(LANES)[None, :] < ns) & (np.arange(ncp)[:, None] < ncp - 1), overlap, 0)
    overlap = jnp.asarray(overlap, BF16)
    gw = group * LANES
    tq = min(tq, seq)
    parts = max(1, min(4, seq // (4 * tq)))
    seg = seq // parts
    o_parts, sel_parts = [], []
    for part in range(parts):
        q_off = part * (seg // tq)
        visible = ((part + 1) * seg - CMP_BLOCK) // CMP_STRIDE + 1
        nck = min(ncp, -(-visible // LANES) * LANES)
        o_seg, sel_seg = pl.pallas_call(
            functools.partial(_nsa_cmp_kernel, tq=tq, group=group, n_sel=n_sel, ns=ns, q_off=q_off),
            grid=(b, NSA_KV_HEADS, seg // tq),
            in_specs=[pl.BlockSpec(memory_space=pltpu.SMEM),
                      pl.BlockSpec((1, tq, gw), lambda bi, h, i, q_off=q_off: (bi, i + q_off, h)),
                      pl.BlockSpec((1, nck, LANES), lambda bi, h, i: (bi, 0, 0)),
                      pl.BlockSpec((1, nck, LANES), lambda bi, h, i: (bi, 0, 0)),
                      pl.BlockSpec((nck, LANES), lambda bi, h, i: (0, 0)),
                      pl.BlockSpec((1, tq, LANES), lambda bi, h, i, q_off=q_off: (bi, i + q_off, 0))],
            out_specs=[pl.BlockSpec((1, tq, group * HEAD_DIM), lambda bi, h, i: (bi, i, h)),
                       pl.BlockSpec((1, 1, tq, LANES), lambda bi, h, i: (bi, h, i, 0))],
            out_shape=[jax.ShapeDtypeStruct((b, seg, NSA_HEADS * HEAD_DIM), F32),
                       jax.ShapeDtypeStruct((b, NSA_KV_HEADS, seg, LANES), BF16)],
            compiler_params=_params(("arbitrary", "arbitrary", "arbitrary")),
            name="nsa_compressed_select",
        )(gate_b, q_cmp, k_cmp, v_cmp, overlap, gates)
        o_parts.append(o_seg)
        sel_parts.append(sel_seg)
    return jnp.concatenate(o_parts, axis=1), jnp.concatenate(sel_parts, axis=2)


def _nsa_slc_kernel(gb_ref, q_ref, k_ref, v_ref, sel_ref, gl_ref, o_ref, sa_ref, sb_ref, *, tq, tk, group):
    hk = pl.program_id(1)
    i = pl.program_id(2)
    q = q_ref[0]
    unsel = (sel_ref[0, 0].astype(F32) - 1.0).astype(BF16)
    lhs = [jnp.concatenate([jnp.concatenate([q[:, g * LANES:(g + 1) * LANES], unsel], axis=1)
                            for g in (2 * c, 2 * c + 1)], axis=0) for c in range(group // 2)]
    n_full = (i * tq) // tk
    chains = group // 2

    def logits_into(buf, j):
        off = pl.multiple_of(j * tk, tk)
        kt = k_ref[0, pl.ds(off, tk), :]
        for c in range(chains):
            buf[c] = _dot_nt(lhs[c], kt)

    def consume(buf, j, states, masked):
        off = pl.multiple_of(j * tk, tk)
        vt = v_ref[0, pl.ds(off, tk), :]
        out = []
        for c in range(chains):
            m, acc = states[c]
            s = buf[c]
            if masked:
                t = i * tq + lax.broadcasted_iota(jnp.int32, (tq, tk), 0)
                key = j * tk + lax.broadcasted_iota(jnp.int32, (tq, tk), 1)
                ok = key <= t
                s = jnp.where(jnp.concatenate([ok, ok], axis=0), s, NEG_INF)
            m_new = jnp.maximum(m, jnp.max(s, axis=-1, keepdims=True))
            p = jnp.exp2(s - m_new)
            out.append((m_new, jnp.exp2(m - m_new) * acc + _dot(p.astype(BF16), vt)))
        return tuple(out)

    def run(j, states, steps):
        bufs = (sa_ref, sb_ref)
        for n in range(steps):
            logits_into(bufs[(n + 1) % 2], j + n + 1)
            states = consume(bufs[n % 2], j + n, states, False)
        return states

    init = tuple((jnp.full((2 * tq, 1), NEG_INF, F32), jnp.zeros((2 * tq, LANES), F32)) for _ in range(chains))
    logits_into(sa_ref, 0)
    states = lax.fori_loop(0, n_full // 4, lambda jj, st: run(4 * jj, st, 4), init)
    states = lax.fori_loop(0, (n_full % 4) // 2, lambda jj, st: run(4 * (n_full // 4), st, 2), states)
    r = 2 * (n_full // 2)

    def diagonal_is_next(states):
        return consume(sa_ref, r, states, True)

    def one_full_tile_left(states):
        logits_into(sb_ref, r + 1)
        return consume(sb_ref, r + 1, consume(sa_ref, r, states, False), True)

    carry = lax.cond(r == n_full, diagonal_is_next, one_full_tile_left, states)
    gl = gl_ref[0]
    heads = []
    for g in range(group):
        acc = carry[g // 2][1][(g % 2) * tq:(g % 2 + 1) * tq]
        colg = (hk * group + g) * 3 + 1
        gate = _sigmoid(_gate_column(gl, colg) + gb_ref[colg])
        heads.append(acc * (gate / acc[:, HEAD_DIM:HEAD_DIM + 1]))
    o_ref[0] = _compact_group(heads, None)


def _nsa_selected(q_rot, k_aug, v_exp, sel, gates, gate_b, *, seq, tq=128, tk=512):
    b = q_rot.shape[0]
    group = NSA_HEADS // NSA_KV_HEADS
    tk = min(tk, seq)
    gw = group * LANES
    return pl.pallas_call(
        functools.partial(_nsa_slc_kernel, tq=tq, tk=tk, group=group),
        grid=(b, NSA_KV_HEADS, seq // tq),
        in_specs=[pl.BlockSpec(memory_space=pltpu.SMEM),
                  pl.BlockSpec((1, tq, gw), lambda bi, h, i: (bi, i, h)),
                  pl.BlockSpec((1, seq, 2 * LANES), lambda bi, h, i: (bi, 0, 0)),
                  pl.BlockSpec((1, seq, LANES), lambda bi, h, i: (bi, 0, h)),
                  pl.BlockSpec((1, 1, tq, LANES), lambda bi, h, i: (bi, h, i, 0)),
                  pl.BlockSpec((1, tq, LANES), lambda bi, h, i: (bi, i, 0))],
        out_specs=pl.BlockSpec((1, tq, group * HEAD_DIM), lambda bi, h, i: (bi, i, h)),
        out_shape=jax.ShapeDtypeStruct((b, seq, NSA_HEADS * HEAD_DIM), F32),
        scratch_shapes=[pltpu.VMEM((group // 2, 2 * tq, tk), F32), pltpu.VMEM((group // 2, 2 * tq, tk), F32)],
        compiler_params=_params(("arbitrary", "arbitrary", "arbitrary")),
        name="nsa_selected",
    )(gate_b, q_rot, k_aug, v_exp, sel, gates)


def _window_kernel(sc_ref, q_ref, k_ref, v_ref, gl_ref, o_ref, sa_ref, sb_ref, *, tq, nt, span, window, group,
                   gated, sinks):
    hk = pl.program_id(1)
    sblk = pl.program_id(2)
    chains = group // 2

    def tile(it):
        gi = sblk * nt + it
        start = pl.multiple_of(jnp.maximum(gi * tq + tq - span, 0), tq)
        return gi, start, pl.ds(pl.multiple_of(it * tq, tq), tq)

    def logits_into(buf, it):
        _, start, rows = tile(it)
        kt = k_ref[0, pl.ds(start, span), :]
        for c in range(chains):
            lhs = jnp.concatenate([q_ref[0, rows, g * LANES:(g + 1) * LANES] for g in (2 * c, 2 * c + 1)], axis=0)
            buf[c] = _dot_nt(lhs, kt)

    def consume(buf, it):
        gi, start, rows = tile(it)
        vt = v_ref[0, pl.ds(start, span), :]
        t = gi * tq + lax.broadcasted_iota(jnp.int32, (tq, span), 0)
        key = start + lax.broadcasted_iota(jnp.int32, (tq, span), 1)
        ok = (key <= t) & (t - key < window)
        ok = jnp.concatenate([ok, ok], axis=0)
        gl = gl_ref[0, rows, :]
        heads = []
        for c in range(chains):
            s = jnp.where(ok, buf[c], NEG_INF)
            m = jnp.max(s, axis=-1, keepdims=True)
            if sinks:
                sk = jnp.concatenate([jnp.full((tq, 1), sc_ref[hk * group + g] * LOG2E, F32)
                                      for g in (2 * c, 2 * c + 1)], axis=0)
                m = jnp.maximum(m, sk)
            acc = _dot(jnp.exp2(s - m).astype(BF16), vt)
            denom = acc[:, HEAD_DIM:HEAD_DIM + 1]
            if sinks:
                denom = denom + jnp.exp2(sk - m)
            for r in range(2):
                g = 2 * c + r
                scale = 1.0 / denom[r * tq:(r + 1) * tq]
                if gated:
                    colg = (hk * group + g) * 3 + 2
                    scale = scale * _sigmoid(_gate_column(gl, colg) + sc_ref[colg])
                heads.append(acc[r * tq:(r + 1) * tq] * scale)
        o_ref[0, rows, :] = _compact_group(heads, None)

    logits_into(sa_ref, 0)

    def quad(jj, carry):
        bufs = (sa_ref, sb_ref)
        for n in range(4):
            it = 4 * jj + n
            logits_into(bufs[(n + 1) % 2], jnp.minimum(it + 1, nt - 1))
            consume(bufs[n % 2], it)
        return carry

    lax.fori_loop(0, nt // 4, quad, 0)


def _window_attention(q_rot, q_blk0, k_pairs, kv_blk, v_heads, scalars, gates, *, seq, window, gated, sinks,
                      tq=128, rows=2048):
    b = q_rot.shape[0]
    group = 4
    span = min(window + tq, seq)
    rows = min(rows, seq)
    nt = rows // tq
    assert nt % 4 == 0
    gw = group * LANES
    qb = q_blk0 // group
    return pl.pallas_call(
        functools.partial(_window_kernel, tq=tq, nt=nt, span=span, window=window, group=group, gated=gated,
                          sinks=sinks),
        grid=(b, 2, seq // rows),
        in_specs=[pl.BlockSpec(memory_space=pltpu.SMEM),
                  pl.BlockSpec((1, rows, gw), lambda bi, h, i: (bi, i, qb + h)),
                  pl.BlockSpec((1, seq, LANES), lambda bi, h, i: (bi, 0, kv_blk)),
                  pl.BlockSpec((1, seq, LANES), lambda bi, h, i: (bi, 0, 2 * kv_blk + h)),
                  pl.BlockSpec((1, rows, LANES), lambda bi, h, i: (bi, i, 0))],
        out_specs=pl.BlockSpec((1, rows, group * HEAD_DIM), lambda bi, h, i: (bi, i, h)),
        out_shape=jax.ShapeDtypeStruct((b, seq, 8 * HEAD_DIM), F32),
        scratch_shapes=[pltpu.VMEM((group // 2, 2 * tq, span), F32), pltpu.VMEM((group // 2, 2 * tq, span), F32)],
        compiler_params=_params(("arbitrary", "arbitrary", "arbitrary")),
        name="window_attention",
    )(scalars, q_rot, k_pairs, v_heads, gates)


def _dense_kernel(q_ref, k_ref, v_ref, o_ref, sa_ref, sb_ref, *, tq, tk, nh):
    i = pl.program_id(2)
    t0 = i * tq
    n_full = t0 // tk
    qs = [q_ref[0][:, e * LANES:(e + 1) * LANES] for e in range(nh)]

    def logits_into(buf, j):
        off = pl.multiple_of(j * tk, tk)
        for e in range(nh):
            buf[e] = _dot_nt(qs[e], k_ref[0, pl.ds(off, tk), e * LANES:(e + 1) * LANES])

    def consume(buf, j, states, masked):
        off = pl.multiple_of(j * tk, tk)
        out = []
        for e in range(nh):
            m, acc = states[e]
            s = buf[e]
            if masked:
                t = t0 + lax.broadcasted_iota(jnp.int32, (tq, tk), 0)
                key = j * tk + lax.broadcasted_iota(jnp.int32, (tq, tk), 1)
                s = jnp.where(key <= t, s, NEG_INF)
            vt = v_ref[0, pl.ds(off, tk), e * LANES:(e + 1) * LANES]
            m_new = jnp.maximum(m, jnp.max(s, axis=-1, keepdims=True))
            p = jnp.exp2(s - m_new)
            out.append((m_new, jnp.exp2(m - m_new) * acc + _dot(p.astype(BF16), vt)))
        return tuple(out)

    def run(j, states, steps):
        bufs = (sa_ref, sb_ref)
        for n in range(steps):
            logits_into(bufs[(n + 1) % 2], j + n + 1)
            states = consume(bufs[n % 2], j + n, states, False)
        return states

    init = tuple((jnp.full((tq, 1), NEG_INF, F32), jnp.zeros((tq, LANES), F32)) for _ in range(nh))
    logits_into(sa_ref, 0)
    states = lax.fori_loop(0, n_full // 4, lambda jj, st: run(4 * jj, st, 4), init)
    states = lax.fori_loop(0, (n_full % 4) // 2, lambda jj, st: run(4 * (n_full // 4), st, 2), states)
    r = 2 * (n_full // 2)

    def diagonal_is_next(states):
        return consume(sa_ref, r, states, True)

    def one_full_tile_left(states):
        logits_into(sb_ref, r + 1)
        return consume(sb_ref, r + 1, consume(sa_ref, r, states, False), True)

    states = lax.cond(r == n_full, diagonal_is_next, one_full_tile_left, states)
    outs = [acc * (1.0 / acc[:, HEAD_DIM:HEAD_DIM + 1]) for _, acc in states]
    lane = _lane((tq, LANES))
    o_ref[0] = jnp.concatenate([jnp.where(lane < HEAD_DIM, outs[e], pltpu.roll(outs[e + 1], HEAD_DIM, 1))
                                for e in range(0, nh, 2)], axis=1)


def _dense_attention(q, k, v, *, seq, tq=256, tk=512, nh=2):
    b = q.shape[0]
    heads = q.shape[2] // LANES
    tk = min(tk, seq)
    tq = min(tq, tk)
    return pl.pallas_call(
        functools.partial(_dense_kernel, tq=tq, tk=tk, nh=nh),
        grid=(b, heads // nh, seq // tq),
        in_specs=[pl.BlockSpec((1, tq, nh * LANES), lambda bi, p, i: (bi, i, p)),
                  pl.BlockSpec((1, seq, nh * LANES), lambda bi, p, i: (bi, 0, p)),
                  pl.BlockSpec((1, seq, nh * LANES), lambda bi, p, i: (bi, 0, p))],
        out_specs=pl.BlockSpec((1, tq, nh * HEAD_DIM), lambda bi, p, i: (bi, i, p)),
        out_shape=jax.ShapeDtypeStruct((b, seq, heads * HEAD_DIM), F32),
        scratch_shapes=[pltpu.VMEM((nh, tq, tk), F32), pltpu.VMEM((nh, tq, tk), F32)],
        compiler_params=_params(("arbitrary", "arbitrary", "arbitrary")),
        name="dense_causal_attention",
    )(q, k, v)


def _decay_kernel(f_ref, b_ref, o_ref):
    x = f_ref[0] + b_ref[...]
    lf = jnp.minimum(x, 0.0) - jnp.log1p(jnp.exp(-jnp.abs(x)))
    n = lf.shape[-1]
    lane = _lane(lf.shape)
    d = 1
    while d < n:
        lf = lf + jnp.where(lane >= d, pltpu.roll(lf, d, 1), 0.0)
        d *= 2
    o_ref[0] = lf * LOG2E


def _decay_cumsum(f_t, bias):
    b, h, s = f_t.shape
    return pl.pallas_call(
        _decay_kernel,
        grid=(b,),
        in_specs=[pl.BlockSpec((1, h, s), lambda i: (i, 0, 0)), pl.BlockSpec((h, 1), lambda i: (0, 0))],
        out_specs=pl.BlockSpec((1, h, s), lambda i: (i, 0, 0)),
        out_shape=jax.ShapeDtypeStruct((b, h, s), F32),
        compiler_params=_params(("arbitrary",)),
        name="fox_decay_cumsum",
    )(f_t, bias.reshape(h, 1).astype(F32))


def _mla_prep_kernel(cq_ref, ckv_ref, misc_ref, gqa_ref, gkva_ref, wq_ref, wk_ref, wv_ref, gq_ref, gk_ref,
                     cos_ref, sin_ref, ones_ref, q_ref, k_ref, v_ref, *, scale):
    tm = cq_ref.shape[0]
    lane = _lane((tm, LANES))
    in_rope = (lane >= MLA_NOPE_DIM) & (lane < MLA_QK_DIM)
    first = lane < MLA_NOPE_DIM + MLA_ROPE_DIM // 2
    cos, sin = cos_ref[...], sin_ref[...]

    def rope_tail(x):
        sw = jnp.where(first, pltpu.roll(x, LANES - MLA_ROPE_DIM // 2, 1), pltpu.roll(x, MLA_ROPE_DIM // 2, 1))
        return x * cos + jnp.where(in_rope, sw, 0.0) * sin

    cq = _rms(cq_ref[...], gqa_ref[...], MLA_Q_RANK).astype(BF16)
    ckv = _rms(ckv_ref[...], gkva_ref[...], MLA_KV_RANK).astype(BF16)
    qa = _dot(cq, wq_ref[...])
    ka = _dot(ckv, wk_ref[...])
    k_rope = jnp.where(in_rope, misc_ref[...], 0.0)
    def head_rms(x, gain):
        ms = _split_dot(x * x, ones_ref[...]) * (1.0 / MLA_QK_DIM)
        return (x * lax.rsqrt(ms + RMS_EPS)) * gain

    for h in range(MLA_HEADS):
        qh = head_rms(qa[:, h * LANES:(h + 1) * LANES], gq_ref[...])
        q_ref[:, h * LANES:(h + 1) * LANES] = (rope_tail(qh) * scale).astype(q_ref.dtype)
        kh = head_rms(ka[:, h * LANES:(h + 1) * LANES] + k_rope, gk_ref[...])
        k_ref[:, h * LANES:(h + 1) * LANES] = rope_tail(kh).astype(k_ref.dtype)
    v = _dot(ckv, wv_ref[...])
    ones_col = (_lane(v.shape) & (LANES - 1)) == MLA_V_DIM
    v_ref[...] = jnp.where(ones_col, 1.0, v).astype(v_ref.dtype)


def _mla_prep(c_q, c_kv, misc, q_a_norm, w_q_b, kv_a_norm, w_kv_b, q_norm, k_norm, *, seq, tm=512):
    n = c_q.shape[0]
    h = MLA_HEADS
    pad = LANES - MLA_QK_DIM
    wq = jnp.pad(w_q_b.reshape(MLA_Q_RANK, h, MLA_QK_DIM), ((0, 0), (0, 0), (0, pad)))
    wq = wq.reshape(MLA_Q_RANK, h * LANES).astype(BF16)
    wkv = w_kv_b.reshape(MLA_KV_RANK, h, MLA_NOPE_DIM + MLA_V_DIM)
    wk = jnp.pad(wkv[:, :, :MLA_NOPE_DIM], ((0, 0), (0, 0), (0, LANES - MLA_NOPE_DIM)))
    wk = wk.reshape(MLA_KV_RANK, h * LANES).astype(BF16)
    wv = jnp.pad(wkv[:, :, MLA_NOPE_DIM:], ((0, 0), (0, 0), (0, LANES - MLA_V_DIM)))
    wv = wv.reshape(MLA_KV_RANK, h * LANES).astype(BF16)
    gq = jnp.pad(q_norm, (0, pad)).reshape(1, LANES)
    gk = jnp.pad(k_norm, (0, pad)).reshape(1, LANES)
    half = MLA_ROPE_DIM // 2
    inv_freq = 1.0 / (ROPE_THETA ** (jnp.arange(0, MLA_ROPE_DIM, 2, dtype=F32) / MLA_ROPE_DIM))
    ang = jnp.arange(seq, dtype=F32)[:, None] * inv_freq[None, :]
    cos, sin = jnp.cos(ang), jnp.sin(ang)
    ones = jnp.ones((seq, MLA_NOPE_DIM), F32)
    zeros = jnp.zeros((seq, MLA_NOPE_DIM), F32)
    cos_t = jnp.concatenate([ones, cos, cos, ones[:, :pad]], axis=1)
    sin_t = jnp.concatenate([zeros, -sin, sin, zeros[:, :pad]], axis=1)
    del half
    sblocks = seq // tm
    full = lambda shp: pl.BlockSpec(shp, lambda i: (0,) * len(shp))
    q, k, v = pl.pallas_call(
        functools.partial(_mla_prep_kernel, scale=MLA_QK_DIM ** -0.5 * LOG2E),
        grid=(n // tm,),
        in_specs=[pl.BlockSpec((tm, MLA_Q_RANK), lambda i: (i, 0)),
                  pl.BlockSpec((tm, LANES), lambda i: (i, 0)),
                  pl.BlockSpec((tm, LANES), lambda i: (i, 0)),
                  full((1, MLA_Q_RANK)), full((1, MLA_KV_RANK)),
                  full((MLA_Q_RANK, h * LANES)), full((MLA_KV_RANK, h * LANES)), full((MLA_KV_RANK, h * LANES)),
                  full((1, LANES)), full((1, LANES)),
                  pl.BlockSpec((tm, LANES), lambda i: (i % sblocks, 0)),
                  pl.BlockSpec((tm, LANES), lambda i: (i % sblocks, 0)),
                  full((LANES, LANES))],
        out_specs=[pl.BlockSpec((tm, h * LANES), lambda i: (i, 0)),
                   pl.BlockSpec((tm, h * LANES), lambda i: (i, 0)),
                   pl.BlockSpec((tm, h * LANES), lambda i: (i, 0))],
        out_shape=[jax.ShapeDtypeStruct((n, h * LANES), BF16),
                   jax.ShapeDtypeStruct((n, h * LANES), BF16),
                   jax.ShapeDtypeStruct((n, h * LANES), BF16)],
        compiler_params=_params(("arbitrary",)),
        name="mla_prep",
    )(c_q, c_kv, misc, q_a_norm.reshape(1, -1), kv_a_norm.reshape(1, -1), wq, wk, wv, gq, gk, cos_t, sin_t,
      _group_sum_matrix(LANES))
    return q, k, v


def _cols(w, a, b):
    return w[:, a:b]


def _nsa_swa_mixer(x, batch, seq, mix_norm, w_in, nsa_gate_b, nsa_q_norm, nsa_kc_norm, nsa_ks_norm, nsa_kw_norm,
                   cmp_pos_k, cmp_pos_v, cmpk_w1, cmpk_w2, cmpv_w1, cmpv_w2,
                   swa_q_norm, swa_k_norm, swa_sinks, w_out):
    n = batch * seq
    o = np.cumsum([0, 512, 128, 128, 128, 128, 128, 128, 24, 512, 128, 128])
    seg = lambda j: _cols(w_in, o[j], o[j + 1])
    q_a, kc, vc, ks, vs, kw, vw, gl, q_b, k_b, v_b = [seg(j) for j in range(11)]
    gl = jnp.pad(gl, ((0, 0), (0, LANES - gl.shape[1])))
    w = jnp.concatenate([q_a, q_b, ks, kw, k_b, kc, vc, vs, vw, v_b, gl], axis=1).astype(BF16)
    cos_t, sin_t = _rope_tables(seq)

    s_q = HEAD_DIM ** -0.5 * LOG2E
    gains = jnp.stack([_pair_gain(g) for g in (nsa_q_norm, swa_q_norm, nsa_ks_norm, nsa_kw_norm, swa_k_norm)])
    jobs = [_Job(blk=c, out=0, col=2 * c, gain=c // 4, rope=True, scale=s_q, mode="q", dst=((c % 4) // 2,) * 2)
            for c in range(8)]
    jobs += [_Job(blk=c, out=1, col=2 * c, gain=0, scale=s_q, mode="q", dst=(c // 2,) * 2)
             for c in range(4)]
    jobs += [_Job(blk=8, out=2, col=0, gain=2, rope=True, mode="kaug"),
             _Job(blk=9, out=3, col=0, gain=3, rope=True), _Job(blk=10, out=3, col=1, gain=4, rope=True),
             _Job(blk=13, out=4, col=0, mode="v"),
             _Job(blk=14, out=5, col=0, mode="v"), _Job(blk=15, out=5, col=2, mode="v"),
             _Job(blk=11, out=6, col=0), _Job(blk=12, out=7, col=0), _Job(blk=16, out=8, col=0)]
    outs = [(16, BF16), (8, BF16), (2, BF16), (2, BF16), (2, BF16), (4, BF16), (1, F32), (1, F32), (1, F32)]
    q_rot, q_cmp, k_aug, k_ws, v_slc, v_ws, kc_raw, vc_raw, gates = (
        a.reshape(batch, seq, -1) for a in _proj_prep(x, mix_norm, w, jobs, outs, gains, seq, cos_t, sin_t))

    k_cmp = _compress(kc_raw, cmp_pos_k, cmpk_w1, cmpk_w2, nsa_kc_norm)
    v_cmp = _compress(vc_raw, cmp_pos_v, cmpv_w1, cmpv_w2, None)
    gate_b = nsa_gate_b.astype(F32)

    o_cmp, sel = _nsa_compressed(q_cmp, k_cmp, v_cmp, gates, gate_b, seq=seq)
    o_slc = _nsa_selected(q_rot, k_aug, v_slc, sel, gates, gate_b, seq=seq)
    o_win = _window_attention(q_rot, 0, k_ws, 0, v_ws, gate_b, gates, seq=seq, window=NSA_WINDOW,
                              gated=True, sinks=False)
    o_swa = _window_attention(q_rot, 8, k_ws, 1, v_ws, swa_sinks.astype(F32), gates, seq=seq, window=SWA_WINDOW,
                              gated=False, sinks=True)
    flat = lambda a: a.reshape(n, -1)
    return _outproj(x, [flat(o_cmp), flat(o_slc), flat(o_win)], [flat(o_swa)], w_out)


def _fox_mla_mixer(x, batch, seq, mix_norm, w_in, fox_f_bias, fox_q_norm, fox_k_norm, mla_q_a_norm, mla_w_q_b,
                   mla_kv_a_norm, mla_w_kv_b, mla_q_norm, mla_k_norm, w_out):
    n = batch * seq
    o = np.cumsum([0, 512, 512, 512, 8, 256, 128, 32])
    seg = lambda j: _cols(w_in, o[j], o[j + 1])
    q_c, k_c, v_c, f_c, c_q, c_kv, k_r = [seg(j) for j in range(7)]
    d = w_in.shape[0]
    misc = jnp.concatenate([f_c, jnp.zeros((d, MLA_NOPE_DIM - 8), w_in.dtype), k_r,
                            jnp.zeros((d, LANES - MLA_QK_DIM), w_in.dtype)], axis=1)
    w = jnp.concatenate([q_c, k_c, v_c, c_q, c_kv, misc], axis=1).astype(BF16)
    cos_t, sin_t = _rope_tables(seq)
    gains = jnp.stack([_pair_gain(fox_q_norm), _pair_gain(fox_k_norm)])
    jobs = [_Job(blk=c, out=0, col=2 * c, gain=0, scale=HEAD_DIM ** -0.5 * LOG2E, mode="q", aug="ones")
            for c in range(4)]
    jobs += [_Job(blk=8 + c, out=1, col=2 * c, mode="v") for c in range(4)]
    jobs += [_Job(blk=4 + c, out=2, col=c) for c in range(4)]
    jobs += [_Job(blk=12, out=3, col=0), _Job(blk=13, out=3, col=1), _Job(blk=14, out=4, col=0),
             _Job(blk=15, out=5, col=0)]
    outs = [(8, BF16), (8, BF16), (4, F32), (2, F32), (1, F32), (1, F32)]
    q_f, v_f, kc_raw, c_q_out, c_kv_out, misc_out = _proj_prep(x, mix_norm, w, jobs, outs, gains, seq, cos_t, sin_t)

    f_t = misc_out[:, :FOX_HEADS].reshape(batch, seq, FOX_HEADS).transpose(0, 2, 1)
    dc = _decay_cumsum(f_t, fox_f_bias)
    dc_tok = jnp.pad(dc.transpose(0, 2, 1).reshape(n, FOX_HEADS), ((0, 0), (0, LANES - FOX_HEADS)))
    kjobs = [_Job(blk=c, out=0, col=2 * c, gain=1, mode="q", aug="decay", heads=(2 * c, 2 * c + 1))
             for c in range(4)]
    (k_f,) = _prep(kc_raw, kjobs, [(8, BF16)], gains, seq, cos_t, sin_t, aux=dc_tok)
    b3 = lambda a: a.reshape(batch, seq, -1)
    o_fox = _dense_attention(b3(q_f), b3(k_f), b3(v_f), seq=seq)

    q_m, k_m, v_m = _mla_prep(c_q_out, c_kv_out, misc_out, mla_q_a_norm, mla_w_q_b, mla_kv_a_norm, mla_w_kv_b,
                              mla_q_norm, mla_k_norm, seq=seq)
    o_mla = _dense_attention(b3(q_m), b3(k_m), b3(v_m), seq=seq)
    flat = lambda a: a.reshape(n, -1)
    return _outproj(x, [flat(o_fox)], [flat(o_mla)], w_out)


def kernel(x, l0_ffn1_norm, l0_ffn1_w_gate, l0_ffn1_w_up, l0_ffn1_w_down, l0_mix_norm, l0_w_in, l0_nsa_gate_b, l0_nsa_q_norm, l0_nsa_kc_norm, l0_nsa_ks_norm, l0_nsa_kw_norm, l0_cmp_pos_k, l0_cmp_pos_v, l0_cmpk_w1, l0_cmpk_w2, l0_cmpv_w1, l0_cmpv_w2, l0_swa_q_norm, l0_swa_k_norm, l0_swa_sinks, l0_w_out, l0_ffn2_norm, l0_ffn2_w_gate, l0_ffn2_w_up, l0_ffn2_w_down, l1_ffn1_norm, l1_ffn1_w_gate, l1_ffn1_w_up, l1_ffn1_w_down, l1_mix_norm, l1_w_in, l1_fox_f_bias, l1_fox_q_norm, l1_fox_k_norm, l1_mla_q_a_norm, l1_mla_w_q_b, l1_mla_kv_a_norm, l1_mla_w_kv_b, l1_mla_q_norm, l1_mla_k_norm, l1_w_out, l1_ffn2_norm, l1_ffn2_w_gate, l1_ffn2_w_up, l1_ffn2_w_down):
    batch, seq, d = x.shape
    h = x.reshape(batch * seq, d)
    h = _ffn(h, l0_ffn1_norm, l0_ffn1_w_gate, l0_ffn1_w_up, l0_ffn1_w_down)
    h = _nsa_swa_mixer(h, batch, seq, l0_mix_norm, l0_w_in, l0_nsa_gate_b, l0_nsa_q_norm, l0_nsa_kc_norm,
                       l0_nsa_ks_norm, l0_nsa_kw_norm, l0_cmp_pos_k, l0_cmp_pos_v, l0_cmpk_w1, l0_cmpk_w2,
                       l0_cmpv_w1, l0_cmpv_w2, l0_swa_q_norm, l0_swa_k_norm, l0_swa_sinks, l0_w_out)
    h = _ffn(h, l0_ffn2_norm, l0_ffn2_w_gate, l0_ffn2_w_up, l0_ffn2_w_down)
    h = _ffn(h, l1_ffn1_norm, l1_ffn1_w_gate, l1_ffn1_w_up, l1_ffn1_w_down)
    h = _fox_mla_mixer(h, batch, seq, l1_mix_norm, l1_w_in, l1_fox_f_bias, l1_fox_q_norm, l1_fox_k_norm,
                       l1_mla_q_a_norm, l1_mla_w_q_b, l1_mla_kv_a_norm, l1_mla_w_kv_b, l1_mla_q_norm,
                       l1_mla_k_norm, l1_w_out)
    h = _ffn(h, l1_ffn2_norm, l1_ffn2_w_gate, l1_ffn2_w_up, l1_ffn2_w_down)
    return h.reshape(batch, seq, d)
```

```python
import functools
from typing import NamedTuple, Optional

import numpy as np
import jax
import jax.numpy as jnp
from jax import lax
from jax.experimental import pallas as pl
from jax.experimental.pallas import tpu as pltpu

F32 = jnp.float32
BF16 = jnp.bfloat16

HEAD_DIM = 64
LANES = 128
ROPE_THETA = 10000.0
RMS_EPS = 1e-6
NEG_INF = -1e30
FORCE_SCORE = 1e9
BELOW_ALL = -3e38
LOG2E = 1.4426950408889634
BIG = 1e30
VT_ROWS = 80

NSA_HEADS = 8
NSA_KV_HEADS = 2
CMP_BLOCK = 32
CMP_STRIDE = 16
CMP_HIDDEN = 256
SLC_BLOCK = 64
N_SELECT = 16
NSA_WINDOW = 512
SWA_HEADS = 8
SWA_KV_HEADS = 2
SWA_WINDOW = 128
FOX_HEADS = 8
MLA_HEADS = 8
MLA_Q_RANK = 256
MLA_KV_RANK = 128
MLA_NOPE_DIM = 64
MLA_ROPE_DIM = 32
MLA_V_DIM = 64
MLA_QK_DIM = MLA_NOPE_DIM + MLA_ROPE_DIM

VMEM_LIMIT = 48 * 1024 * 1024

NT_DIMS = (((1,), (1,)), ((), ()))


def _params(sem):
    return pltpu.CompilerParams(dimension_semantics=sem, vmem_limit_bytes=VMEM_LIMIT)


def _dot(a, b):
    return jnp.dot(a, b, preferred_element_type=F32)


def _dot_nt(a, b):
    return lax.dot_general(a, b, NT_DIMS, preferred_element_type=F32)


def _rms(x, gain, n):
    ms = jnp.sum(x * x, axis=-1, keepdims=True) * (1.0 / n)
    return (x * lax.rsqrt(ms + RMS_EPS)) * gain


def _lane(shape):
    return lax.broadcasted_iota(jnp.int32, shape, len(shape) - 1)


def _split_dot(x, m):
    hi = x.astype(BF16)
    lo = (x - hi.astype(F32)).astype(BF16)
    return _dot(hi, m) + _dot(lo, m)


def _group_sum_matrix(width):
    lane = np.arange(LANES)
    return jnp.asarray(lane[:, None] // width == lane[None, :] // width, BF16)


def _ffn_kernel(x_ref, g_ref, wg_ref, wu_ref, wd_ref, o_ref, h_sc, acc_sc, *, tf):
    x = x_ref[...]
    h_sc[...] = _rms(x, g_ref[...], x.shape[-1]).astype(BF16)
    acc_sc[...] = jnp.zeros_like(acc_sc)
    f = wg_ref.shape[1]
    for c0 in range(0, f, tf):
        cols = slice(c0, min(c0 + tf, f))
        h = h_sc[...]
        g = _dot(h, wg_ref[:, cols])
        u = _dot(h, wu_ref[:, cols])
        a = (g * (1.0 / (1.0 + jnp.exp(-g)))) * u
        acc_sc[...] += _dot(a.astype(BF16), wd_ref[cols, :])
    o_ref[...] = x + 0.5 * acc_sc[...]


def _ffn(x, norm, w_gate, w_up, w_down, *, tm=512, tf=256):
    n, d = x.shape
    f = w_gate.shape[1]
    wg, wu, wd = w_gate.astype(BF16), w_up.astype(BF16), w_down.astype(BF16)
    wspec = lambda shp: pl.BlockSpec(shp, lambda i: (0, 0), pipeline_mode=pl.Buffered(1))
    return pl.pallas_call(
        functools.partial(_ffn_kernel, tf=tf),
        grid=(n // tm,),
        in_specs=[pl.BlockSpec((tm, d), lambda i: (i, 0)),
                  pl.BlockSpec((1, d), lambda i: (0, 0)),
                  wspec((d, f)), wspec((d, f)), wspec((f, d))],
        out_specs=pl.BlockSpec((tm, d), lambda i: (i, 0)),
        out_shape=jax.ShapeDtypeStruct((n, d), F32),
        scratch_shapes=[pltpu.VMEM((tm, d), BF16), pltpu.VMEM((tm, d), F32)],
        compiler_params=_params(("arbitrary",)),
        name="ffn",
    )(x, norm.reshape(1, d), wg, wu, wd)


def _outproj_kernel(*refs, n_a, n_b):
    x_ref = refs[0]
    a_refs = refs[1:1 + n_a]
    b_refs = refs[1 + n_a:1 + n_a + n_b]
    wa_ref, wb_ref, o_ref = refs[1 + n_a + n_b:]
    a = a_refs[0][...]
    for r in a_refs[1:]:
        a = a + r[...]
    b = b_refs[0][...]
    for r in b_refs[1:]:
        b = b + r[...]
    o_ref[...] = x_ref[...] + _dot(a.astype(BF16), wa_ref[...]) + _dot(b.astype(BF16), wb_ref[...])


def _outproj(x, a_list, b_list, w_out, *, tm=512):
    n, d = x.shape
    ca = a_list[0].shape[1]
    cb = b_list[0].shape[1]
    wa = w_out[:ca].astype(BF16)
    wb = w_out[ca:].astype(BF16)
    row = lambda c: pl.BlockSpec((tm, c), lambda i: (i, 0))
    return pl.pallas_call(
        functools.partial(_outproj_kernel, n_a=len(a_list), n_b=len(b_list)),
        grid=(n // tm,),
        in_specs=[row(d)] + [row(ca)] * len(a_list) + [row(cb)] * len(b_list)
                 + [pl.BlockSpec((ca, d), lambda i: (0, 0)), pl.BlockSpec((cb, d), lambda i: (0, 0))],
        out_specs=row(d),
        out_shape=jax.ShapeDtypeStruct((n, d), F32),
        compiler_params=_params(("arbitrary",)),
        name="outproj",
    )(x, *a_list, *b_list, wa, wb)


class _Job(NamedTuple):
    blk: int
    out: int
    col: int
    gain: Optional[int] = None
    rope: bool = False
    scale: float = 1.0
    mode: str = "plain"
    dst: tuple = (0, 0)
    aug: Optional[str] = None
    heads: tuple = (0, 0)


def _prep_kernel(y_ref, gain_ref, cos_ref, sin_ref, aux_ref, hsum_ref, *o_refs, jobs, seq):
    tm = y_ref.shape[0]
    lane = _lane((tm, LANES))
    low = lane < HEAD_DIM
    zero = jnp.zeros((tm, LANES), F32)
    for job in jobs:
        x = y_ref[:, job.blk * LANES:(job.blk + 1) * LANES]
        if job.gain is not None:
            ms = _split_dot(x * x, hsum_ref[...]) * (1.0 / HEAD_DIM)
            x = (x * lax.rsqrt(ms + RMS_EPS)) * gain_ref[job.gain]
        if job.rope:
            swapped = jnp.where((lane & (HEAD_DIM - 1)) < HEAD_DIM // 2,
                                pltpu.roll(x, LANES - HEAD_DIM // 2, 1), pltpu.roll(x, HEAD_DIM // 2, 1))
            x = x * cos_ref[...] + swapped * sin_ref[...]
        if job.scale != 1.0:
            x = x * job.scale
        if job.mode == "plain":
            pieces = [x]
        elif job.mode == "kaug":
            pos = (pl.program_id(0) * tm + lax.broadcasted_iota(jnp.int32, (tm, LANES), 0)) % seq
            pieces = [x, jnp.where(lane == pos // SLC_BLOCK, BIG, 0.0)]
        elif job.mode == "v":
            r = pltpu.roll(x, HEAD_DIM, 1)
            tail = jnp.where(lane == HEAD_DIM, 1.0, 0.0)
            pieces = [jnp.where(low, x, tail), jnp.where(low, r, tail)]
        else:
            r = pltpu.roll(x, HEAD_DIM, 1)
            h_even = jnp.where(low, x, zero) if job.dst[0] == 0 else jnp.where(low, zero, r)
            h_odd = jnp.where(low, r, zero) if job.dst[1] == 0 else jnp.where(low, zero, x)
            if job.aug is not None:
                tails = []
                for e in range(2):
                    if job.aug == "ones":
                        tails.append(jnp.where((lane >= HEAD_DIM) & (lane < HEAD_DIM + 3), 1.0, 0.0))
                    else:
                        d = jnp.sum(jnp.where(lane == job.heads[e], aux_ref[...], 0.0), axis=-1, keepdims=True)
                        hi = d.astype(BF16).astype(F32)
                        mid = (d - hi).astype(BF16).astype(F32)
                        lo = d - hi - mid
                        tails.append(jnp.where(lane == HEAD_DIM, -hi, jnp.where(lane == HEAD_DIM + 1, -mid,
                                     jnp.where(lane == HEAD_DIM + 2, -lo, 0.0))))
                h_even = jnp.where(low, h_even, tails[0])
                h_odd = jnp.where(low, h_odd, tails[1])
            pieces = [h_even, h_odd]
        o_ref = o_refs[job.out]
        for n, piece in enumerate(pieces):
            o_ref[:, (job.col + n) * LANES:(job.col + n + 1) * LANES] = piece.astype(o_ref.dtype)


def _prep(y, jobs, outs, gains, seq, cos_t, sin_t, *, aux=None, tm=512):
    n, c = y.shape
    aux_spec = pl.BlockSpec((tm, LANES), lambda i: (i, 0))
    if aux is None:
        aux, aux_spec = jnp.zeros((tm, LANES), F32), pl.BlockSpec((tm, LANES), lambda i: (0, 0))
    sblocks = seq // tm
    return pl.pallas_call(
        functools.partial(_prep_kernel, jobs=tuple(jobs), seq=seq),
        grid=(n // tm,),
        in_specs=[pl.BlockSpec((tm, c), lambda i: (i, 0)),
                  pl.BlockSpec(gains.shape, lambda i: (0, 0, 0)),
                  pl.BlockSpec((tm, LANES), lambda i: (i % sblocks, 0)),
                  pl.BlockSpec((tm, LANES), lambda i: (i % sblocks, 0)),
                  aux_spec,
                  pl.BlockSpec((LANES, LANES), lambda i: (0, 0))],
        out_specs=[pl.BlockSpec((tm, w * LANES), lambda i: (i, 0)) for w, _ in outs],
        out_shape=[jax.ShapeDtypeStruct((n, w * LANES), dt) for w, dt in outs],
        compiler_params=_params(("arbitrary",)),
        name="head_prep",
    )(y, gains, cos_t, sin_t, aux, _group_sum_matrix(HEAD_DIM))


def _proj_prep_kernel(x_ref, g_ref, w_ref, gain_ref, cos_ref, sin_ref, aux_ref, hsum_ref, *rest, jobs, seq):
    *o_refs, y_sc = rest
    x = x_ref[...]
    y_sc[...] = _dot(_rms(x, g_ref[...], x.shape[-1]).astype(BF16), w_ref[...])
    _prep_kernel(y_sc, gain_ref, cos_ref, sin_ref, aux_ref, hsum_ref, *o_refs, jobs=jobs, seq=seq)


def _proj_prep(x, norm, w, jobs, outs, gains, seq, cos_t, sin_t, *, tm=512):
    n, d = x.shape
    c = w.shape[1]
    sblocks = seq // tm
    return pl.pallas_call(
        functools.partial(_proj_prep_kernel, jobs=tuple(jobs), seq=seq),
        grid=(n // tm,),
        in_specs=[pl.BlockSpec((tm, d), lambda i: (i, 0)),
                  pl.BlockSpec((1, d), lambda i: (0, 0)),
                  pl.BlockSpec((d, c), lambda i: (0, 0), pipeline_mode=pl.Buffered(1)),
                  pl.BlockSpec(gains.shape, lambda i: (0, 0, 0)),
                  pl.BlockSpec((tm, LANES), lambda i: (i % sblocks, 0)),
                  pl.BlockSpec((tm, LANES), lambda i: (i % sblocks, 0)),
                  pl.BlockSpec((tm, LANES), lambda i: (0, 0)),
                  pl.BlockSpec((LANES, LANES), lambda i: (0, 0))],
        out_specs=[pl.BlockSpec((tm, wd * LANES), lambda i: (i, 0)) for wd, _ in outs],
        out_shape=[jax.ShapeDtypeStruct((n, wd * LANES), dt) for wd, dt in outs],
        scratch_shapes=[pltpu.VMEM((tm, c), F32)],
        compiler_params=_params(("arbitrary",)),
        name="proj_prep",
    )(x, norm.reshape(1, d), w, gains, cos_t, sin_t, jnp.zeros((tm, LANES), F32), _group_sum_matrix(HEAD_DIM))


def _pair_gain(g):
    return jnp.concatenate([g, g]).reshape(1, LANES).astype(F32)


def _rope_tables(seq):
    half = HEAD_DIM // 2
    inv_freq = 1.0 / (ROPE_THETA ** (jnp.arange(0, HEAD_DIM, 2, dtype=F32) / HEAD_DIM))
    ang = jnp.arange(seq, dtype=F32)[:, None] * inv_freq[None, :]
    cos, sin = jnp.cos(ang), jnp.sin(ang)
    cos_t = jnp.concatenate([cos, cos, cos, cos], axis=1)
    sin_t = jnp.concatenate([-sin, sin, -sin, sin], axis=1)
    del half
    return cos_t, sin_t


def _compress_kernel(ch_ref, ptop_ref, pbot_ref, w1t_ref, w1b_ref, w2_ref, gain_ref, o_ref, *, norm):
    ch = ch_ref[0]
    a = _dot((ch + ptop_ref[...]).astype(BF16), w1t_ref[...])
    b = _dot((ch + pbot_ref[...]).astype(BF16), w1b_ref[...])
    nc = a.shape[0]
    hid = a + pltpu.roll(b, nc - 1, 0)
    act = hid * (1.0 / (1.0 + jnp.exp(-hid)))
    out = _dot(act.astype(BF16), w2_ref[...])
    if norm:
        lane = _lane(out.shape)
        low = lane < HEAD_DIM
        o2 = out * out
        s_lo = jnp.sum(jnp.where(low, o2, 0.0), axis=-1, keepdims=True)
        s_hi = jnp.sum(jnp.where(low, 0.0, o2), axis=-1, keepdims=True)
        ms = jnp.where(low, s_lo, s_hi) * (1.0 / HEAD_DIM)
        out = (out * lax.rsqrt(ms + RMS_EPS)) * gain_ref[...]
    o_ref[0] = out.astype(o_ref.dtype)


def _compress(t_pair, pos_emb, w1, w2, gain):
    b, s, _ = t_pair.shape
    nc = s // CMP_STRIDE
    hid = w1.shape[1]
    ch = t_pair.reshape(b, nc, CMP_STRIDE * LANES)
    eye2 = jnp.eye(2, dtype=F32)
    w1r = w1.reshape(CMP_BLOCK, HEAD_DIM, hid)
    def expand_w1(w):
        return jnp.einsum('pdj,kl->pkdlj', w, eye2).reshape(CMP_STRIDE * LANES, 2 * hid).astype(BF16)
    w1t, w1b = expand_w1(w1r[:CMP_STRIDE]), expand_w1(w1r[CMP_STRIDE:])
    w2e = jnp.einsum('jd,kl->kjld', w2, eye2).reshape(2 * hid, LANES).astype(BF16)
    def expand_pos(p):
        return jnp.broadcast_to(p[:, None, :], (CMP_STRIDE, 2, HEAD_DIM)).reshape(1, CMP_STRIDE * LANES)
    ptop, pbot = expand_pos(pos_emb[:CMP_STRIDE]), expand_pos(pos_emb[CMP_STRIDE:])
    norm = gain is not None
    g = _pair_gain(gain) if norm else jnp.ones((1, LANES), F32)
    full = lambda shp: pl.BlockSpec(shp, lambda i: (0,) * len(shp))
    return pl.pallas_call(
        functools.partial(_compress_kernel, norm=norm),
        grid=(b,),
        in_specs=[pl.BlockSpec((1, nc, CMP_STRIDE * LANES), lambda i: (i, 0, 0)),
                  full((1, CMP_STRIDE * LANES)), full((1, CMP_STRIDE * LANES)),
                  full((CMP_STRIDE * LANES, 2 * hid)), full((CMP_STRIDE * LANES, 2 * hid)),
                  full((2 * hid, LANES)), full((1, LANES))],
        out_specs=pl.BlockSpec((1, nc, LANES), lambda i: (i, 0, 0)),
        out_shape=jax.ShapeDtypeStruct((b, nc, LANES), BF16),
        compiler_params=_params(("arbitrary",)),
        name="nsa_compress",
    )(ch, ptop, pbot, w1t, w1b, w2e, g)


def _stack_heads(q, n):
    return jnp.concatenate([q[:, g * LANES:(g + 1) * LANES] for g in range(n)], axis=0)


def _gate_column(gl, col):
    lane = _lane(gl.shape)
    return jnp.sum(jnp.where(lane == col, gl, 0.0), axis=-1, keepdims=True)


def _sigmoid(x):
    return 1.0 / (1.0 + jnp.exp(-x))


def _compact_group(heads, hk):
    tq = heads[0].shape[0]
    lane = _lane((tq, LANES))
    low = lane < HEAD_DIM
    outs = []
    for e in range(0, len(heads), 2):
        he, ho = heads[e], heads[e + 1]
        ho_r = pltpu.roll(ho, HEAD_DIM, 1)
        if hk is None:
            lo_part, hi_part = he, ho_r
        else:
            at_low = jnp.broadcast_to(hk, (tq, LANES)) == 0
            lo_part = jnp.where(at_low, he, pltpu.roll(he, HEAD_DIM, 1))
            hi_part = jnp.where(at_low, ho_r, ho)
        outs.append(jnp.where(low, lo_part, hi_part))
    return jnp.concatenate(outs, axis=1)


def _nsa_cmp_kernel(gb_ref, q_ref, k_ref, v_ref, ov_ref, gl_ref, o_ref, sel_ref, *, tq, group, n_sel, ns, q_off):
    hk = pl.program_id(1)
    i = pl.program_id(2) + q_off
    q4 = _stack_heads(q_ref[0], group)
    kc = k_ref[0]
    ncp = kc.shape[0]
    logits = _dot_nt(q4, kc).reshape(group, tq, ncp)
    t = i * tq + lax.broadcasted_iota(jnp.int32, (tq, ncp), 0)
    cmp_end = lax.broadcasted_iota(jnp.int32, (tq, ncp), 1) * CMP_STRIDE + (CMP_BLOCK - 1)
    logits = jnp.where((cmp_end <= t)[None], logits, NEG_INF)
    m = jnp.max(logits, axis=-1, keepdims=True)
    e = jnp.exp2(logits - m)
    t_row = i * tq + lax.broadcasted_iota(jnp.int32, (tq, 1), 0)
    seen = jnp.where(t_row >= CMP_BLOCK - 1, 1.0, 0.0)[None]
    inv = seen / jnp.maximum(jnp.sum(e, axis=-1, keepdims=True), 1e-30)
    p = e * inv
    o4 = _dot(p.reshape(group * tq, ncp).astype(BF16), v_ref[0])
    gl = gl_ref[0]
    heads = []
    for g in range(group):
        col = (hk * group + g) * 3
        gate = _sigmoid(_gate_column(gl, col) + gb_ref[col])
        heads.append(o4[g * tq:(g + 1) * tq] * gate)
    o_ref[0] = _compact_group(heads, hk)

    ps = jnp.sum(p, axis=0)
    ps_hi = ps.astype(BF16)
    ps_lo = (ps - ps_hi.astype(F32)).astype(BF16)
    imp = _dot(ps_hi, ov_ref[...]) + _dot(ps_lo, ov_ref[...])
    imp_t = imp.T
    blk = lax.broadcasted_iota(jnp.int32, (LANES, tq), 0)
    cur = (i * tq + lax.broadcasted_iota(jnp.int32, (LANES, tq), 1)) // SLC_BLOCK
    forced = (blk == 0) | (blk == cur) | (blk == cur - 1)
    score = jnp.where(forced, BELOW_ALL, jnp.where(blk <= cur, imp_t, NEG_INF))
    score = jnp.where(blk < ns, score, BELOW_ALL)
    blk_f = blk.astype(F32)

    def pick(_, carry):
        sc, sel = carry
        mx = jnp.max(sc, axis=0, keepdims=True)
        first = jnp.min(jnp.where(sc == mx, blk_f, float(LANES)), axis=0, keepdims=True)
        hit = blk_f == first
        return jnp.where(hit, BELOW_ALL, sc), jnp.where(hit, 1.0, sel)

    _, sel = lax.fori_loop(0, max(n_sel - 3, 0), pick, (score, jnp.where(forced, 1.0, 0.0)))
    sel_ref[0, 0] = jnp.where(blk <= cur, sel, 0.0).T.astype(sel_ref.dtype)


def _nsa_compressed(q_cmp, k_cmp, v_cmp, gates, gate_b, *, seq, tq=256):
    b = q_cmp.shape[0]
    group = NSA_HEADS // NSA_KV_HEADS
    ncp = k_cmp.shape[1]
    ns = seq // SLC_BLOCK
    n_sel = min(N_SELECT, ns)
    c_start = np.arange(ncp)[:, None] * CMP_STRIDE
    s_start = np.arange(LANES)[None, :] * SLC_BLOCK
    overlap = np.maximum(np.minimum(c_start + CMP_BLOCK, s_start + SLC_BLOCK) - np.maximum(c_start, s_start), 0)
    overlap = np.where((np.arange(LANES)[None, :] < ns) & (np.arange(ncp)[:, None] < ncp - 1), overlap, 0)
    overlap = jnp.asarray(overlap, BF16)
    gw = group * LANES
    tq = min(tq, seq)
    parts = max(1, min(4, seq // (4 * tq)))
    seg = seq // parts
    o_parts, sel_parts = [], []
    for part in range(parts):
        q_off = part * (seg // tq)
        visible = ((part + 1) * seg - CMP_BLOCK) // CMP_STRIDE + 1
        nck = min(ncp, -(-visible // LANES) * LANES)
        o_seg, sel_seg = pl.pallas_call(
            functools.partial(_nsa_cmp_kernel, tq=tq, group=group, n_sel=n_sel, ns=ns, q_off=q_off),
            grid=(b, NSA_KV_HEADS, seg // tq),
            in_specs=[pl.BlockSpec(memory_space=pltpu.SMEM),
                      pl.BlockSpec((1, tq, gw), lambda bi, h, i, q_off=q_off: (bi, i + q_off, h)),
                      pl.BlockSpec((1, nck, LANES), lambda bi, h, i: (bi, 0, 0)),
                      pl.BlockSpec((1, nck, LANES), lambda bi, h, i: (bi, 0, 0)),
                      pl.BlockSpec((nck, LANES), lambda bi, h, i: (0, 0)),
                      pl.BlockSpec((1, tq, LANES), lambda bi, h, i, q_off=q_off: (bi, i + q_off, 0))],
            out_specs=[pl.BlockSpec((1, tq, group * HEAD_DIM), lambda bi, h, i: (bi, i, h)),
                       pl.BlockSpec((1, 1, tq, LANES), lambda bi, h, i: (bi, h, i, 0))],
            out_shape=[jax.ShapeDtypeStruct((b, seg, NSA_HEADS * HEAD_DIM), F32),
                       jax.ShapeDtypeStruct((b, NSA_KV_HEADS, seg, LANES), BF16)],
            compiler_params=_params(("arbitrary", "arbitrary", "arbitrary")),
            name="nsa_compressed_select",
        )(gate_b, q_cmp, k_cmp, v_cmp, overlap, gates)
        o_parts.append(o_seg)
        sel_parts.append(sel_seg)
    return jnp.concatenate(o_parts, axis=1), jnp.concatenate(sel_parts, axis=2)


def _nsa_slc_kernel(gb_ref, q_ref, k_ref, v_ref, sel_ref, gl_ref, o_ref, sa_ref, sb_ref, *, tq, tk, group):
    hk = pl.program_id(1)
    i = pl.program_id(2)
    q = q_ref[0]
    unsel = (sel_ref[0, 0].astype(F32) - 1.0).astype(BF16)
    lhs = [jnp.concatenate([jnp.concatenate([q[:, g * LANES:(g + 1) * LANES], unsel], axis=1)
                            for g in (2 * c, 2 * c + 1)], axis=0) for c in range(group // 2)]
    n_full = (i * tq) // tk
    chains = group // 2

    def logits_into(buf, j):
        off = pl.multiple_of(j * tk, tk)
        kt = k_ref[0, pl.ds(off, tk), :]
        for c in range(chains):
            buf[c] = _dot_nt(lhs[c], kt)

    def consume(buf, j, states, masked):
        off = pl.multiple_of(j * tk, tk)
        vt = v_ref[0, pl.ds(off, tk), :]
        out = []
        for c in range(chains):
            m, acc = states[c]
            s = buf[c]
            if masked:
                t = i * tq + lax.broadcasted_iota(jnp.int32, (tq, tk), 0)
                key = j * tk + lax.broadcasted_iota(jnp.int32, (tq, tk), 1)
                ok = key <= t
                s = jnp.where(jnp.concatenate([ok, ok], axis=0), s, NEG_INF)
            m_new = jnp.maximum(m, jnp.max(s, axis=-1, keepdims=True))
            p = jnp.exp2(s - m_new)
            out.append((m_new, jnp.exp2(m - m_new) * acc + _dot(p.astype(BF16), vt)))
        return tuple(out)

    def run(j, states, steps):
        bufs = (sa_ref, sb_ref)
        for n in range(steps):
            logits_into(bufs[(n + 1) % 2], j + n + 1)
            states = consume(bufs[n % 2], j + n, states, False)
        return states

    init = tuple((jnp.full((2 * tq, 1), NEG_INF, F32), jnp.zeros((2 * tq, LANES), F32)) for _ in range(chains))
    logits_into(sa_ref, 0)
    states = lax.fori_loop(0, n_full // 4, lambda jj, st: run(4 * jj, st, 4), init)
    states = lax.fori_loop(0, (n_full % 4) // 2, lambda jj, st: run(4 * (n_full // 4), st, 2), states)
    r = 2 * (n_full // 2)

    def diagonal_is_next(states):
        return consume(sa_ref, r, states, True)

    def one_full_tile_left(states):
        logits_into(sb_ref, r + 1)
        return consume(sb_ref, r + 1, consume(sa_ref, r, states, False), True)

    carry = lax.cond(r == n_full, diagonal_is_next, one_full_tile_left, states)
    gl = gl_ref[0]
    heads = []
    for g in range(group):
        acc = carry[g // 2][1][(g % 2) * tq:(g % 2 + 1) * tq]
        colg = (hk * group + g) * 3 + 1
        gate = _sigmoid(_gate_column(gl, colg) + gb_ref[colg])
        heads.append(acc * (gate / acc[:, HEAD_DIM:HEAD_DIM + 1]))
    o_ref[0] = _compact_group(heads, None)


def _nsa_selected(q_rot, k_aug, v_exp, sel, gates, gate_b, *, seq, tq=256, tk=512):
    b = q_rot.shape[0]
    group = NSA_HEADS // NSA_KV_HEADS
    tk = min(tk, seq)
    gw = group * LANES
    return pl.pallas_call(
        functools.partial(_nsa_slc_kernel, tq=tq, tk=tk, group=group),
        grid=(b, NSA_KV_HEADS, seq // tq),
        in_specs=[pl.BlockSpec(memory_space=pltpu.SMEM),
                  pl.BlockSpec((1, tq, gw), lambda bi, h, i: (bi, i, h)),
                  pl.BlockSpec((1, seq, 2 * LANES), lambda bi, h, i: (bi, 0, 0)),
                  pl.BlockSpec((1, seq, LANES), lambda bi, h, i: (bi, 0, h)),
                  pl.BlockSpec((1, 1, tq, LANES), lambda bi, h, i: (bi, h, i, 0)),
                  pl.BlockSpec((1, tq, LANES), lambda bi, h, i: (bi, i, 0))],
        out_specs=pl.BlockSpec((1, tq, group * HEAD_DIM), lambda bi, h, i: (bi, i, h)),
        out_shape=jax.ShapeDtypeStruct((b, seq, NSA_HEADS * HEAD_DIM), F32),
        scratch_shapes=[pltpu.VMEM((group // 2, 2 * tq, tk), F32), pltpu.VMEM((group // 2, 2 * tq, tk), F32)],
        compiler_params=_params(("arbitrary", "arbitrary", "arbitrary")),
        name="nsa_selected",
    )(gate_b, q_rot, k_aug, v_exp, sel, gates)


def _window_kernel(sc_ref, q_ref, k_ref, v_ref, gl_ref, o_ref, sa_ref, sb_ref, *, tq, nt, span, window, group,
                   gated, sinks):
    hk = pl.program_id(1)
    sblk = pl.program_id(2)
    chains = group // 2

    def tile(it):
        gi = sblk * nt + it
        start = pl.multiple_of(jnp.maximum(gi * tq + tq - span, 0), int(np.gcd(tq, span)))
        return gi, start, pl.ds(pl.multiple_of(it * tq, tq), tq)

    def logits_into(buf, it):
        _, start, rows = tile(it)
        kt = k_ref[0, pl.ds(start, span), :]
        for c in range(chains):
            lhs = jnp.concatenate([q_ref[0, rows, g * LANES:(g + 1) * LANES] for g in (2 * c, 2 * c + 1)], axis=0)
            buf[c] = _dot_nt(lhs, kt)

    def consume(buf, it):
        gi, start, rows = tile(it)
        vt = v_ref[0, pl.ds(start, span), :]
        t = gi * tq + lax.broadcasted_iota(jnp.int32, (tq, span), 0)
        key = start + lax.broadcasted_iota(jnp.int32, (tq, span), 1)
        ok = (key <= t) & (t - key < window)
        ok = jnp.concatenate([ok, ok], axis=0)
        gl = gl_ref[0, rows, :]
        heads = []
        for c in range(chains):
            s = jnp.where(ok, buf[c], NEG_INF)
            m = jnp.max(s, axis=-1, keepdims=True)
            if sinks:
                sk = jnp.concatenate([jnp.full((tq, 1), sc_ref[hk * group + g] * LOG2E, F32)
                                      for g in (2 * c, 2 * c + 1)], axis=0)
                m = jnp.maximum(m, sk)
            acc = _dot(jnp.exp2(s - m).astype(BF16), vt)
            denom = acc[:, HEAD_DIM:HEAD_DIM + 1]
            if sinks:
                denom = denom + jnp.exp2(sk - m)
            for r in range(2):
                g = 2 * c + r
                scale = 1.0 / denom[r * tq:(r + 1) * tq]
                if gated:
                    colg = (hk * group + g) * 3 + 2
                    scale = scale * _sigmoid(_gate_column(gl, colg) + sc_ref[colg])
                heads.append(acc[r * tq:(r + 1) * tq] * scale)
        o_ref[0, rows, :] = _compact_group(heads, None)

    logits_into(sa_ref, 0)

    def quad(jj, carry):
        bufs = (sa_ref, sb_ref)
        for n in range(4):
            it = 4 * jj + n
            logits_into(bufs[(n + 1) % 2], jnp.minimum(it + 1, nt - 1))
            consume(bufs[n % 2], it)
        return carry

    lax.fori_loop(0, nt // 4, quad, 0)


def _window_attention(q_rot, q_blk0, k_pairs, kv_blk, v_heads, scalars, gates, *, seq, window, gated, sinks,
                      tq=128, rows=2048):
    b = q_rot.shape[0]
    group = 4
    span = min(window + tq, seq)
    rows = min(rows, seq)
    nt = rows // tq
    assert nt % 4 == 0
    gw = group * LANES
    qb = q_blk0 // group
    return pl.pallas_call(
        functools.partial(_window_kernel, tq=tq, nt=nt, span=span, window=window, group=group, gated=gated,
                          sinks=sinks),
        grid=(b, 2, seq // rows),
        in_specs=[pl.BlockSpec(memory_space=pltpu.SMEM),
                  pl.BlockSpec((1, rows, gw), lambda bi, h, i: (bi, i, qb + h)),
                  pl.BlockSpec((1, seq, LANES), lambda bi, h, i: (bi, 0, kv_blk)),
                  pl.BlockSpec((1, seq, LANES), lambda bi, h, i: (bi, 0, 2 * kv_blk + h)),
                  pl.BlockSpec((1, rows, LANES), lambda bi, h, i: (bi, i, 0))],
        out_specs=pl.BlockSpec((1, rows, group * HEAD_DIM), lambda bi, h, i: (bi, i, h)),
        out_shape=jax.ShapeDtypeStruct((b, seq, 8 * HEAD_DIM), F32),
        scratch_shapes=[pltpu.VMEM((group // 2, 2 * tq, span), F32), pltpu.VMEM((group // 2, 2 * tq, span), F32)],
        compiler_params=_params(("arbitrary", "arbitrary", "arbitrary")),
        name="window_attention",
    )(scalars, q_rot, k_pairs, v_heads, gates)


def _dense_kernel(q_ref, k_ref, v_ref, o_ref, sa_ref, sb_ref, *, tq, tk, nh):
    i = pl.program_id(2)
    t0 = i * tq
    n_full = t0 // tk
    qs = [q_ref[0][:, e * LANES:(e + 1) * LANES] for e in range(nh)]

    def logits_into(buf, j):
        off = pl.multiple_of(j * tk, tk)
        for e in range(nh):
            buf[e] = _dot_nt(qs[e], k_ref[0, pl.ds(off, tk), e * LANES:(e + 1) * LANES])

    def consume(buf, j, states, masked):
        off = pl.multiple_of(j * tk, tk)
        out = []
        for e in range(nh):
            m, acc = states[e]
            s = buf[e]
            if masked:
                t = t0 + lax.broadcasted_iota(jnp.int32, (tq, tk), 0)
                key = j * tk + lax.broadcasted_iota(jnp.int32, (tq, tk), 1)
                s = jnp.where(key <= t, s, NEG_INF)
            vt = v_ref[0, pl.ds(off, tk), e * LANES:(e + 1) * LANES]
            m_new = jnp.maximum(m, jnp.max(s, axis=-1, keepdims=True))
            p = jnp.exp2(s - m_new)
            out.append((m_new, jnp.exp2(m - m_new) * acc + _dot(p.astype(BF16), vt)))
        return tuple(out)

    def run(j, states, steps):
        bufs = (sa_ref, sb_ref)
        for n in range(steps):
            logits_into(bufs[(n + 1) % 2], j + n + 1)
            states = consume(bufs[n % 2], j + n, states, False)
        return states

    init = tuple((jnp.full((tq, 1), NEG_INF, F32), jnp.zeros((tq, LANES), F32)) for _ in range(nh))
    logits_into(sa_ref, 0)
    states = lax.fori_loop(0, n_full // 4, lambda jj, st: run(4 * jj, st, 4), init)
    states = lax.fori_loop(0, (n_full % 4) // 2, lambda jj, st: run(4 * (n_full // 4), st, 2), states)
    r = 2 * (n_full // 2)

    def diagonal_is_next(states):
        return consume(sa_ref, r, states, True)

    def one_full_tile_left(states):
        logits_into(sb_ref, r + 1)
        return consume(sb_ref, r + 1, consume(sa_ref, r, states, False), True)

    states = lax.cond(r == n_full, diagonal_is_next, one_full_tile_left, states)
    outs = [acc * (1.0 / acc[:, HEAD_DIM:HEAD_DIM + 1]) for _, acc in states]
    lane = _lane((tq, LANES))
    o_ref[0] = jnp.concatenate([jnp.where(lane < HEAD_DIM, outs[e], pltpu.roll(outs[e + 1], HEAD_DIM, 1))
                                for e in range(0, nh, 2)], axis=1)


def _dense_attention(q, k, v, *, seq, tq=512, tk=512, nh=2):
    b = q.shape[0]
    heads = q.shape[2] // LANES
    tk = min(tk, seq)
    tq = min(tq, tk)
    return pl.pallas_call(
        functools.partial(_dense_kernel, tq=tq, tk=tk, nh=nh),
        grid=(b, heads // nh, seq // tq),
        in_specs=[pl.BlockSpec((1, tq, nh * LANES), lambda bi, p, i: (bi, i, p)),
                  pl.BlockSpec((1, seq, nh * LANES), lambda bi, p, i: (bi, 0, p)),
                  pl.BlockSpec((1, seq, nh * LANES), lambda bi, p, i: (bi, 0, p))],
        out_specs=pl.BlockSpec((1, tq, nh * HEAD_DIM), lambda bi, p, i: (bi, i, p)),
        out_shape=jax.ShapeDtypeStruct((b, seq, heads * HEAD_DIM), F32),
        scratch_shapes=[pltpu.VMEM((nh, tq, tk), F32), pltpu.VMEM((nh, tq, tk), F32)],
        compiler_params=_params(("arbitrary", "arbitrary", "arbitrary")),
        name="dense_causal_attention",
    )(q, k, v)


def _decay_kernel(f_ref, b_ref, o_ref):
    x = f_ref[0] + b_ref[...]
    lf = jnp.minimum(x, 0.0) - jnp.log1p(jnp.exp(-jnp.abs(x)))
    n = lf.shape[-1]
    lane = _lane(lf.shape)
    d = 1
    while d < n:
        lf = lf + jnp.where(lane >= d, pltpu.roll(lf, d, 1), 0.0)
        d *= 2
    o_ref[0] = lf * LOG2E


def _decay_cumsum(f_t, bias):
    b, h, s = f_t.shape
    return pl.pallas_call(
        _decay_kernel,
        grid=(b,),
        in_specs=[pl.BlockSpec((1, h, s), lambda i: (i, 0, 0)), pl.BlockSpec((h, 1), lambda i: (0, 0))],
        out_specs=pl.BlockSpec((1, h, s), lambda i: (i, 0, 0)),
        out_shape=jax.ShapeDtypeStruct((b, h, s), F32),
        compiler_params=_params(("arbitrary",)),
        name="fox_decay_cumsum",
    )(f_t, bias.reshape(h, 1).astype(F32))


def _mla_prep_kernel(cq_ref, ckv_ref, misc_ref, gqa_ref, gkva_ref, wq_ref, wk_ref, wv_ref, gq_ref, gk_ref,
                     cos_ref, sin_ref, ones_ref, q_ref, k_ref, v_ref, *, scale):
    tm = cq_ref.shape[0]
    lane = _lane((tm, LANES))
    in_rope = (lane >= MLA_NOPE_DIM) & (lane < MLA_QK_DIM)
    first = lane < MLA_NOPE_DIM + MLA_ROPE_DIM // 2
    cos, sin = cos_ref[...], sin_ref[...]

    def rope_tail(x):
        sw = jnp.where(first, pltpu.roll(x, LANES - MLA_ROPE_DIM // 2, 1), pltpu.roll(x, MLA_ROPE_DIM // 2, 1))
        return x * cos + jnp.where(in_rope, sw, 0.0) * sin

    cq = _rms(cq_ref[...], gqa_ref[...], MLA_Q_RANK).astype(BF16)
    ckv = _rms(ckv_ref[...], gkva_ref[...], MLA_KV_RANK).astype(BF16)
    qa = _dot(cq, wq_ref[...])
    ka = _dot(ckv, wk_ref[...])
    k_rope = jnp.where(in_rope, misc_ref[...], 0.0)
    def head_rms(x, gain):
        ms = _split_dot(x * x, ones_ref[...]) * (1.0 / MLA_QK_DIM)
        return (x * lax.rsqrt(ms + RMS_EPS)) * gain

    for h in range(MLA_HEADS):
        qh = head_rms(qa[:, h * LANES:(h + 1) * LANES], gq_ref[...])
        q_ref[:, h * LANES:(h + 1) * LANES] = (rope_tail(qh) * scale).astype(q_ref.dtype)
        kh = head_rms(ka[:, h * LANES:(h + 1) * LANES] + k_rope, gk_ref[...])
        k_ref[:, h * LANES:(h + 1) * LANES] = rope_tail(kh).astype(k_ref.dtype)
    v = _dot(ckv, wv_ref[...])
    ones_col = (_lane(v.shape) & (LANES - 1)) == MLA_V_DIM
    v_ref[...] = jnp.where(ones_col, 1.0, v).astype(v_ref.dtype)


def _mla_prep(c_q, c_kv, misc, q_a_norm, w_q_b, kv_a_norm, w_kv_b, q_norm, k_norm, *, seq, tm=512):
    n = c_q.shape[0]
    h = MLA_HEADS
    pad = LANES - MLA_QK_DIM
    wq = jnp.pad(w_q_b.reshape(MLA_Q_RANK, h, MLA_QK_DIM), ((0, 0), (0, 0), (0, pad)))
    wq = wq.reshape(MLA_Q_RANK, h * LANES).astype(BF16)
    wkv = w_kv_b.reshape(MLA_KV_RANK, h, MLA_NOPE_DIM + MLA_V_DIM)
    wk = jnp.pad(wkv[:, :, :MLA_NOPE_DIM], ((0, 0), (0, 0), (0, LANES - MLA_NOPE_DIM)))
    wk = wk.reshape(MLA_KV_RANK, h * LANES).astype(BF16)
    wv = jnp.pad(wkv[:, :, MLA_NOPE_DIM:], ((0, 0), (0, 0), (0, LANES - MLA_V_DIM)))
    wv = wv.reshape(MLA_KV_RANK, h * LANES).astype(BF16)
    gq = jnp.pad(q_norm, (0, pad)).reshape(1, LANES)
    gk = jnp.pad(k_norm, (0, pad)).reshape(1, LANES)
    half = MLA_ROPE_DIM // 2
    inv_freq = 1.0 / (ROPE_THETA ** (jnp.arange(0, MLA_ROPE_DIM, 2, dtype=F32) / MLA_ROPE_DIM))
    ang = jnp.arange(seq, dtype=F32)[:, None] * inv_freq[None, :]
    cos, sin = jnp.cos(ang), jnp.sin(ang)
    ones = jnp.ones((seq, MLA_NOPE_DIM), F32)
    zeros = jnp.zeros((seq, MLA_NOPE_DIM), F32)
    cos_t = jnp.concatenate([ones, cos, cos, ones[:, :pad]], axis=1)
    sin_t = jnp.concatenate([zeros, -sin, sin, zeros[:, :pad]], axis=1)
    del half
    sblocks = seq // tm
    full = lambda shp: pl.BlockSpec(shp, lambda i: (0,) * len(shp))
    q, k, v = pl.pallas_call(
        functools.partial(_mla_prep_kernel, scale=MLA_QK_DIM ** -0.5 * LOG2E),
        grid=(n // tm,),
        in_specs=[pl.BlockSpec((tm, MLA_Q_RANK), lambda i: (i, 0)),
                  pl.BlockSpec((tm, LANES), lambda i: (i, 0)),
                  pl.BlockSpec((tm, LANES), lambda i: (i, 0)),
                  full((1, MLA_Q_RANK)), full((1, MLA_KV_RANK)),
                  full((MLA_Q_RANK, h * LANES)), full((MLA_KV_RANK, h * LANES)), full((MLA_KV_RANK, h * LANES)),
                  full((1, LANES)), full((1, LANES)),
                  pl.BlockSpec((tm, LANES), lambda i: (i % sblocks, 0)),
                  pl.BlockSpec((tm, LANES), lambda i: (i % sblocks, 0)),
                  full((LANES, LANES))],
        out_specs=[pl.BlockSpec((tm, h * LANES), lambda i: (i, 0)),
                   pl.BlockSpec((tm, h * LANES), lambda i: (i, 0)),
                   pl.BlockSpec((tm, h * LANES), lambda i: (i, 0))],
        out_shape=[jax.ShapeDtypeStruct((n, h * LANES), BF16),
                   jax.ShapeDtypeStruct((n, h * LANES), BF16),
                   jax.ShapeDtypeStruct((n, h * LANES), BF16)],
        compiler_params=_params(("arbitrary",)),
        name="mla_prep",
    )(c_q, c_kv, misc, q_a_norm.reshape(1, -1), kv_a_norm.reshape(1, -1), wq, wk, wv, gq, gk, cos_t, sin_t,
      _group_sum_matrix(LANES))
    return q, k, v


def _cols(w, a, b):
    return w[:, a:b]


def _nsa_swa_mixer(x, batch, seq, mix_norm, w_in, nsa_gate_b, nsa_q_norm, nsa_kc_norm, nsa_ks_norm, nsa_kw_norm,
                   cmp_pos_k, cmp_pos_v, cmpk_w1, cmpk_w2, cmpv_w1, cmpv_w2,
                   swa_q_norm, swa_k_norm, swa_sinks, w_out):
    n = batch * seq
    o = np.cumsum([0, 512, 128, 128, 128, 128, 128, 128, 24, 512, 128, 128])
    seg = lambda j: _cols(w_in, o[j], o[j + 1])
    q_a, kc, vc, ks, vs, kw, vw, gl, q_b, k_b, v_b = [seg(j) for j in range(11)]
    gl = jnp.pad(gl, ((0, 0), (0, LANES - gl.shape[1])))
    w = jnp.concatenate([q_a, q_b, ks, kw, k_b, kc, vc, vs, vw, v_b, gl], axis=1).astype(BF16)
    cos_t, sin_t = _rope_tables(seq)

    s_q = HEAD_DIM ** -0.5 * LOG2E
    gains = jnp.stack([_pair_gain(g) for g in (nsa_q_norm, swa_q_norm, nsa_ks_norm, nsa_kw_norm, swa_k_norm)])
    jobs = [_Job(blk=c, out=0, col=2 * c, gain=c // 4, rope=True, scale=s_q, mode="q", dst=((c % 4) // 2,) * 2)
            for c in range(8)]
    jobs += [_Job(blk=c, out=1, col=2 * c, gain=0, scale=s_q, mode="q", dst=(c // 2,) * 2)
             for c in range(4)]
    jobs += [_Job(blk=8, out=2, col=0, gain=2, rope=True, mode="kaug"),
             _Job(blk=9, out=3, col=0, gain=3, rope=True), _Job(blk=10, out=3, col=1, gain=4, rope=True),
             _Job(blk=13, out=4, col=0, mode="v"),
             _Job(blk=14, out=5, col=0, mode="v"), _Job(blk=15, out=5, col=2, mode="v"),
             _Job(blk=11, out=6, col=0), _Job(blk=12, out=7, col=0), _Job(blk=16, out=8, col=0)]
    outs = [(16, BF16), (8, BF16), (2, BF16), (2, BF16), (2, BF16), (4, BF16), (1, F32), (1, F32), (1, F32)]
    q_rot, q_cmp, k_aug, k_ws, v_slc, v_ws, kc_raw, vc_raw, gates = (
        a.reshape(batch, seq, -1) for a in _proj_prep(x, mix_norm, w, jobs, outs, gains, seq, cos_t, sin_t))

    k_cmp = _compress(kc_raw, cmp_pos_k, cmpk_w1, cmpk_w2, nsa_kc_norm)
    v_cmp = _compress(vc_raw, cmp_pos_v, cmpv_w1, cmpv_w2, None)
    gate_b = nsa_gate_b.astype(F32)

    o_cmp, sel = _nsa_compressed(q_cmp, k_cmp, v_cmp, gates, gate_b, seq=seq)
    o_slc = _nsa_selected(q_rot, k_aug, v_slc, sel, gates, gate_b, seq=seq)
    o_win = _window_attention(q_rot, 0, k_ws, 0, v_ws, gate_b, gates, seq=seq, window=NSA_WINDOW,
                              gated=True, sinks=False)
    o_swa = _window_attention(q_rot, 8, k_ws, 1, v_ws, swa_sinks.astype(F32), gates, seq=seq, window=SWA_WINDOW,
                              gated=False, sinks=True, tq=256)
    flat = lambda a: a.reshape(n, -1)
    return _outproj(x, [flat(o_cmp), flat(o_slc), flat(o_win)], [flat(o_swa)], w_out)


def _fox_mla_mixer(x, batch, seq, mix_norm, w_in, fox_f_bias, fox_q_norm, fox_k_norm, mla_q_a_norm, mla_w_q_b,
                   mla_kv_a_norm, mla_w_kv_b, mla_q_norm, mla_k_norm, w_out):
    n = batch * seq
    o = np.cumsum([0, 512, 512, 512, 8, 256, 128, 32])
    seg = lambda j: _cols(w_in, o[j], o[j + 1])
    q_c, k_c, v_c, f_c, c_q, c_kv, k_r = [seg(j) for j in range(7)]
    d = w_in.shape[0]
    misc = jnp.concatenate([f_c, jnp.zeros((d, MLA_NOPE_DIM - 8), w_in.dtype), k_r,
                            jnp.zeros((d, LANES - MLA_QK_DIM), w_in.dtype)], axis=1)
    w = jnp.concatenate([q_c, k_c, v_c, c_q, c_kv, misc], axis=1).astype(BF16)
    cos_t, sin_t = _rope_tables(seq)
    gains = jnp.stack([_pair_gain(fox_q_norm), _pair_gain(fox_k_norm)])
    jobs = [_Job(blk=c, out=0, col=2 * c, gain=0, scale=HEAD_DIM ** -0.5 * LOG2E, mode="q", aug="ones")
            for c in range(4)]
    jobs += [_Job(blk=8 + c, out=1, col=2 * c, mode="v") for c in range(4)]
    jobs += [_Job(blk=4 + c, out=2, col=c) for c in range(4)]
    jobs += [_Job(blk=12, out=3, col=0), _Job(blk=13, out=3, col=1), _Job(blk=14, out=4, col=0),
             _Job(blk=15, out=5, col=0)]
    outs = [(8, BF16), (8, BF16), (4, F32), (2, F32), (1, F32), (1, F32)]
    q_f, v_f, kc_raw, c_q_out, c_kv_out, misc_out = _proj_prep(x, mix_norm, w, jobs, outs, gains, seq, cos_t, sin_t)

    f_t = misc_out[:, :FOX_HEADS].reshape(batch, seq, FOX_HEADS).transpose(0, 2, 1)
    dc = _decay_cumsum(f_t, fox_f_bias)
    dc_tok = jnp.pad(dc.transpose(0, 2, 1).reshape(n, FOX_HEADS), ((0, 0), (0, LANES - FOX_HEADS)))
    kjobs = [_Job(blk=c, out=0, col=2 * c, gain=1, mode="q", aug="decay", heads=(2 * c, 2 * c + 1))
             for c in range(4)]
    (k_f,) = _prep(kc_raw, kjobs, [(8, BF16)], gains, seq, cos_t, sin_t, aux=dc_tok)
    b3 = lambda a: a.reshape(batch, seq, -1)
    o_fox = _dense_attention(b3(q_f), b3(k_f), b3(v_f), seq=seq)

    q_m, k_m, v_m = _mla_prep(c_q_out, c_kv_out, misc_out, mla_q_a_norm, mla_w_q_b, mla_kv_a_norm, mla_w_kv_b,
                              mla_q_norm, mla_k_norm, seq=seq)
    o_mla = _dense_attention(b3(q_m), b3(k_m), b3(v_m), seq=seq)
    flat = lambda a: a.reshape(n, -1)
    return _outproj(x, [flat(o_fox)], [flat(o_mla)], w_out)


def kernel(x, l0_ffn1_norm, l0_ffn1_w_gate, l0_ffn1_w_up, l0_ffn1_w_down, l0_mix_norm, l0_w_in, l0_nsa_gate_b, l0_nsa_q_norm, l0_nsa_kc_norm, l0_nsa_ks_norm, l0_nsa_kw_norm, l0_cmp_pos_k, l0_cmp_pos_v, l0_cmpk_w1, l0_cmpk_w2, l0_cmpv_w1, l0_cmpv_w2, l0_swa_q_norm, l0_swa_k_norm, l0_swa_sinks, l0_w_out, l0_ffn2_norm, l0_ffn2_w_gate, l0_ffn2_w_up, l0_ffn2_w_down, l1_ffn1_norm, l1_ffn1_w_gate, l1_ffn1_w_up, l1_ffn1_w_down, l1_mix_norm, l1_w_in, l1_fox_f_bias, l1_fox_q_norm, l1_fox_k_norm, l1_mla_q_a_norm, l1_mla_w_q_b, l1_mla_kv_a_norm, l1_mla_w_kv_b, l1_mla_q_norm, l1_mla_k_norm, l1_w_out, l1_ffn2_norm, l1_ffn2_w_gate, l1_ffn2_w_up, l1_ffn2_w_down):
    batch, seq, d = x.shape
    h = x.reshape(batch * seq, d)
    h = _ffn(h, l0_ffn1_norm, l0_ffn1_w_gate, l0_ffn1_w_up, l0_ffn1_w_down)
    h = _nsa_swa_mixer(h, batch, seq, l0_mix_norm, l0_w_in, l0_nsa_gate_b, l0_nsa_q_norm, l0_nsa_kc_norm,
                       l0_nsa_ks_norm, l0_nsa_kw_norm, l0_cmp_pos_k, l0_cmp_pos_v, l0_cmpk_w1, l0_cmpk_w2,
                       l0_cmpv_w1, l0_cmpv_w2, l0_swa_q_norm, l0_swa_k_norm, l0_swa_sinks, l0_w_out)
    h = _ffn(h, l0_ffn2_norm, l0_ffn2_w_gate, l0_ffn2_w_up, l0_ffn2_w_down)
    h = _ffn(h, l1_ffn1_norm, l1_ffn1_w_gate, l1_ffn1_w_up, l1_ffn1_w_down)
    h = _fox_mla_mixer(h, batch, seq, l1_mix_norm, l1_w_in, l1_fox_f_bias, l1_fox_q_norm, l1_fox_k_norm,
                       l1_mla_q_a_norm, l1_mla_w_q_b, l1_mla_kv_a_norm, l1_mla_w_kv_b, l1_mla_q_norm,
                       l1_mla_k_norm, l1_w_out)
    h = _ffn(h, l1_ffn2_norm, l1_ffn2_w_gate, l1_ffn2_w_up, l1_ffn2_w_down)
    return h.reshape(batch, seq, d)
```

```python
import functools
from typing import NamedTuple, Optional

import numpy as np
import jax
import jax.numpy as jnp
from jax import lax
from jax.experimental import pallas as pl
from jax.experimental.pallas import tpu as pltpu

F32 = jnp.float32
BF16 = jnp.bfloat16

HEAD_DIM = 64
LANES = 128
ROPE_THETA = 10000.0
RMS_EPS = 1e-6
NEG_INF = -1e30
FORCE_SCORE = 1e9
BELOW_ALL = -3e38
LOG2E = 1.4426950408889634
BIG = 1e30
VT_ROWS = 80

NSA_HEADS = 8
NSA_KV_HEADS = 2
CMP_BLOCK = 32
CMP_STRIDE = 16
CMP_HIDDEN = 256
SLC_BLOCK = 64
N_SELECT = 16
NSA_WINDOW = 512
SWA_HEADS = 8
SWA_KV_HEADS = 2
SWA_WINDOW = 128
FOX_HEADS = 8
MLA_HEADS = 8
MLA_Q_RANK = 256
MLA_KV_RANK = 128
MLA_NOPE_DIM = 64
MLA_ROPE_DIM = 32
MLA_V_DIM = 64
MLA_QK_DIM = MLA_NOPE_DIM + MLA_ROPE_DIM

VMEM_LIMIT = 48 * 1024 * 1024

NT_DIMS = (((1,), (1,)), ((), ()))


def _params(sem):
    return pltpu.CompilerParams(dimension_semantics=sem, vmem_limit_bytes=VMEM_LIMIT)


def _dot(a, b):
    return jnp.dot(a, b, preferred_element_type=F32)


def _dot_nt(a, b):
    return lax.dot_general(a, b, NT_DIMS, preferred_element_type=F32)


def _rms(x, gain, n):
    ms = jnp.sum(x * x, axis=-1, keepdims=True) * (1.0 / n)
    return (x * lax.rsqrt(ms + RMS_EPS)) * gain


def _lane(shape):
    return lax.broadcasted_iota(jnp.int32, shape, len(shape) - 1)


def _split_dot(x, m):
    hi = x.astype(BF16)
    lo = (x - hi.astype(F32)).astype(BF16)
    return _dot(hi, m) + _dot(lo, m)


def _group_sum_matrix(width):
    lane = np.arange(LANES)
    return jnp.asarray(lane[:, None] // width == lane[None, :] // width, BF16)


def _ffn_kernel(x_ref, g_ref, wg_ref, wu_ref, wd_ref, o_ref, h_sc, acc_sc, *, tf):
    x = x_ref[...]
    h_sc[...] = _rms(x, g_ref[...], x.shape[-1]).astype(BF16)
    acc_sc[...] = jnp.zeros_like(acc_sc)
    f = wg_ref.shape[1]
    for c0 in range(0, f, tf):
        cols = slice(c0, min(c0 + tf, f))
        h = h_sc[...]
        g = _dot(h, wg_ref[:, cols])
        u = _dot(h, wu_ref[:, cols])
        a = (g * (1.0 / (1.0 + jnp.exp(-g)))) * u
        acc_sc[...] += _dot(a.astype(BF16), wd_ref[cols, :])
    o_ref[...] = x + 0.5 * acc_sc[...]


def _ffn(x, norm, w_gate, w_up, w_down, *, tm=512, tf=256):
    n, d = x.shape
    f = w_gate.shape[1]
    wg, wu, wd = w_gate.astype(BF16), w_up.astype(BF16), w_down.astype(BF16)
    wspec = lambda shp: pl.BlockSpec(shp, lambda i: (0, 0), pipeline_mode=pl.Buffered(1))
    return pl.pallas_call(
        functools.partial(_ffn_kernel, tf=tf),
        grid=(n // tm,),
        in_specs=[pl.BlockSpec((tm, d), lambda i: (i, 0)),
                  pl.BlockSpec((1, d), lambda i: (0, 0)),
                  wspec((d, f)), wspec((d, f)), wspec((f, d))],
        out_specs=pl.BlockSpec((tm, d), lambda i: (i, 0)),
        out_shape=jax.ShapeDtypeStruct((n, d), F32),
        scratch_shapes=[pltpu.VMEM((tm, d), BF16), pltpu.VMEM((tm, d), F32)],
        compiler_params=_params(("arbitrary",)),
        name="ffn",
    )(x, norm.reshape(1, d), wg, wu, wd)


def _outproj_kernel(*refs, n_a, n_b):
    x_ref = refs[0]
    a_refs = refs[1:1 + n_a]
    b_refs = refs[1 + n_a:1 + n_a + n_b]
    wa_ref, wb_ref, o_ref = refs[1 + n_a + n_b:]
    a = a_refs[0][...]
    for r in a_refs[1:]:
        a = a + r[...]
    b = b_refs[0][...]
    for r in b_refs[1:]:
        b = b + r[...]
    o_ref[...] = x_ref[...] + _dot(a.astype(BF16), wa_ref[...]) + _dot(b.astype(BF16), wb_ref[...])


def _outproj(x, a_list, b_list, w_out, *, tm=512):
    n, d = x.shape
    ca = a_list[0].shape[1]
    cb = b_list[0].shape[1]
    wa = w_out[:ca].astype(BF16)
    wb = w_out[ca:].astype(BF16)
    row = lambda c: pl.BlockSpec((tm, c), lambda i: (i, 0))
    return pl.pallas_call(
        functools.partial(_outproj_kernel, n_a=len(a_list), n_b=len(b_list)),
        grid=(n // tm,),
        in_specs=[row(d)] + [row(ca)] * len(a_list) + [row(cb)] * len(b_list)
                 + [pl.BlockSpec((ca, d), lambda i: (0, 0)), pl.BlockSpec((cb, d), lambda i: (0, 0))],
        out_specs=row(d),
        out_shape=jax.ShapeDtypeStruct((n, d), F32),
        compiler_params=_params(("arbitrary",)),
        name="outproj",
    )(x, *a_list, *b_list, wa, wb)


class _Job(NamedTuple):
    blk: int
    out: int
    col: int
    gain: Optional[int] = None
    rope: bool = False
    scale: float = 1.0
    mode: str = "plain"
    dst: tuple = (0, 0)
    aug: Optional[str] = None
    heads: tuple = (0, 0)


def _prep_kernel(y_ref, gain_ref, cos_ref, sin_ref, aux_ref, hsum_ref, *o_refs, jobs, seq):
    tm = y_ref.shape[0]
    lane = _lane((tm, LANES))
    low = lane < HEAD_DIM
    zero = jnp.zeros((tm, LANES), F32)
    for job in jobs:
        x = y_ref[:, job.blk * LANES:(job.blk + 1) * LANES]
        if job.gain is not None:
            ms = _split_dot(x * x, hsum_ref[...]) * (1.0 / HEAD_DIM)
            x = (x * lax.rsqrt(ms + RMS_EPS)) * gain_ref[job.gain]
        if job.rope:
            swapped = jnp.where((lane & (HEAD_DIM - 1)) < HEAD_DIM // 2,
                                pltpu.roll(x, LANES - HEAD_DIM // 2, 1), pltpu.roll(x, HEAD_DIM // 2, 1))
            x = x * cos_ref[...] + swapped * sin_ref[...]
        if job.scale != 1.0:
            x = x * job.scale
        if job.mode == "plain":
            pieces = [x]
        elif job.mode == "kaug":
            pos = (pl.program_id(0) * tm + lax.broadcasted_iota(jnp.int32, (tm, LANES), 0)) % seq
            pieces = [x, jnp.where(lane == pos // SLC_BLOCK, BIG, 0.0)]
        elif job.mode == "v":
            r = pltpu.roll(x, HEAD_DIM, 1)
            tail = jnp.where(lane == HEAD_DIM, 1.0, 0.0)
            pieces = [jnp.where(low, x, tail), jnp.where(low, r, tail)]
        else:
            r = pltpu.roll(x, HEAD_DIM, 1)
            h_even = jnp.where(low, x, zero) if job.dst[0] == 0 else jnp.where(low, zero, r)
            h_odd = jnp.where(low, r, zero) if job.dst[1] == 0 else jnp.where(low, zero, x)
            if job.aug is not None:
                tails = []
                for e in range(2):
                    if job.aug == "ones":
                        tails.append(jnp.where((lane >= HEAD_DIM) & (lane < HEAD_DIM + 3), 1.0, 0.0))
                    else:
                        d = jnp.sum(jnp.where(lane == job.heads[e], aux_ref[...], 0.0), axis=-1, keepdims=True)
                        hi = d.astype(BF16).astype(F32)
                        mid = (d - hi).astype(BF16).astype(F32)
                        lo = d - hi - mid
                        tails.append(jnp.where(lane == HEAD_DIM, -hi, jnp.where(lane == HEAD_DIM + 1, -mid,
                                     jnp.where(lane == HEAD_DIM + 2, -lo, 0.0))))
                h_even = jnp.where(low, h_even, tails[0])
                h_odd = jnp.where(low, h_odd, tails[1])
            pieces = [h_even, h_odd]
        o_ref = o_refs[job.out]
        for n, piece in enumerate(pieces):
            o_ref[:, (job.col + n) * LANES:(job.col + n + 1) * LANES] = piece.astype(o_ref.dtype)


def _prep(y, jobs, outs, gains, seq, cos_t, sin_t, *, aux=None, tm=512):
    n, c = y.shape
    aux_spec = pl.BlockSpec((tm, LANES), lambda i: (i, 0))
    if aux is None:
        aux, aux_spec = jnp.zeros((tm, LANES), F32), pl.BlockSpec((tm, LANES), lambda i: (0, 0))
    sblocks = seq // tm
    return pl.pallas_call(
        functools.partial(_prep_kernel, jobs=tuple(jobs), seq=seq),
        grid=(n // tm,),
        in_specs=[pl.BlockSpec((tm, c), lambda i: (i, 0)),
                  pl.BlockSpec(gains.shape, lambda i: (0, 0, 0)),
                  pl.BlockSpec((tm, LANES), lambda i: (i % sblocks, 0)),
                  pl.BlockSpec((tm, LANES), lambda i: (i % sblocks, 0)),
                  aux_spec,
                  pl.BlockSpec((LANES, LANES), lambda i: (0, 0))],
        out_specs=[pl.BlockSpec((tm, w * LANES), lambda i: (i, 0)) for w, _ in outs],
        out_shape=[jax.ShapeDtypeStruct((n, w * LANES), dt) for w, dt in outs],
        compiler_params=_params(("arbitrary",)),
        name="head_prep",
    )(y, gains, cos_t, sin_t, aux, _group_sum_matrix(HEAD_DIM))


def _proj_prep_kernel(x_ref, g_ref, w_ref, gain_ref, cos_ref, sin_ref, aux_ref, hsum_ref, *rest, jobs, seq):
    *o_refs, y_sc = rest
    x = x_ref[...]
    y_sc[...] = _dot(_rms(x, g_ref[...], x.shape[-1]).astype(BF16), w_ref[...])
    _prep_kernel(y_sc, gain_ref, cos_ref, sin_ref, aux_ref, hsum_ref, *o_refs, jobs=jobs, seq=seq)


def _proj_prep(x, norm, w, jobs, outs, gains, seq, cos_t, sin_t, *, tm=512):
    n, d = x.shape
    c = w.shape[1]
    sblocks = seq // tm
    return pl.pallas_call(
        functools.partial(_proj_prep_kernel, jobs=tuple(jobs), seq=seq),
        grid=(n // tm,),
        in_specs=[pl.BlockSpec((tm, d), lambda i: (i, 0)),
                  pl.BlockSpec((1, d), lambda i: (0, 0)),
                  pl.BlockSpec((d, c), lambda i: (0, 0), pipeline_mode=pl.Buffered(1)),
                  pl.BlockSpec(gains.shape, lambda i: (0, 0, 0)),
                  pl.BlockSpec((tm, LANES), lambda i: (i % sblocks, 0)),
                  pl.BlockSpec((tm, LANES), lambda i: (i % sblocks, 0)),
                  pl.BlockSpec((tm, LANES), lambda i: (0, 0)),
                  pl.BlockSpec((LANES, LANES), lambda i: (0, 0))],
        out_specs=[pl.BlockSpec((tm, wd * LANES), lambda i: (i, 0)) for wd, _ in outs],
        out_shape=[jax.ShapeDtypeStruct((n, wd * LANES), dt) for wd, dt in outs],
        scratch_shapes=[pltpu.VMEM((tm, c), F32)],
        compiler_params=_params(("arbitrary",)),
        name="proj_prep",
    )(x, norm.reshape(1, d), w, gains, cos_t, sin_t, jnp.zeros((tm, LANES), F32), _group_sum_matrix(HEAD_DIM))


def _pair_gain(g):
    return jnp.concatenate([g, g]).reshape(1, LANES).astype(F32)


def _rope_tables(seq):
    half = HEAD_DIM // 2
    inv_freq = 1.0 / (ROPE_THETA ** (jnp.arange(0, HEAD_DIM, 2, dtype=F32) / HEAD_DIM))
    ang = jnp.arange(seq, dtype=F32)[:, None] * inv_freq[None, :]
    cos, sin = jnp.cos(ang), jnp.sin(ang)
    cos_t = jnp.concatenate([cos, cos, cos, cos], axis=1)
    sin_t = jnp.concatenate([-sin, sin, -sin, sin], axis=1)
    del half
    return cos_t, sin_t


def _compress_kernel(ch_ref, ptop_ref, pbot_ref, w1t_ref, w1b_ref, w2_ref, gain_ref, o_ref, *, norm):
    ch = ch_ref[0]
    a = _dot((ch + ptop_ref[...]).astype(BF16), w1t_ref[...])
    b = _dot((ch + pbot_ref[...]).astype(BF16), w1b_ref[...])
    nc = a.shape[0]
    hid = a + pltpu.roll(b, nc - 1, 0)
    act = hid * (1.0 / (1.0 + jnp.exp(-hid)))
    out = _dot(act.astype(BF16), w2_ref[...])
    if norm:
        lane = _lane(out.shape)
        low = lane < HEAD_DIM
        o2 = out * out
        s_lo = jnp.sum(jnp.where(low, o2, 0.0), axis=-1, keepdims=True)
        s_hi = jnp.sum(jnp.where(low, 0.0, o2), axis=-1, keepdims=True)
        ms = jnp.where(low, s_lo, s_hi) * (1.0 / HEAD_DIM)
        out = (out * lax.rsqrt(ms + RMS_EPS)) * gain_ref[...]
    o_ref[0] = out.astype(o_ref.dtype)


def _compress(t_pair, pos_emb, w1, w2, gain):
    b, s, _ = t_pair.shape
    nc = s // CMP_STRIDE
    hid = w1.shape[1]
    ch = t_pair.reshape(b, nc, CMP_STRIDE * LANES)
    eye2 = jnp.eye(2, dtype=F32)
    w1r = w1.reshape(CMP_BLOCK, HEAD_DIM, hid)
    def expand_w1(w):
        return jnp.einsum('pdj,kl->pkdlj', w, eye2).reshape(CMP_STRIDE * LANES, 2 * hid).astype(BF16)
    w1t, w1b = expand_w1(w1r[:CMP_STRIDE]), expand_w1(w1r[CMP_STRIDE:])
    w2e = jnp.einsum('jd,kl->kjld', w2, eye2).reshape(2 * hid, LANES).astype(BF16)
    def expand_pos(p):
        return jnp.broadcast_to(p[:, None, :], (CMP_STRIDE, 2, HEAD_DIM)).reshape(1, CMP_STRIDE * LANES)
    ptop, pbot = expand_pos(pos_emb[:CMP_STRIDE]), expand_pos(pos_emb[CMP_STRIDE:])
    norm = gain is not None
    g = _pair_gain(gain) if norm else jnp.ones((1, LANES), F32)
    full = lambda shp: pl.BlockSpec(shp, lambda i: (0,) * len(shp))
    return pl.pallas_call(
        functools.partial(_compress_kernel, norm=norm),
        grid=(b,),
        in_specs=[pl.BlockSpec((1, nc, CMP_STRIDE * LANES), lambda i: (i, 0, 0)),
                  full((1, CMP_STRIDE * LANES)), full((1, CMP_STRIDE * LANES)),
                  full((CMP_STRIDE * LANES, 2 * hid)), full((CMP_STRIDE * LANES, 2 * hid)),
                  full((2 * hid, LANES)), full((1, LANES))],
        out_specs=pl.BlockSpec((1, nc, LANES), lambda i: (i, 0, 0)),
        out_shape=jax.ShapeDtypeStruct((b, nc, LANES), BF16),
        compiler_params=_params(("arbitrary",)),
        name="nsa_compress",
    )(ch, ptop, pbot, w1t, w1b, w2e, g)


def _stack_heads(q, n):
    return jnp.concatenate([q[:, g * LANES:(g + 1) * LANES] for g in range(n)], axis=0)


def _gate_column(gl, col):
    lane = _lane(gl.shape)
    return jnp.sum(jnp.where(lane == col, gl, 0.0), axis=-1, keepdims=True)


def _sigmoid(x):
    return 1.0 / (1.0 + jnp.exp(-x))


def _compact_group(heads, hk):
    tq = heads[0].shape[0]
    lane = _lane((tq, LANES))
    low = lane < HEAD_DIM
    outs = []
    for e in range(0, len(heads), 2):
        he, ho = heads[e], heads[e + 1]
        ho_r = pltpu.roll(ho, HEAD_DIM, 1)
        if hk is None:
            lo_part, hi_part = he, ho_r
        else:
            at_low = jnp.broadcast_to(hk, (tq, LANES)) == 0
            lo_part = jnp.where(at_low, he, pltpu.roll(he, HEAD_DIM, 1))
            hi_part = jnp.where(at_low, ho_r, ho)
        outs.append(jnp.where(low, lo_part, hi_part))
    return jnp.concatenate(outs, axis=1)


def _nsa_cmp_kernel(gb_ref, q_ref, k_ref, v_ref, ov_ref, gl_ref, o_ref, sel_ref, *, tq, group, n_sel, ns, q_off):
    hk = pl.program_id(1)
    i = pl.program_id(2) + q_off
    q4 = _stack_heads(q_ref[0], group)
    kc = k_ref[0]
    ncp = kc.shape[0]
    logits = _dot_nt(q4, kc).reshape(group, tq, ncp)
    t = i * tq + lax.broadcasted_iota(jnp.int32, (tq, ncp), 0)
    cmp_end = lax.broadcasted_iota(jnp.int32, (tq, ncp), 1) * CMP_STRIDE + (CMP_BLOCK - 1)
    logits = jnp.where((cmp_end <= t)[None], logits, NEG_INF)
    m = jnp.max(logits, axis=-1, keepdims=True)
    e = jnp.exp2(logits - m)
    t_row = i * tq + lax.broadcasted_iota(jnp.int32, (tq, 1), 0)
    seen = jnp.where(t_row >= CMP_BLOCK - 1, 1.0, 0.0)[None]
    inv = seen / jnp.maximum(jnp.sum(e, axis=-1, keepdims=True), 1e-30)
    p = e * inv
    o4 = _dot(p.reshape(group * tq, ncp).astype(BF16), v_ref[0])
    gl = gl_ref[0]
    heads = []
    for g in range(group):
        col = (hk * group + g) * 3
        gate = _sigmoid(_gate_column(gl, col) + gb_ref[col])
        heads.append(o4[g * tq:(g + 1) * tq] * gate)
    o_ref[0] = _compact_group(heads, hk)

    ps = jnp.sum(p, axis=0)
    ps_hi = ps.astype(BF16)
    ps_lo = (ps - ps_hi.astype(F32)).astype(BF16)
    imp = _dot(ps_hi, ov_ref[...]) + _dot(ps_lo, ov_ref[...])
    imp_t = imp.T
    blk = lax.broadcasted_iota(jnp.int32, (LANES, tq), 0)
    cur = (i * tq + lax.broadcasted_iota(jnp.int32, (LANES, tq), 1)) // SLC_BLOCK
    forced = (blk == 0) | (blk == cur) | (blk == cur - 1)
    score = jnp.where(forced, BELOW_ALL, jnp.where(blk <= cur, imp_t, NEG_INF))
    score = jnp.where(blk < ns, score, BELOW_ALL)
    blk_f = blk.astype(F32)

    def pick(_, carry):
        sc, sel = carry
        mx = jnp.max(sc, axis=0, keepdims=True)
        first = jnp.min(jnp.where(sc == mx, blk_f, float(LANES)), axis=0, keepdims=True)
        hit = blk_f == first
        return jnp.where(hit, BELOW_ALL, sc), jnp.where(hit, 1.0, sel)

    _, sel = lax.fori_loop(0, max(n_sel - 3, 0), pick, (score, jnp.where(forced, 1.0, 0.0)))
    sel_ref[0, 0] = jnp.where(blk <= cur, sel, 0.0).T.astype(sel_ref.dtype)


def _nsa_compressed(q_cmp, k_cmp, v_cmp, gates, gate_b, *, seq, tq=256):
    b = q_cmp.shape[0]
    group = NSA_HEADS // NSA_KV_HEADS
    ncp = k_cmp.shape[1]
    ns = seq // SLC_BLOCK
    n_sel = min(N_SELECT, ns)
    c_start = np.arange(ncp)[:, None] * CMP_STRIDE
    s_start = np.arange(LANES)[None, :] * SLC_BLOCK
    overlap = np.maximum(np.minimum(c_start + CMP_BLOCK, s_start + SLC_BLOCK) - np.maximum(c_start, s_start), 0)
    overlap = np.where((np.arange(LANES)[None, :] < ns) & (np.arange(ncp)[:, None] < ncp - 1), overlap, 0)
    overlap = jnp.asarray(overlap, BF16)
    gw = group * LANES
    tq = min(tq, seq)
    parts = max(1, min(4, seq // (4 * tq)))
    seg = seq // parts
    o_parts, sel_parts = [], []
    for part in range(parts):
        q_off = part * (seg // tq)
        visible = ((part + 1) * seg - CMP_BLOCK) // CMP_STRIDE + 1
        nck = min(ncp, -(-visible // LANES) * LANES)
        o_seg, sel_seg = pl.pallas_call(
            functools.partial(_nsa_cmp_kernel, tq=tq, group=group, n_sel=n_sel, ns=ns, q_off=q_off),
            grid=(b, NSA_KV_HEADS, seg // tq),
            in_specs=[pl.BlockSpec(memory_space=pltpu.SMEM),
                      pl.BlockSpec((1, tq, gw), lambda bi, h, i, q_off=q_off: (bi, i + q_off, h)),
                      pl.BlockSpec((1, nck, LANES), lambda bi, h, i: (bi, 0, 0)),
                      pl.BlockSpec((1, nck, LANES), lambda bi, h, i: (bi, 0, 0)),
                      pl.BlockSpec((nck, LANES), lambda bi, h, i: (0, 0)),
                      pl.BlockSpec((1, tq, LANES), lambda bi, h, i, q_off=q_off: (bi, i + q_off, 0))],
            out_specs=[pl.BlockSpec((1, tq, group * HEAD_DIM), lambda bi, h, i: (bi, i, h)),
                       pl.BlockSpec((1, 1, tq, LANES), lambda bi, h, i: (bi, h, i, 0))],
            out_shape=[jax.ShapeDtypeStruct((b, seg, NSA_HEADS * HEAD_DIM), F32),
                       jax.ShapeDtypeStruct((b, NSA_KV_HEADS, seg, LANES), BF16)],
            compiler_params=_params(("arbitrary", "arbitrary", "arbitrary")),
            name="nsa_compressed_select",
        )(gate_b, q_cmp, k_cmp, v_cmp, overlap, gates)
        o_parts.append(o_seg)
        sel_parts.append(sel_seg)
    return jnp.concatenate(o_parts, axis=1), jnp.concatenate(sel_parts, axis=2)


def _nsa_slc_kernel(gb_ref, q_ref, k_ref, v_ref, sel_ref, gl_ref, o_ref, sa_ref, sb_ref, *, tq, tk, group):
    hk = pl.program_id(1)
    i = pl.program_id(2)
    q = q_ref[0]
    unsel = (sel_ref[0, 0].astype(F32) - 1.0).astype(BF16)
    lhs = [jnp.concatenate([jnp.concatenate([q[:, g * LANES:(g + 1) * LANES], unsel], axis=1)
                            for g in (2 * c, 2 * c + 1)], axis=0) for c in range(group // 2)]
    n_full = (i * tq) // tk
    chains = group // 2

    def logits_into(buf, j):
        off = pl.multiple_of(j * tk, tk)
        kt = k_ref[0, pl.ds(off, tk), :]
        for c in range(chains):
            buf[c] = _dot_nt(lhs[c], kt)

    def consume(buf, j, states, masked):
        off = pl.multiple_of(j * tk, tk)
        vt = v_ref[0, pl.ds(off, tk), :]
        out = []
        for c in range(chains):
            m, acc = states[c]
            s = buf[c]
            if masked:
                t = i * tq + lax.broadcasted_iota(jnp.int32, (tq, tk), 0)
                key = j * tk + lax.broadcasted_iota(jnp.int32, (tq, tk), 1)
                ok = key <= t
                s = jnp.where(jnp.concatenate([ok, ok], axis=0), s, NEG_INF)
            m_new = jnp.maximum(m, jnp.max(s, axis=-1, keepdims=True))
            p = jnp.exp2(s - m_new)
            out.append((m_new, jnp.exp2(m - m_new) * acc + _dot(p.astype(BF16), vt)))
        return tuple(out)

    def run(j, states, steps):
        bufs = (sa_ref, sb_ref)
        for n in range(steps):
            logits_into(bufs[(n + 1) % 2], j + n + 1)
            states = consume(bufs[n % 2], j + n, states, False)
        return states

    init = tuple((jnp.full((2 * tq, 1), NEG_INF, F32), jnp.zeros((2 * tq, LANES), F32)) for _ in range(chains))
    logits_into(sa_ref, 0)
    states = lax.fori_loop(0, n_full // 4, lambda jj, st: run(4 * jj, st, 4), init)
    r = 4 * (n_full // 4)

    def tail(left):
        def f(states):
            states = run(r, states, left)
            return consume((sa_ref, sb_ref)[left % 2], r + left, states, True)
        return f

    carry = lax.switch(n_full - r, [tail(left) for left in range(4)], states)
    gl = gl_ref[0]
    heads = []
    for g in range(group):
        acc = carry[g // 2][1][(g % 2) * tq:(g % 2 + 1) * tq]
        colg = (hk * group + g) * 3 + 1
        gate = _sigmoid(_gate_column(gl, colg) + gb_ref[colg])
        heads.append(acc * (gate / acc[:, HEAD_DIM:HEAD_DIM + 1]))
    o_ref[0] = _compact_group(heads, None)


def _nsa_selected(q_rot, k_aug, v_exp, sel, gates, gate_b, *, seq, tq=512, tk=512):
    b = q_rot.shape[0]
    group = NSA_HEADS // NSA_KV_HEADS
    tk = min(tk, seq)
    gw = group * LANES
    return pl.pallas_call(
        functools.partial(_nsa_slc_kernel, tq=tq, tk=tk, group=group),
        grid=(b, NSA_KV_HEADS, seq // tq),
        in_specs=[pl.BlockSpec(memory_space=pltpu.SMEM),
                  pl.BlockSpec((1, tq, gw), lambda bi, h, i: (bi, i, h)),
                  pl.BlockSpec((1, seq, 2 * LANES), lambda bi, h, i: (bi, 0, 0)),
                  pl.BlockSpec((1, seq, LANES), lambda bi, h, i: (bi, 0, h)),
                  pl.BlockSpec((1, 1, tq, LANES), lambda bi, h, i: (bi, h, i, 0)),
                  pl.BlockSpec((1, tq, LANES), lambda bi, h, i: (bi, i, 0))],
        out_specs=pl.BlockSpec((1, tq, group * HEAD_DIM), lambda bi, h, i: (bi, i, h)),
        out_shape=jax.ShapeDtypeStruct((b, seq, NSA_HEADS * HEAD_DIM), F32),
        scratch_shapes=[pltpu.VMEM((group // 2, 2 * tq, tk), F32), pltpu.VMEM((group // 2, 2 * tq, tk), F32)],
        compiler_params=_params(("arbitrary", "arbitrary", "arbitrary")),
        name="nsa_selected",
    )(gate_b, q_rot, k_aug, v_exp, sel, gates)


def _window_kernel(sc_ref, q_ref, k_ref, v_ref, gl_ref, o_ref, sa_ref, sb_ref, *, tq, nt, span, window, group,
                   gated, sinks):
    hk = pl.program_id(1)
    sblk = pl.program_id(2)
    chains = group // 2

    def tile(it):
        gi = sblk * nt + it
        start = pl.multiple_of(jnp.maximum(gi * tq + tq - span, 0), int(np.gcd(tq, span)))
        return gi, start, pl.ds(pl.multiple_of(it * tq, tq), tq)

    def logits_into(buf, it):
        _, start, rows = tile(it)
        kt = k_ref[0, pl.ds(start, span), :]
        for c in range(chains):
            lhs = jnp.concatenate([q_ref[0, rows, g * LANES:(g + 1) * LANES] for g in (2 * c, 2 * c + 1)], axis=0)
            buf[c] = _dot_nt(lhs, kt)

    def consume(buf, it):
        gi, start, rows = tile(it)
        vt = v_ref[0, pl.ds(start, span), :]
        t = gi * tq + lax.broadcasted_iota(jnp.int32, (tq, span), 0)
        key = start + lax.broadcasted_iota(jnp.int32, (tq, span), 1)
        ok = (key <= t) & (t - key < window)
        ok = jnp.concatenate([ok, ok], axis=0)
        gl = gl_ref[0, rows, :]
        heads = []
        for c in range(chains):
            s = jnp.where(ok, buf[c], NEG_INF)
            m = jnp.max(s, axis=-1, keepdims=True)
            if sinks:
                sk = jnp.concatenate([jnp.full((tq, 1), sc_ref[hk * group + g] * LOG2E, F32)
                                      for g in (2 * c, 2 * c + 1)], axis=0)
                m = jnp.maximum(m, sk)
            acc = _dot(jnp.exp2(s - m).astype(BF16), vt)
            denom = acc[:, HEAD_DIM:HEAD_DIM + 1]
            if sinks:
                denom = denom + jnp.exp2(sk - m)
            for r in range(2):
                g = 2 * c + r
                scale = 1.0 / denom[r * tq:(r + 1) * tq]
                if gated:
                    colg = (hk * group + g) * 3 + 2
                    scale = scale * _sigmoid(_gate_column(gl, colg) + sc_ref[colg])
                heads.append(acc[r * tq:(r + 1) * tq] * scale)
        o_ref[0, rows, :] = _compact_group(heads, None)

    logits_into(sa_ref, 0)

    def quad(jj, carry):
        bufs = (sa_ref, sb_ref)
        for n in range(4):
            it = 4 * jj + n
            logits_into(bufs[(n + 1) % 2], jnp.minimum(it + 1, nt - 1))
            consume(bufs[n % 2], it)
        return carry

    lax.fori_loop(0, nt // 4, quad, 0)


def _window_attention(q_rot, q_blk0, k_pairs, kv_blk, v_heads, scalars, gates, *, seq, window, gated, sinks,
                      tq=128, rows=2048):
    b = q_rot.shape[0]
    group = 4
    span = min(window + tq, seq)
    rows = min(rows, seq)
    nt = rows // tq
    assert nt % 4 == 0
    gw = group * LANES
    qb = q_blk0 // group
    return pl.pallas_call(
        functools.partial(_window_kernel, tq=tq, nt=nt, span=span, window=window, group=group, gated=gated,
                          sinks=sinks),
        grid=(b, 2, seq // rows),
        in_specs=[pl.BlockSpec(memory_space=pltpu.SMEM),
                  pl.BlockSpec((1, rows, gw), lambda bi, h, i: (bi, i, qb + h)),
                  pl.BlockSpec((1, seq, LANES), lambda bi, h, i: (bi, 0, kv_blk)),
                  pl.BlockSpec((1, seq, LANES), lambda bi, h, i: (bi, 0, 2 * kv_blk + h)),
                  pl.BlockSpec((1, rows, LANES), lambda bi, h, i: (bi, i, 0))],
        out_specs=pl.BlockSpec((1, rows, group * HEAD_DIM), lambda bi, h, i: (bi, i, h)),
        out_shape=jax.ShapeDtypeStruct((b, seq, 8 * HEAD_DIM), F32),
        scratch_shapes=[pltpu.VMEM((group // 2, 2 * tq, span), F32), pltpu.VMEM((group // 2, 2 * tq, span), F32)],
        compiler_params=_params(("arbitrary", "arbitrary", "arbitrary")),
        name="window_attention",
    )(scalars, q_rot, k_pairs, v_heads, gates)


def _dense_kernel(q_ref, k_ref, v_ref, o_ref, sa_ref, sb_ref, *, tq, tk, nh):
    i = pl.program_id(2)
    t0 = i * tq
    n_full = t0 // tk
    qs = [q_ref[0][:, e * LANES:(e + 1) * LANES] for e in range(nh)]

    def logits_into(buf, j):
        off = pl.multiple_of(j * tk, tk)
        for e in range(nh):
            buf[e] = _dot_nt(qs[e], k_ref[0, pl.ds(off, tk), e * LANES:(e + 1) * LANES])

    def consume(buf, j, states, masked):
        off = pl.multiple_of(j * tk, tk)
        out = []
        for e in range(nh):
            m, acc = states[e]
            s = buf[e]
            if masked:
                t = t0 + lax.broadcasted_iota(jnp.int32, (tq, tk), 0)
                key = j * tk + lax.broadcasted_iota(jnp.int32, (tq, tk), 1)
                s = jnp.where(key <= t, s, NEG_INF)
            vt = v_ref[0, pl.ds(off, tk), e * LANES:(e + 1) * LANES]
            m_new = jnp.maximum(m, jnp.max(s, axis=-1, keepdims=True))
            p = jnp.exp2(s - m_new)
            out.append((m_new, jnp.exp2(m - m_new) * acc + _dot(p.astype(BF16), vt)))
        return tuple(out)

    def run(j, states, steps):
        bufs = (sa_ref, sb_ref)
        for n in range(steps):
            logits_into(bufs[(n + 1) % 2], j + n + 1)
            states = consume(bufs[n % 2], j + n, states, False)
        return states

    init = tuple((jnp.full((tq, 1), NEG_INF, F32), jnp.zeros((tq, LANES), F32)) for _ in range(nh))
    logits_into(sa_ref, 0)
    states = lax.fori_loop(0, n_full // 4, lambda jj, st: run(4 * jj, st, 4), init)
    r = 4 * (n_full // 4)

    def tail(left):
        def f(states):
            states = run(r, states, left)
            return consume((sa_ref, sb_ref)[left % 2], r + left, states, True)
        return f

    states = lax.switch(n_full - r, [tail(left) for left in range(4)], states)
    outs = [acc * (1.0 / acc[:, HEAD_DIM:HEAD_DIM + 1]) for _, acc in states]
    lane = _lane((tq, LANES))
    o_ref[0] = jnp.concatenate([jnp.where(lane < HEAD_DIM, outs[e], pltpu.roll(outs[e + 1], HEAD_DIM, 1))
                                for e in range(0, nh, 2)], axis=1)


def _dense_attention(q, k, v, *, seq, tq=512, tk=512, nh=2):
    b = q.shape[0]
    heads = q.shape[2] // LANES
    tk = min(tk, seq)
    tq = min(tq, tk)
    return pl.pallas_call(
        functools.partial(_dense_kernel, tq=tq, tk=tk, nh=nh),
        grid=(b, heads // nh, seq // tq),
        in_specs=[pl.BlockSpec((1, tq, nh * LANES), lambda bi, p, i: (bi, i, p)),
                  pl.BlockSpec((1, seq, nh * LANES), lambda bi, p, i: (bi, 0, p)),
                  pl.BlockSpec((1, seq, nh * LANES), lambda bi, p, i: (bi, 0, p))],
        out_specs=pl.BlockSpec((1, tq, nh * HEAD_DIM), lambda bi, p, i: (bi, i, p)),
        out_shape=jax.ShapeDtypeStruct((b, seq, heads * HEAD_DIM), F32),
        scratch_shapes=[pltpu.VMEM((nh, tq, tk), F32), pltpu.VMEM((nh, tq, tk), F32)],
        compiler_params=_params(("arbitrary", "arbitrary", "arbitrary")),
        name="dense_causal_attention",
    )(q, k, v)


def _decay_kernel(f_ref, b_ref, o_ref):
    x = f_ref[0] + b_ref[...]
    lf = jnp.minimum(x, 0.0) - jnp.log1p(jnp.exp(-jnp.abs(x)))
    n = lf.shape[-1]
    lane = _lane(lf.shape)
    d = 1
    while d < n:
        lf = lf + jnp.where(lane >= d, pltpu.roll(lf, d, 1), 0.0)
        d *= 2
    o_ref[0] = lf * LOG2E


def _decay_cumsum(f_t, bias):
    b, h, s = f_t.shape
    return pl.pallas_call(
        _decay_kernel,
        grid=(b,),
        in_specs=[pl.BlockSpec((1, h, s), lambda i: (i, 0, 0)), pl.BlockSpec((h, 1), lambda i: (0, 0))],
        out_specs=pl.BlockSpec((1, h, s), lambda i: (i, 0, 0)),
        out_shape=jax.ShapeDtypeStruct((b, h, s), F32),
        compiler_params=_params(("arbitrary",)),
        name="fox_decay_cumsum",
    )(f_t, bias.reshape(h, 1).astype(F32))


def _mla_prep_kernel(cq_ref, ckv_ref, misc_ref, gqa_ref, gkva_ref, wq_ref, wk_ref, wv_ref, gq_ref, gk_ref,
                     cos_ref, sin_ref, ones_ref, q_ref, k_ref, v_ref, *, scale):
    tm = cq_ref.shape[0]
    lane = _lane((tm, LANES))
    in_rope = (lane >= MLA_NOPE_DIM) & (lane < MLA_QK_DIM)
    first = lane < MLA_NOPE_DIM + MLA_ROPE_DIM // 2
    cos, sin = cos_ref[...], sin_ref[...]

    def rope_tail(x):
        sw = jnp.where(first, pltpu.roll(x, LANES - MLA_ROPE_DIM // 2, 1), pltpu.roll(x, MLA_ROPE_DIM // 2, 1))
        return x * cos + jnp.where(in_rope, sw, 0.0) * sin

    cq = _rms(cq_ref[...], gqa_ref[...], MLA_Q_RANK).astype(BF16)
    ckv = _rms(ckv_ref[...], gkva_ref[...], MLA_KV_RANK).astype(BF16)
    qa = _dot(cq, wq_ref[...])
    ka = _dot(ckv, wk_ref[...])
    k_rope = jnp.where(in_rope, misc_ref[...], 0.0)
    def head_rms(x, gain):
        ms = _split_dot(x * x, ones_ref[...]) * (1.0 / MLA_QK_DIM)
        return (x * lax.rsqrt(ms + RMS_EPS)) * gain

    for h in range(MLA_HEADS):
        qh = head_rms(qa[:, h * LANES:(h + 1) * LANES], gq_ref[...])
        q_ref[:, h * LANES:(h + 1) * LANES] = (rope_tail(qh) * scale).astype(q_ref.dtype)
        kh = head_rms(ka[:, h * LANES:(h + 1) * LANES] + k_rope, gk_ref[...])
        k_ref[:, h * LANES:(h + 1) * LANES] = rope_tail(kh).astype(k_ref.dtype)
    v = _dot(ckv, wv_ref[...])
    ones_col = (_lane(v.shape) & (LANES - 1)) == MLA_V_DIM
    v_ref[...] = jnp.where(ones_col, 1.0, v).astype(v_ref.dtype)


def _mla_prep(c_q, c_kv, misc, q_a_norm, w_q_b, kv_a_norm, w_kv_b, q_norm, k_norm, *, seq, tm=512):
    n = c_q.shape[0]
    h = MLA_HEADS
    pad = LANES - MLA_QK_DIM
    wq = jnp.pad(w_q_b.reshape(MLA_Q_RANK, h, MLA_QK_DIM), ((0, 0), (0, 0), (0, pad)))
    wq = wq.reshape(MLA_Q_RANK, h * LANES).astype(BF16)
    wkv = w_kv_b.reshape(MLA_KV_RANK, h, MLA_NOPE_DIM + MLA_V_DIM)
    wk = jnp.pad(wkv[:, :, :MLA_NOPE_DIM], ((0, 0), (0, 0), (0, LANES - MLA_NOPE_DIM)))
    wk = wk.reshape(MLA_KV_RANK, h * LANES).astype(BF16)
    wv = jnp.pad(wkv[:, :, MLA_NOPE_DIM:], ((0, 0), (0, 0), (0, LANES - MLA_V_DIM)))
    wv = wv.reshape(MLA_KV_RANK, h * LANES).astype(BF16)
    gq = jnp.pad(q_norm, (0, pad)).reshape(1, LANES)
    gk = jnp.pad(k_norm, (0, pad)).reshape(1, LANES)
    half = MLA_ROPE_DIM // 2
    inv_freq = 1.0 / (ROPE_THETA ** (jnp.arange(0, MLA_ROPE_DIM, 2, dtype=F32) / MLA_ROPE_DIM))
    ang = jnp.arange(seq, dtype=F32)[:, None] * inv_freq[None, :]
    cos, sin = jnp.cos(ang), jnp.sin(ang)
    ones = jnp.ones((seq, MLA_NOPE_DIM), F32)
    zeros = jnp.zeros((seq, MLA_NOPE_DIM), F32)
    cos_t = jnp.concatenate([ones, cos, cos, ones[:, :pad]], axis=1)
    sin_t = jnp.concatenate([zeros, -sin, sin, zeros[:, :pad]], axis=1)
    del half
    sblocks = seq // tm
    full = lambda shp: pl.BlockSpec(shp, lambda i: (0,) * len(shp))
    q, k, v = pl.pallas_call(
        functools.partial(_mla_prep_kernel, scale=MLA_QK_DIM ** -0.5 * LOG2E),
        grid=(n // tm,),
        in_specs=[pl.BlockSpec((tm, MLA_Q_RANK), lambda i: (i, 0)),
                  pl.BlockSpec((tm, LANES), lambda i: (i, 0)),
                  pl.BlockSpec((tm, LANES), lambda i: (i, 0)),
                  full((1, MLA_Q_RANK)), full((1, MLA_KV_RANK)),
                  full((MLA_Q_RANK, h * LANES)), full((MLA_KV_RANK, h * LANES)), full((MLA_KV_RANK, h * LANES)),
                  full((1, LANES)), full((1, LANES)),
                  pl.BlockSpec((tm, LANES), lambda i: (i % sblocks, 0)),
                  pl.BlockSpec((tm, LANES), lambda i: (i % sblocks, 0)),
                  full((LANES, LANES))],
        out_specs=[pl.BlockSpec((tm, h * LANES), lambda i: (i, 0)),
                   pl.BlockSpec((tm, h * LANES), lambda i: (i, 0)),
                   pl.BlockSpec((tm, h * LANES), lambda i: (i, 0))],
        out_shape=[jax.ShapeDtypeStruct((n, h * LANES), BF16),
                   jax.ShapeDtypeStruct((n, h * LANES), BF16),
                   jax.ShapeDtypeStruct((n, h * LANES), BF16)],
        compiler_params=_params(("arbitrary",)),
        name="mla_prep",
    )(c_q, c_kv, misc, q_a_norm.reshape(1, -1), kv_a_norm.reshape(1, -1), wq, wk, wv, gq, gk, cos_t, sin_t,
      _group_sum_matrix(LANES))
    return q, k, v


def _cols(w, a, b):
    return w[:, a:b]


def _nsa_swa_mixer(x, batch, seq, mix_norm, w_in, nsa_gate_b, nsa_q_norm, nsa_kc_norm, nsa_ks_norm, nsa_kw_norm,
                   cmp_pos_k, cmp_pos_v, cmpk_w1, cmpk_w2, cmpv_w1, cmpv_w2,
                   swa_q_norm, swa_k_norm, swa_sinks, w_out):
    n = batch * seq
    o = np.cumsum([0, 512, 128, 128, 128, 128, 128, 128, 24, 512, 128, 128])
    seg = lambda j: _cols(w_in, o[j], o[j + 1])
    q_a, kc, vc, ks, vs, kw, vw, gl, q_b, k_b, v_b = [seg(j) for j in range(11)]
    gl = jnp.pad(gl, ((0, 0), (0, LANES - gl.shape[1])))
    w = jnp.concatenate([q_a, q_b, ks, kw, k_b, kc, vc, vs, vw, v_b, gl], axis=1).astype(BF16)
    cos_t, sin_t = _rope_tables(seq)

    s_q = HEAD_DIM ** -0.5 * LOG2E
    gains = jnp.stack([_pair_gain(g) for g in (nsa_q_norm, swa_q_norm, nsa_ks_norm, nsa_kw_norm, swa_k_norm)])
    jobs = [_Job(blk=c, out=0, col=2 * c, gain=c // 4, rope=True, scale=s_q, mode="q", dst=((c % 4) // 2,) * 2)
            for c in range(8)]
    jobs += [_Job(blk=c, out=1, col=2 * c, gain=0, scale=s_q, mode="q", dst=(c // 2,) * 2)
             for c in range(4)]
    jobs += [_Job(blk=8, out=2, col=0, gain=2, rope=True, mode="kaug"),
             _Job(blk=9, out=3, col=0, gain=3, rope=True), _Job(blk=10, out=3, col=1, gain=4, rope=True),
             _Job(blk=13, out=4, col=0, mode="v"),
             _Job(blk=14, out=5, col=0, mode="v"), _Job(blk=15, out=5, col=2, mode="v"),
             _Job(blk=11, out=6, col=0), _Job(blk=12, out=7, col=0), _Job(blk=16, out=8, col=0)]
    outs = [(16, BF16), (8, BF16), (2, BF16), (2, BF16), (2, BF16), (4, BF16), (1, F32), (1, F32), (1, F32)]
    q_rot, q_cmp, k_aug, k_ws, v_slc, v_ws, kc_raw, vc_raw, gates = (
        a.reshape(batch, seq, -1) for a in _proj_prep(x, mix_norm, w, jobs, outs, gains, seq, cos_t, sin_t))

    k_cmp = _compress(kc_raw, cmp_pos_k, cmpk_w1, cmpk_w2, nsa_kc_norm)
    v_cmp = _compress(vc_raw, cmp_pos_v, cmpv_w1, cmpv_w2, None)
    gate_b = nsa_gate_b.astype(F32)

    o_cmp, sel = _nsa_compressed(q_cmp, k_cmp, v_cmp, gates, gate_b, seq=seq)
    o_slc = _nsa_selected(q_rot, k_aug, v_slc, sel, gates, gate_b, seq=seq)
    o_win = _window_attention(q_rot, 0, k_ws, 0, v_ws, gate_b, gates, seq=seq, window=NSA_WINDOW,
                              gated=True, sinks=False)
    o_swa = _window_attention(q_rot, 8, k_ws, 1, v_ws, swa_sinks.astype(F32), gates, seq=seq, window=SWA_WINDOW,
                              gated=False, sinks=True, tq=256)
    flat = lambda a: a.reshape(n, -1)
    return _outproj(x, [flat(o_cmp), flat(o_slc), flat(o_win)], [flat(o_swa)], w_out)


def _fox_mla_mixer(x, batch, seq, mix_norm, w_in, fox_f_bias, fox_q_norm, fox_k_norm, mla_q_a_norm, mla_w_q_b,
                   mla_kv_a_norm, mla_w_kv_b, mla_q_norm, mla_k_norm, w_out):
    n = batch * seq
    o = np.cumsum([0, 512, 512, 512, 8, 256, 128, 32])
    seg = lambda j: _cols(w_in, o[j], o[j + 1])
    q_c, k_c, v_c, f_c, c_q, c_kv, k_r = [seg(j) for j in range(7)]
    d = w_in.shape[0]
    misc = jnp.concatenate([f_c, jnp.zeros((d, MLA_NOPE_DIM - 8), w_in.dtype), k_r,
                            jnp.zeros((d, LANES - MLA_QK_DIM), w_in.dtype)], axis=1)
    w = jnp.concatenate([q_c, k_c, v_c, c_q, c_kv, misc], axis=1).astype(BF16)
    cos_t, sin_t = _rope_tables(seq)
    gains = jnp.stack([_pair_gain(fox_q_norm), _pair_gain(fox_k_norm)])
    jobs = [_Job(blk=c, out=0, col=2 * c, gain=0, scale=HEAD_DIM ** -0.5 * LOG2E, mode="q", aug="ones")
            for c in range(4)]
    jobs += [_Job(blk=8 + c, out=1, col=2 * c, mode="v") for c in range(4)]
    jobs += [_Job(blk=4 + c, out=2, col=c) for c in range(4)]
    jobs += [_Job(blk=12, out=3, col=0), _Job(blk=13, out=3, col=1), _Job(blk=14, out=4, col=0),
             _Job(blk=15, out=5, col=0)]
    outs = [(8, BF16), (8, BF16), (4, F32), (2, F32), (1, F32), (1, F32)]
    q_f, v_f, kc_raw, c_q_out, c_kv_out, misc_out = _proj_prep(x, mix_norm, w, jobs, outs, gains, seq, cos_t, sin_t)

    f_t = misc_out[:, :FOX_HEADS].reshape(batch, seq, FOX_HEADS).transpose(0, 2, 1)
    dc = _decay_cumsum(f_t, fox_f_bias)
    dc_tok = jnp.pad(dc.transpose(0, 2, 1).reshape(n, FOX_HEADS), ((0, 0), (0, LANES - FOX_HEADS)))
    kjobs = [_Job(blk=c, out=0, col=2 * c, gain=1, mode="q", aug="decay", heads=(2 * c, 2 * c + 1))
             for c in range(4)]
    (k_f,) = _prep(kc_raw, kjobs, [(8, BF16)], gains, seq, cos_t, sin_t, aux=dc_tok)
    b3 = lambda a: a.reshape(batch, seq, -1)
    o_fox = _dense_attention(b3(q_f), b3(k_f), b3(v_f), seq=seq)

    q_m, k_m, v_m = _mla_prep(c_q_out, c_kv_out, misc_out, mla_q_a_norm, mla_w_q_b, mla_kv_a_norm, mla_w_kv_b,
                              mla_q_norm, mla_k_norm, seq=seq)
    o_mla = _dense_attention(b3(q_m), b3(k_m), b3(v_m), seq=seq)
    flat = lambda a: a.reshape(n, -1)
    return _outproj(x, [flat(o_fox)], [flat(o_mla)], w_out)


def kernel(x, l0_ffn1_norm, l0_ffn1_w_gate, l0_ffn1_w_up, l0_ffn1_w_down, l0_mix_norm, l0_w_in, l0_nsa_gate_b, l0_nsa_q_norm, l0_nsa_kc_norm, l0_nsa_ks_norm, l0_nsa_kw_norm, l0_cmp_pos_k, l0_cmp_pos_v, l0_cmpk_w1, l0_cmpk_w2, l0_cmpv_w1, l0_cmpv_w2, l0_swa_q_norm, l0_swa_k_norm, l0_swa_sinks, l0_w_out, l0_ffn2_norm, l0_ffn2_w_gate, l0_ffn2_w_up, l0_ffn2_w_down, l1_ffn1_norm, l1_ffn1_w_gate, l1_ffn1_w_up, l1_ffn1_w_down, l1_mix_norm, l1_w_in, l1_fox_f_bias, l1_fox_q_norm, l1_fox_k_norm, l1_mla_q_a_norm, l1_mla_w_q_b, l1_mla_kv_a_norm, l1_mla_w_kv_b, l1_mla_q_norm, l1_mla_k_norm, l1_w_out, l1_ffn2_norm, l1_ffn2_w_gate, l1_ffn2_w_up, l1_ffn2_w_down):
    batch, seq, d = x.shape
    h = x.reshape(batch * seq, d)
    h = _ffn(h, l0_ffn1_norm, l0_ffn1_w_gate, l0_ffn1_w_up, l0_ffn1_w_down)
    h = _nsa_swa_mixer(h, batch, seq, l0_mix_norm, l0_w_in, l0_nsa_gate_b, l0_nsa_q_norm, l0_nsa_kc_norm,
                       l0_nsa_ks_norm, l0_nsa_kw_norm, l0_cmp_pos_k, l0_cmp_pos_v, l0_cmpk_w1, l0_cmpk_w2,
                       l0_cmpv_w1, l0_cmpv_w2, l0_swa_q_norm, l0_swa_k_norm, l0_swa_sinks, l0_w_out)
    h = _ffn(h, l0_ffn2_norm, l0_ffn2_w_gate, l0_ffn2_w_up, l0_ffn2_w_down)
    h = _ffn(h, l1_ffn1_norm, l1_ffn1_w_gate, l1_ffn1_w_up, l1_ffn1_w_down)
    h = _fox_mla_mixer(h, batch, seq, l1_mix_norm, l1_w_in, l1_fox_f_bias, l1_fox_q_norm, l1_fox_k_norm,
                       l1_mla_q_a_norm, l1_mla_w_q_b, l1_mla_kv_a_norm, l1_mla_w_kv_b, l1_mla_q_norm,
                       l1_mla_k_norm, l1_w_out)
    h = _ffn(h, l1_ffn2_norm, l1_ffn2_w_gate, l1_ffn2_w_up, l1_ffn2_w_down)
    return h.reshape(batch, seq, d)
```

```python
import functools
from typing import NamedTuple, Optional

import numpy as np
import jax
import jax.numpy as jnp
from jax import lax
from jax.experimental import pallas as pl
from jax.experimental.pallas import tpu as pltpu

F32 = jnp.float32
BF16 = jnp.bfloat16

HEAD_DIM = 64
LANES = 128
ROPE_THETA = 10000.0
RMS_EPS = 1e-6
NEG_INF = -1e30
FORCE_SCORE = 1e9
BELOW_ALL = -3e38
LOG2E = 1.4426950408889634
BIG = 1e30

NSA_HEADS = 8
NSA_KV_HEADS = 2
CMP_BLOCK = 32
CMP_STRIDE = 16
CMP_HIDDEN = 256
SLC_BLOCK = 64
N_SELECT = 16
NSA_WINDOW = 512
SWA_HEADS = 8
SWA_KV_HEADS = 2
SWA_WINDOW = 128
FOX_HEADS = 8
MLA_HEADS = 8
MLA_Q_RANK = 256
MLA_KV_RANK = 128
MLA_NOPE_DIM = 64
MLA_ROPE_DIM = 32
MLA_V_DIM = 64
MLA_QK_DIM = MLA_NOPE_DIM + MLA_ROPE_DIM

VMEM_LIMIT = 48 * 1024 * 1024

NT_DIMS = (((1,), (1,)), ((), ()))


def _params(sem):
    return pltpu.CompilerParams(dimension_semantics=sem, vmem_limit_bytes=VMEM_LIMIT)


def _dot(a, b):
    return jnp.dot(a, b, preferred_element_type=F32)


def _dot_nt(a, b):
    return lax.dot_general(a, b, NT_DIMS, preferred_element_type=F32)


def _rms(x, gain, n):
    ms = jnp.sum(x * x, axis=-1, keepdims=True) * (1.0 / n)
    return (x * lax.rsqrt(ms + RMS_EPS)) * gain


def _lane(shape):
    return lax.broadcasted_iota(jnp.int32, shape, len(shape) - 1)


def _split_dot(x, m):
    hi = x.astype(BF16)
    lo = (x - hi.astype(F32)).astype(BF16)
    return _dot(hi, m) + _dot(lo, m)


def _group_sum_matrix(width):
    lane = np.arange(LANES)
    return jnp.asarray(lane[:, None] // width == lane[None, :] // width, BF16)


def _ffn_kernel(x_ref, g_ref, wg_ref, wu_ref, wd_ref, o_ref, h_sc, acc_sc, *, tf):
    x = x_ref[...]
    h_sc[...] = _rms(x, g_ref[...], x.shape[-1]).astype(BF16)
    acc_sc[...] = jnp.zeros_like(acc_sc)
    f = wg_ref.shape[1]
    for c0 in range(0, f, tf):
        cols = slice(c0, min(c0 + tf, f))
        h = h_sc[...]
        g = _dot(h, wg_ref[:, cols])
        u = _dot(h, wu_ref[:, cols])
        a = (g * (1.0 / (1.0 + jnp.exp(-g)))) * u
        acc_sc[...] += _dot(a.astype(BF16), wd_ref[cols, :])
    o_ref[...] = x + 0.5 * acc_sc[...]


def _ffn(x, norm, w_gate, w_up, w_down, *, tm=512, tf=256):
    n, d = x.shape
    f = w_gate.shape[1]
    wg, wu, wd = w_gate.astype(BF16), w_up.astype(BF16), w_down.astype(BF16)
    wspec = lambda shp: pl.BlockSpec(shp, lambda i: (0, 0), pipeline_mode=pl.Buffered(1))
    return pl.pallas_call(
        functools.partial(_ffn_kernel, tf=tf),
        grid=(n // tm,),
        in_specs=[pl.BlockSpec((tm, d), lambda i: (i, 0)),
                  pl.BlockSpec((1, d), lambda i: (0, 0)),
                  wspec((d, f)), wspec((d, f)), wspec((f, d))],
        out_specs=pl.BlockSpec((tm, d), lambda i: (i, 0)),
        out_shape=jax.ShapeDtypeStruct((n, d), F32),
        scratch_shapes=[pltpu.VMEM((tm, d), BF16), pltpu.VMEM((tm, d), F32)],
        compiler_params=_params(("arbitrary",)),
        name="ffn",
    )(x, norm.reshape(1, d), wg, wu, wd)


def _outproj_kernel(*refs, n_a, n_b):
    x_ref = refs[0]
    a_refs = refs[1:1 + n_a]
    b_refs = refs[1 + n_a:1 + n_a + n_b]
    wa_ref, wb_ref, o_ref = refs[1 + n_a + n_b:]
    a = a_refs[0][...]
    for r in a_refs[1:]:
        a = a + r[...]
    b = b_refs[0][...]
    for r in b_refs[1:]:
        b = b + r[...]
    o_ref[...] = x_ref[...] + _dot(a.astype(BF16), wa_ref[...]) + _dot(b.astype(BF16), wb_ref[...])


def _outproj(x, a_list, b_list, w_out, *, tm=512):
    n, d = x.shape
    ca = a_list[0].shape[1]
    cb = b_list[0].shape[1]
    wa = w_out[:ca].astype(BF16)
    wb = w_out[ca:].astype(BF16)
    row = lambda c: pl.BlockSpec((tm, c), lambda i: (i, 0))
    return pl.pallas_call(
        functools.partial(_outproj_kernel, n_a=len(a_list), n_b=len(b_list)),
        grid=(n // tm,),
        in_specs=[row(d)] + [row(ca)] * len(a_list) + [row(cb)] * len(b_list)
                 + [pl.BlockSpec((ca, d), lambda i: (0, 0)), pl.BlockSpec((cb, d), lambda i: (0, 0))],
        out_specs=row(d),
        out_shape=jax.ShapeDtypeStruct((n, d), F32),
        compiler_params=_params(("arbitrary",)),
        name="outproj",
    )(x, *a_list, *b_list, wa, wb)


class _Job(NamedTuple):
    blk: int
    out: int
    col: int
    gain: Optional[int] = None
    rope: bool = False
    scale: float = 1.0
    mode: str = "plain"
    dst: tuple = (0, 0)
    aug: Optional[str] = None
    heads: tuple = (0, 0)


def _prep_kernel(y_ref, gain_ref, cos_ref, sin_ref, aux_ref, hsum_ref, *o_refs, jobs, seq):
    tm = y_ref.shape[0]
    lane = _lane((tm, LANES))
    low = lane < HEAD_DIM
    zero = jnp.zeros((tm, LANES), F32)
    for job in jobs:
        x = y_ref[:, job.blk * LANES:(job.blk + 1) * LANES]
        if job.gain is not None:
            ms = _split_dot(x * x, hsum_ref[...]) * (1.0 / HEAD_DIM)
            x = (x * lax.rsqrt(ms + RMS_EPS)) * gain_ref[job.gain]
        if job.rope:
            swapped = jnp.where((lane & (HEAD_DIM - 1)) < HEAD_DIM // 2,
                                pltpu.roll(x, LANES - HEAD_DIM // 2, 1), pltpu.roll(x, HEAD_DIM // 2, 1))
            x = x * cos_ref[...] + swapped * sin_ref[...]
        if job.scale != 1.0:
            x = x * job.scale
        if job.mode == "plain":
            pieces = [x]
        elif job.mode == "kaug":
            pos = (pl.program_id(0) * tm + lax.broadcasted_iota(jnp.int32, (tm, LANES), 0)) % seq
            pieces = [x, jnp.where(lane == pos // SLC_BLOCK, BIG, 0.0)]
        elif job.mode == "v":
            r = pltpu.roll(x, HEAD_DIM, 1)
            tail = jnp.where(lane == HEAD_DIM, 1.0, 0.0)
            pieces = [jnp.where(low, x, tail), jnp.where(low, r, tail)]
        else:
            r = pltpu.roll(x, HEAD_DIM, 1)
            h_even = jnp.where(low, x, zero) if job.dst[0] == 0 else jnp.where(low, zero, r)
            h_odd = jnp.where(low, r, zero) if job.dst[1] == 0 else jnp.where(low, zero, x)
            if job.aug is not None:
                tails = []
                for e in range(2):
                    if job.aug == "ones":
                        tails.append(jnp.where((lane >= HEAD_DIM) & (lane < HEAD_DIM + 3), 1.0, 0.0))
                    else:
                        d = jnp.sum(jnp.where(lane == job.heads[e], aux_ref[...], 0.0), axis=-1, keepdims=True)
                        hi = d.astype(BF16).astype(F32)
                        mid = (d - hi).astype(BF16).astype(F32)
                        lo = d - hi - mid
                        tails.append(jnp.where(lane == HEAD_DIM, -hi, jnp.where(lane == HEAD_DIM + 1, -mid,
                                     jnp.where(lane == HEAD_DIM + 2, -lo, 0.0))))
                h_even = jnp.where(low, h_even, tails[0])
                h_odd = jnp.where(low, h_odd, tails[1])
            pieces = [h_even, h_odd]
        o_ref = o_refs[job.out]
        for n, piece in enumerate(pieces):
            o_ref[:, (job.col + n) * LANES:(job.col + n + 1) * LANES] = piece.astype(o_ref.dtype)


def _prep(y, jobs, outs, gains, seq, cos_t, sin_t, *, aux=None, tm=512):
    n, c = y.shape
    aux_spec = pl.BlockSpec((tm, LANES), lambda i: (i, 0))
    if aux is None:
        aux, aux_spec = jnp.zeros((tm, LANES), F32), pl.BlockSpec((tm, LANES), lambda i: (0, 0))
    sblocks = seq // tm
    return pl.pallas_call(
        functools.partial(_prep_kernel, jobs=tuple(jobs), seq=seq),
        grid=(n // tm,),
        in_specs=[pl.BlockSpec((tm, c), lambda i: (i, 0)),
                  pl.BlockSpec(gains.shape, lambda i: (0, 0, 0)),
                  pl.BlockSpec((tm, LANES), lambda i: (i % sblocks, 0)),
                  pl.BlockSpec((tm, LANES), lambda i: (i % sblocks, 0)),
                  aux_spec,
                  pl.BlockSpec((LANES, LANES), lambda i: (0, 0))],
        out_specs=[pl.BlockSpec((tm, w * LANES), lambda i: (i, 0)) for w, _ in outs],
        out_shape=[jax.ShapeDtypeStruct((n, w * LANES), dt) for w, dt in outs],
        compiler_params=_params(("arbitrary",)),
        name="head_prep",
    )(y, gains, cos_t, sin_t, aux, _group_sum_matrix(HEAD_DIM))


def _proj_prep_kernel(x_ref, g_ref, w_ref, gain_ref, cos_ref, sin_ref, aux_ref, hsum_ref, *rest, jobs, seq):
    *o_refs, y_sc = rest
    x = x_ref[...]
    y_sc[...] = _dot(_rms(x, g_ref[...], x.shape[-1]).astype(BF16), w_ref[...])
    _prep_kernel(y_sc, gain_ref, cos_ref, sin_ref, aux_ref, hsum_ref, *o_refs, jobs=jobs, seq=seq)


def _proj_prep(x, norm, w, jobs, outs, gains, seq, cos_t, sin_t, *, tm=512):
    n, d = x.shape
    c = w.shape[1]
    sblocks = seq // tm
    return pl.pallas_call(
        functools.partial(_proj_prep_kernel, jobs=tuple(jobs), seq=seq),
        grid=(n // tm,),
        in_specs=[pl.BlockSpec((tm, d), lambda i: (i, 0)),
                  pl.BlockSpec((1, d), lambda i: (0, 0)),
                  pl.BlockSpec((d, c), lambda i: (0, 0), pipeline_mode=pl.Buffered(1)),
                  pl.BlockSpec(gains.shape, lambda i: (0, 0, 0)),
                  pl.BlockSpec((tm, LANES), lambda i: (i % sblocks, 0)),
                  pl.BlockSpec((tm, LANES), lambda i: (i % sblocks, 0)),
                  pl.BlockSpec((tm, LANES), lambda i: (0, 0)),
                  pl.BlockSpec((LANES, LANES), lambda i: (0, 0))],
        out_specs=[pl.BlockSpec((tm, wd * LANES), lambda i: (i, 0)) for wd, _ in outs],
        out_shape=[jax.ShapeDtypeStruct((n, wd * LANES), dt) for wd, dt in outs],
        scratch_shapes=[pltpu.VMEM((tm, c), F32)],
        compiler_params=_params(("arbitrary",)),
        name="proj_prep",
    )(x, norm.reshape(1, d), w, gains, cos_t, sin_t, jnp.zeros((tm, LANES), F32), _group_sum_matrix(HEAD_DIM))


def _pair_gain(g):
    return jnp.concatenate([g, g]).reshape(1, LANES).astype(F32)


def _rope_tables(seq):
    inv_freq = 1.0 / (ROPE_THETA ** (jnp.arange(0, HEAD_DIM, 2, dtype=F32) / HEAD_DIM))
    ang = jnp.arange(seq, dtype=F32)[:, None] * inv_freq[None, :]
    cos, sin = jnp.cos(ang), jnp.sin(ang)
    cos_t = jnp.concatenate([cos, cos, cos, cos], axis=1)
    sin_t = jnp.concatenate([-sin, sin, -sin, sin], axis=1)
    return cos_t, sin_t


def _compress_kernel(ch_ref, ptop_ref, pbot_ref, w1t_ref, w1b_ref, w2_ref, gain_ref, o_ref, *, norm):
    ch = ch_ref[0]
    a = _dot((ch + ptop_ref[...]).astype(BF16), w1t_ref[...])
    b = _dot((ch + pbot_ref[...]).astype(BF16), w1b_ref[...])
    nc = a.shape[0]
    hid = a + pltpu.roll(b, nc - 1, 0)
    act = hid * (1.0 / (1.0 + jnp.exp(-hid)))
    out = _dot(act.astype(BF16), w2_ref[...])
    if norm:
        lane = _lane(out.shape)
        low = lane < HEAD_DIM
        o2 = out * out
        s_lo = jnp.sum(jnp.where(low, o2, 0.0), axis=-1, keepdims=True)
        s_hi = jnp.sum(jnp.where(low, 0.0, o2), axis=-1, keepdims=True)
        ms = jnp.where(low, s_lo, s_hi) * (1.0 / HEAD_DIM)
        out = (out * lax.rsqrt(ms + RMS_EPS)) * gain_ref[...]
    o_ref[0] = out.astype(o_ref.dtype)


def _compress(t_pair, pos_emb, w1, w2, gain):
    b, s, _ = t_pair.shape
    nc = s // CMP_STRIDE
    hid = w1.shape[1]
    ch = t_pair.reshape(b, nc, CMP_STRIDE * LANES)
    eye2 = jnp.eye(2, dtype=F32)
    w1r = w1.reshape(CMP_BLOCK, HEAD_DIM, hid)
    def expand_w1(w):
        return jnp.einsum('pdj,kl->pkdlj', w, eye2).reshape(CMP_STRIDE * LANES, 2 * hid).astype(BF16)
    w1t, w1b = expand_w1(w1r[:CMP_STRIDE]), expand_w1(w1r[CMP_STRIDE:])
    w2e = jnp.einsum('jd,kl->kjld', w2, eye2).reshape(2 * hid, LANES).astype(BF16)
    def expand_pos(p):
        return jnp.broadcast_to(p[:, None, :], (CMP_STRIDE, 2, HEAD_DIM)).reshape(1, CMP_STRIDE * LANES)
    ptop, pbot = expand_pos(pos_emb[:CMP_STRIDE]), expand_pos(pos_emb[CMP_STRIDE:])
    norm = gain is not None
    g = _pair_gain(gain) if norm else jnp.ones((1, LANES), F32)
    full = lambda shp: pl.BlockSpec(shp, lambda i: (0,) * len(shp))
    return pl.pallas_call(
        functools.partial(_compress_kernel, norm=norm),
        grid=(b,),
        in_specs=[pl.BlockSpec((1, nc, CMP_STRIDE * LANES), lambda i: (i, 0, 0)),
                  full((1, CMP_STRIDE * LANES)), full((1, CMP_STRIDE * LANES)),
                  full((CMP_STRIDE * LANES, 2 * hid)), full((CMP_STRIDE * LANES, 2 * hid)),
                  full((2 * hid, LANES)), full((1, LANES))],
        out_specs=pl.BlockSpec((1, nc, LANES), lambda i: (i, 0, 0)),
        out_shape=jax.ShapeDtypeStruct((b, nc, LANES), BF16),
        compiler_params=_params(("arbitrary",)),
        name="nsa_compress",
    )(ch, ptop, pbot, w1t, w1b, w2e, g)


def _stack_heads(q, n):
    return jnp.concatenate([q[:, g * LANES:(g + 1) * LANES] for g in range(n)], axis=0)


def _gate_column(gl, col):
    lane = _lane(gl.shape)
    return jnp.sum(jnp.where(lane == col, gl, 0.0), axis=-1, keepdims=True)


def _sigmoid(x):
    return 1.0 / (1.0 + jnp.exp(-x))


def _compact_group(heads, hk):
    tq = heads[0].shape[0]
    lane = _lane((tq, LANES))
    low = lane < HEAD_DIM
    outs = []
    for e in range(0, len(heads), 2):
        he, ho = heads[e], heads[e + 1]
        ho_r = pltpu.roll(ho, HEAD_DIM, 1)
        if hk is None:
            lo_part, hi_part = he, ho_r
        else:
            at_low = jnp.broadcast_to(hk, (tq, LANES)) == 0
            lo_part = jnp.where(at_low, he, pltpu.roll(he, HEAD_DIM, 1))
            hi_part = jnp.where(at_low, ho_r, ho)
        outs.append(jnp.where(low, lo_part, hi_part))
    return jnp.concatenate(outs, axis=1)


def _nsa_cmp_kernel(gb_ref, q_ref, k_ref, v_ref, ov_ref, gl_ref, o_ref, sel_ref, *, tq, group, n_sel, ns, q_off):
    hk = pl.program_id(1)
    i = pl.program_id(2) + q_off
    q4 = _stack_heads(q_ref[0], group)
    kc = k_ref[0]
    ncp = kc.shape[0]
    logits = _dot_nt(q4, kc).reshape(group, tq, ncp)
    t = i * tq + lax.broadcasted_iota(jnp.int32, (tq, ncp), 0)
    cmp_end = lax.broadcasted_iota(jnp.int32, (tq, ncp), 1) * CMP_STRIDE + (CMP_BLOCK - 1)
    logits = jnp.where((cmp_end <= t)[None], logits, NEG_INF)
    m = jnp.max(logits, axis=-1, keepdims=True)
    e = jnp.exp2(logits - m)
    t_row = i * tq + lax.broadcasted_iota(jnp.int32, (tq, 1), 0)
    seen = jnp.where(t_row >= CMP_BLOCK - 1, 1.0, 0.0)[None]
    inv = seen / jnp.maximum(jnp.sum(e, axis=-1, keepdims=True), 1e-30)
    p = e * inv
    o4 = _dot(p.reshape(group * tq, ncp).astype(BF16), v_ref[0])
    gl = gl_ref[0]
    heads = []
    for g in range(group):
        col = (hk * group + g) * 3
        gate = _sigmoid(_gate_column(gl, col) + gb_ref[col])
        heads.append(o4[g * tq:(g + 1) * tq] * gate)
    o_ref[0] = _compact_group(heads, hk)

    ps = jnp.sum(p, axis=0)
    ps_hi = ps.astype(BF16)
    ps_lo = (ps - ps_hi.astype(F32)).astype(BF16)
    imp = _dot(ps_hi, ov_ref[...]) + _dot(ps_lo, ov_ref[...])
    imp_t = imp.T
    blk = lax.broadcasted_iota(jnp.int32, (LANES, tq), 0)
    cur = (i * tq + lax.broadcasted_iota(jnp.int32, (LANES, tq), 1)) // SLC_BLOCK
    forced = (blk == 0) | (blk == cur) | (blk == cur - 1)
    score = jnp.where(forced, BELOW_ALL, jnp.where(blk <= cur, imp_t, NEG_INF))
    score = jnp.where(blk < ns, score, BELOW_ALL)
    blk_f = blk.astype(F32)

    def pick(_, carry):
        sc, sel = carry
        mx = jnp.max(sc, axis=0, keepdims=True)
        first = jnp.min(jnp.where(sc == mx, blk_f, float(LANES)), axis=0, keepdims=True)
        hit = blk_f == first
        return jnp.where(hit, BELOW_ALL, sc), jnp.where(hit, 1.0, sel)

    _, sel = lax.fori_loop(0, max(n_sel - 3, 0), pick, (score, jnp.where(forced, 1.0, 0.0)))
    sel_ref[0, 0] = jnp.where(blk <= cur, sel, 0.0).T.astype(sel_ref.dtype)


def _nsa_compressed(q_cmp, k_cmp, v_cmp, gates, gate_b, *, seq, tq=256):
    b = q_cmp.shape[0]
    group = NSA_HEADS // NSA_KV_HEADS
    ncp = k_cmp.shape[1]
    ns = seq // SLC_BLOCK
    n_sel = min(N_SELECT, ns)
    c_start = np.arange(ncp)[:, None] * CMP_STRIDE
    s_start = np.arange(LANES)[None, :] * SLC_BLOCK
    overlap = np.maximum(np.minimum(c_start + CMP_BLOCK, s_start + SLC_BLOCK) - np.maximum(c_start, s_start), 0)
    overlap = np.where((np.arange(LANES)[None, :] < ns) & (np.arange(ncp)[:, None] < ncp - 1), overlap, 0)
    overlap = jnp.asarray(overlap, BF16)
    gw = group * LANES
    tq = min(tq, seq)
    parts = max(1, min(4, seq // (4 * tq)))
    seg = seq // parts
    o_parts, sel_parts = [], []
    for part in range(parts):
        q_off = part * (seg // tq)
        visible = ((part + 1) * seg - CMP_BLOCK) // CMP_STRIDE + 1
        nck = min(ncp, -(-visible // LANES) * LANES)
        o_seg, sel_seg = pl.pallas_call(
            functools.partial(_nsa_cmp_kernel, tq=tq, group=group, n_sel=n_sel, ns=ns, q_off=q_off),
            grid=(b, NSA_KV_HEADS, seg // tq),
            in_specs=[pl.BlockSpec(memory_space=pltpu.SMEM),
                      pl.BlockSpec((1, tq, gw), lambda bi, h, i, q_off=q_off: (bi, i + q_off, h)),
                      pl.BlockSpec((1, nck, LANES), lambda bi, h, i: (bi, 0, 0)),
                      pl.BlockSpec((1, nck, LANES), lambda bi, h, i: (bi, 0, 0)),
                      pl.BlockSpec((nck, LANES), lambda bi, h, i: (0, 0)),
                      pl.BlockSpec((1, tq, LANES), lambda bi, h, i, q_off=q_off: (bi, i + q_off, 0))],
            out_specs=[pl.BlockSpec((1, tq, group * HEAD_DIM), lambda bi, h, i: (bi, i, h)),
                       pl.BlockSpec((1, 1, tq, LANES), lambda bi, h, i: (bi, h, i, 0))],
            out_shape=[jax.ShapeDtypeStruct((b, seg, NSA_HEADS * HEAD_DIM), F32),
                       jax.ShapeDtypeStruct((b, NSA_KV_HEADS, seg, LANES), BF16)],
            compiler_params=_params(("arbitrary", "arbitrary", "arbitrary")),
            name="nsa_compressed_select",
        )(gate_b, q_cmp, k_cmp, v_cmp, overlap, gates)
        o_parts.append(o_seg)
        sel_parts.append(sel_seg)
    return jnp.concatenate(o_parts, axis=1), jnp.concatenate(sel_parts, axis=2)


def _nsa_slc_kernel(gb_ref, q_ref, k_ref, v_ref, sel_ref, gl_ref, o_ref, sa_ref, sb_ref, *, tq, tk, group):
    hk = pl.program_id(1)
    i = pl.program_id(2)
    q = q_ref[0]
    unsel = (sel_ref[0, 0].astype(F32) - 1.0).astype(BF16)
    lhs = [jnp.concatenate([jnp.concatenate([q[:, g * LANES:(g + 1) * LANES], unsel], axis=1)
                            for g in (2 * c, 2 * c + 1)], axis=0) for c in range(group // 2)]
    n_full = (i * tq) // tk
    chains = group // 2

    def logits_into(buf, j):
        off = pl.multiple_of(j * tk, tk)
        kt = k_ref[0, pl.ds(off, tk), :]
        for c in range(chains):
            buf[c] = _dot_nt(lhs[c], kt)

    def consume(buf, j, states, masked):
        off = pl.multiple_of(j * tk, tk)
        vt = v_ref[0, pl.ds(off, tk), :]
        out = []
        for c in range(chains):
            m, acc = states[c]
            s = buf[c]
            if masked:
                t = i * tq + lax.broadcasted_iota(jnp.int32, (tq, tk), 0)
                key = j * tk + lax.broadcasted_iota(jnp.int32, (tq, tk), 1)
                ok = key <= t
                s = jnp.where(jnp.concatenate([ok, ok], axis=0), s, NEG_INF)
            m_new = jnp.maximum(m, jnp.max(s, axis=-1, keepdims=True))
            p = jnp.exp2(s - m_new)
            out.append((m_new, jnp.exp2(m - m_new) * acc + _dot(p.astype(BF16), vt)))
        return tuple(out)

    def run(j, states, steps):
        bufs = (sa_ref, sb_ref)
        for n in range(steps):
            logits_into(bufs[(n + 1) % 2], j + n + 1)
            states = consume(bufs[n % 2], j + n, states, False)
        return states

    init = tuple((jnp.full((2 * tq, 1), NEG_INF, F32), jnp.zeros((2 * tq, LANES), F32)) for _ in range(chains))
    logits_into(sa_ref, 0)
    states = lax.fori_loop(0, n_full // 4, lambda jj, st: run(4 * jj, st, 4), init)
    r = 4 * (n_full // 4)

    def tail(left):
        def f(states):
            states = run(r, states, left)
            return consume((sa_ref, sb_ref)[left % 2], r + left, states, True)
        return f

    carry = lax.switch(n_full - r, [tail(left) for left in range(4)], states)
    gl = gl_ref[0]
    heads = []
    for g in range(group):
        acc = carry[g // 2][1][(g % 2) * tq:(g % 2 + 1) * tq]
        colg = (hk * group + g) * 3 + 1
        gate = _sigmoid(_gate_column(gl, colg) + gb_ref[colg])
        heads.append(acc * (gate / acc[:, HEAD_DIM:HEAD_DIM + 1]))
    o_ref[0] = _compact_group(heads, None)


def _nsa_selected(q_rot, k_aug, v_exp, sel, gates, gate_b, *, seq, tq=512, tk=512):
    b = q_rot.shape[0]
    group = NSA_HEADS // NSA_KV_HEADS
    tk = min(tk, seq)
    gw = group * LANES
    return pl.pallas_call(
        functools.partial(_nsa_slc_kernel, tq=tq, tk=tk, group=group),
        grid=(b, NSA_KV_HEADS, seq // tq),
        in_specs=[pl.BlockSpec(memory_space=pltpu.SMEM),
                  pl.BlockSpec((1, tq, gw), lambda bi, h, i: (bi, i, h)),
                  pl.BlockSpec((1, seq, 2 * LANES), lambda bi, h, i: (bi, 0, 0)),
                  pl.BlockSpec((1, seq, LANES), lambda bi, h, i: (bi, 0, h)),
                  pl.BlockSpec((1, 1, tq, LANES), lambda bi, h, i: (bi, h, i, 0)),
                  pl.BlockSpec((1, tq, LANES), lambda bi, h, i: (bi, i, 0))],
        out_specs=pl.BlockSpec((1, tq, group * HEAD_DIM), lambda bi, h, i: (bi, i, h)),
        out_shape=jax.ShapeDtypeStruct((b, seq, NSA_HEADS * HEAD_DIM), F32),
        scratch_shapes=[pltpu.VMEM((group // 2, 2 * tq, tk), F32), pltpu.VMEM((group // 2, 2 * tq, tk), F32)],
        compiler_params=_params(("arbitrary", "arbitrary", "arbitrary")),
        name="nsa_selected",
    )(gate_b, q_rot, k_aug, v_exp, sel, gates)


def _window_kernel(sc_ref, q_ref, k_ref, v_ref, gl_ref, o_ref, sa_ref, sb_ref, *, tq, nt, span, window, group,
                   gated, sinks):
    hk = pl.program_id(1)
    sblk = pl.program_id(2)
    chains = group // 2

    def tile(it):
        gi = sblk * nt + it
        start = pl.multiple_of(jnp.maximum(gi * tq + tq - span, 0), int(np.gcd(tq, span)))
        return gi, start, pl.ds(pl.multiple_of(it * tq, tq), tq)

    def logits_into(buf, it):
        _, start, rows = tile(it)
        kt = k_ref[0, pl.ds(start, span), :]
        for c in range(chains):
            lhs = jnp.concatenate([q_ref[0, rows, g * LANES:(g + 1) * LANES] for g in (2 * c, 2 * c + 1)], axis=0)
            buf[c] = _dot_nt(lhs, kt)

    def consume(buf, it):
        gi, start, rows = tile(it)
        vt = v_ref[0, pl.ds(start, span), :]
        t = gi * tq + lax.broadcasted_iota(jnp.int32, (tq, span), 0)
        key = start + lax.broadcasted_iota(jnp.int32, (tq, span), 1)
        ok = (key <= t) & (t - key < window)
        ok = jnp.concatenate([ok, ok], axis=0)
        gl = gl_ref[0, rows, :]
        heads = []
        for c in range(chains):
            s = jnp.where(ok, buf[c], NEG_INF)
            m = jnp.max(s, axis=-1, keepdims=True)
            if sinks:
                sk = jnp.concatenate([jnp.full((tq, 1), sc_ref[hk * group + g] * LOG2E, F32)
                                      for g in (2 * c, 2 * c + 1)], axis=0)
                m = jnp.maximum(m, sk)
            acc = _dot(jnp.exp2(s - m).astype(BF16), vt)
            denom = acc[:, HEAD_DIM:HEAD_DIM + 1]
            if sinks:
                denom = denom + jnp.exp2(sk - m)
            for r in range(2):
                g = 2 * c + r
                scale = 1.0 / denom[r * tq:(r + 1) * tq]
                if gated:
                    colg = (hk * group + g) * 3 + 2
                    scale = scale * _sigmoid(_gate_column(gl, colg) + sc_ref[colg])
                heads.append(acc[r * tq:(r + 1) * tq] * scale)
        o_ref[0, rows, :] = _compact_group(heads, None)

    logits_into(sa_ref, 0)

    def quad(jj, carry):
        bufs = (sa_ref, sb_ref)
        for n in range(4):
            it = 4 * jj + n
            logits_into(bufs[(n + 1) % 2], jnp.minimum(it + 1, nt - 1))
            consume(bufs[n % 2], it)
        return carry

    lax.fori_loop(0, nt // 4, quad, 0)


def _window_attention(q_rot, q_blk0, k_pairs, kv_blk, v_heads, scalars, gates, *, seq, window, gated, sinks,
                      tq=128, rows=4096):
    b = q_rot.shape[0]
    group = 4
    span = min(window + tq, seq)
    rows = min(rows, seq)
    nt = rows // tq
    assert nt % 4 == 0
    gw = group * LANES
    qb = q_blk0 // group
    return pl.pallas_call(
        functools.partial(_window_kernel, tq=tq, nt=nt, span=span, window=window, group=group, gated=gated,
                          sinks=sinks),
        grid=(b, 2, seq // rows),
        in_specs=[pl.BlockSpec(memory_space=pltpu.SMEM),
                  pl.BlockSpec((1, rows, gw), lambda bi, h, i: (bi, i, qb + h)),
                  pl.BlockSpec((1, seq, LANES), lambda bi, h, i: (bi, 0, kv_blk)),
                  pl.BlockSpec((1, seq, LANES), lambda bi, h, i: (bi, 0, 2 * kv_blk + h)),
                  pl.BlockSpec((1, rows, LANES), lambda bi, h, i: (bi, i, 0))],
        out_specs=pl.BlockSpec((1, rows, group * HEAD_DIM), lambda bi, h, i: (bi, i, h)),
        out_shape=jax.ShapeDtypeStruct((b, seq, 8 * HEAD_DIM), F32),
        scratch_shapes=[pltpu.VMEM((group // 2, 2 * tq, span), F32), pltpu.VMEM((group // 2, 2 * tq, span), F32)],
        compiler_params=_params(("arbitrary", "arbitrary", "arbitrary")),
        name="window_attention",
    )(scalars, q_rot, k_pairs, v_heads, gates)


def _dense_kernel(q_ref, k_ref, v_ref, o_ref, sa_ref, sb_ref, *, tq, tk, nh):
    i = pl.program_id(2)
    t0 = i * tq
    n_full = t0 // tk
    qs = [q_ref[0][:, e * LANES:(e + 1) * LANES] for e in range(nh)]

    def logits_into(buf, j):
        off = pl.multiple_of(j * tk, tk)
        for e in range(nh):
            buf[e] = _dot_nt(qs[e], k_ref[0, pl.ds(off, tk), e * LANES:(e + 1) * LANES])

    def consume(buf, j, states, masked):
        off = pl.multiple_of(j * tk, tk)
        out = []
        for e in range(nh):
            m, acc = states[e]
            s = buf[e]
            if masked:
                t = t0 + lax.broadcasted_iota(jnp.int32, (tq, tk), 0)
                key = j * tk + lax.broadcasted_iota(jnp.int32, (tq, tk), 1)
                s = jnp.where(key <= t, s, NEG_INF)
            vt = v_ref[0, pl.ds(off, tk), e * LANES:(e + 1) * LANES]
            m_new = jnp.maximum(m, jnp.max(s, axis=-1, keepdims=True))
            p = jnp.exp2(s - m_new)
            out.append((m_new, jnp.exp2(m - m_new) * acc + _dot(p.astype(BF16), vt)))
        return tuple(out)

    def run(j, states, steps):
        bufs = (sa_ref, sb_ref)
        for n in range(steps):
            logits_into(bufs[(n + 1) % 2], j + n + 1)
            states = consume(bufs[n % 2], j + n, states, False)
        return states

    init = tuple((jnp.full((tq, 1), NEG_INF, F32), jnp.zeros((tq, LANES), F32)) for _ in range(nh))
    logits_into(sa_ref, 0)
    states = lax.fori_loop(0, n_full // 4, lambda jj, st: run(4 * jj, st, 4), init)
    r = 4 * (n_full // 4)

    def tail(left):
        def f(states):
            states = run(r, states, left)
            return consume((sa_ref, sb_ref)[left % 2], r + left, states, True)
        return f

    states = lax.switch(n_full - r, [tail(left) for left in range(4)], states)
    outs = [acc * (1.0 / acc[:, HEAD_DIM:HEAD_DIM + 1]) for _, acc in states]
    lane = _lane((tq, LANES))
    o_ref[0] = jnp.concatenate([jnp.where(lane < HEAD_DIM, outs[e], pltpu.roll(outs[e + 1], HEAD_DIM, 1))
                                for e in range(0, nh, 2)], axis=1)


def _dense_attention(q, k, v, *, seq, tq=512, tk=512, nh=2):
    b = q.shape[0]
    heads = q.shape[2] // LANES
    tk = min(tk, seq)
    tq = min(tq, tk)
    return pl.pallas_call(
        functools.partial(_dense_kernel, tq=tq, tk=tk, nh=nh),
        grid=(b, heads // nh, seq // tq),
        in_specs=[pl.BlockSpec((1, tq, nh * LANES), lambda bi, p, i: (bi, i, p)),
                  pl.BlockSpec((1, seq, nh * LANES), lambda bi, p, i: (bi, 0, p)),
                  pl.BlockSpec((1, seq, nh * LANES), lambda bi, p, i: (bi, 0, p))],
        out_specs=pl.BlockSpec((1, tq, nh * HEAD_DIM), lambda bi, p, i: (bi, i, p)),
        out_shape=jax.ShapeDtypeStruct((b, seq, heads * HEAD_DIM), F32),
        scratch_shapes=[pltpu.VMEM((nh, tq, tk), F32), pltpu.VMEM((nh, tq, tk), F32)],
        compiler_params=_params(("arbitrary", "arbitrary", "arbitrary")),
        name="dense_causal_attention",
    )(q, k, v)


def _decay_kernel(f_ref, b_ref, o_ref):
    x = f_ref[0] + b_ref[...]
    lf = jnp.minimum(x, 0.0) - jnp.log1p(jnp.exp(-jnp.abs(x)))
    n = lf.shape[-1]
    lane = _lane(lf.shape)
    d = 1
    while d < n:
        lf = lf + jnp.where(lane >= d, pltpu.roll(lf, d, 1), 0.0)
        d *= 2
    o_ref[0] = lf * LOG2E


def _decay_cumsum(f_t, bias):
    b, h, s = f_t.shape
    return pl.pallas_call(
        _decay_kernel,
        grid=(b,),
        in_specs=[pl.BlockSpec((1, h, s), lambda i: (i, 0, 0)), pl.BlockSpec((h, 1), lambda i: (0, 0))],
        out_specs=pl.BlockSpec((1, h, s), lambda i: (i, 0, 0)),
        out_shape=jax.ShapeDtypeStruct((b, h, s), F32),
        compiler_params=_params(("arbitrary",)),
        name="fox_decay_cumsum",
    )(f_t, bias.reshape(h, 1).astype(F32))


def _mla_prep_kernel(cq_ref, ckv_ref, misc_ref, gqa_ref, gkva_ref, wq_ref, wk_ref, wv_ref, gq_ref, gk_ref,
                     cos_ref, sin_ref, ones_ref, q_ref, k_ref, v_ref, *, scale):
    tm = cq_ref.shape[0]
    lane = _lane((tm, LANES))
    in_rope = (lane >= MLA_NOPE_DIM) & (lane < MLA_QK_DIM)
    first = lane < MLA_NOPE_DIM + MLA_ROPE_DIM // 2
    cos, sin = cos_ref[...], sin_ref[...]

    def rope_tail(x):
        sw = jnp.where(first, pltpu.roll(x, LANES - MLA_ROPE_DIM // 2, 1), pltpu.roll(x, MLA_ROPE_DIM // 2, 1))
        return x * cos + jnp.where(in_rope, sw, 0.0) * sin

    cq = _rms(cq_ref[...], gqa_ref[...], MLA_Q_RANK).astype(BF16)
    ckv = _rms(ckv_ref[...], gkva_ref[...], MLA_KV_RANK).astype(BF16)
    qa = _dot(cq, wq_ref[...])
    ka = _dot(ckv, wk_ref[...])
    k_rope = jnp.where(in_rope, misc_ref[...], 0.0)
    def head_rms(x, gain):
        ms = _split_dot(x * x, ones_ref[...]) * (1.0 / MLA_QK_DIM)
        return (x * lax.rsqrt(ms + RMS_EPS)) * gain

    for h in range(MLA_HEADS):
        qh = head_rms(qa[:, h * LANES:(h + 1) * LANES], gq_ref[...])
        q_ref[:, h * LANES:(h + 1) * LANES] = (rope_tail(qh) * scale).astype(q_ref.dtype)
        kh = head_rms(ka[:, h * LANES:(h + 1) * LANES] + k_rope, gk_ref[...])
        k_ref[:, h * LANES:(h + 1) * LANES] = rope_tail(kh).astype(k_ref.dtype)
    v = _dot(ckv, wv_ref[...])
    ones_col = (_lane(v.shape) & (LANES - 1)) == MLA_V_DIM
    v_ref[...] = jnp.where(ones_col, 1.0, v).astype(v_ref.dtype)


def _mla_prep(c_q, c_kv, misc, q_a_norm, w_q_b, kv_a_norm, w_kv_b, q_norm, k_norm, *, seq, tm=512):
    n = c_q.shape[0]
    h = MLA_HEADS
    pad = LANES - MLA_QK_DIM
    wq = jnp.pad(w_q_b.reshape(MLA_Q_RANK, h, MLA_QK_DIM), ((0, 0), (0, 0), (0, pad)))
    wq = wq.reshape(MLA_Q_RANK, h * LANES).astype(BF16)
    wkv = w_kv_b.reshape(MLA_KV_RANK, h, MLA_NOPE_DIM + MLA_V_DIM)
    wk = jnp.pad(wkv[:, :, :MLA_NOPE_DIM], ((0, 0), (0, 0), (0, LANES - MLA_NOPE_DIM)))
    wk = wk.reshape(MLA_KV_RANK, h * LANES).astype(BF16)
    wv = jnp.pad(wkv[:, :, MLA_NOPE_DIM:], ((0, 0), (0, 0), (0, LANES - MLA_V_DIM)))
    wv = wv.reshape(MLA_KV_RANK, h * LANES).astype(BF16)
    gq = jnp.pad(q_norm, (0, pad)).reshape(1, LANES)
    gk = jnp.pad(k_norm, (0, pad)).reshape(1, LANES)
    inv_freq = 1.0 / (ROPE_THETA ** (jnp.arange(0, MLA_ROPE_DIM, 2, dtype=F32) / MLA_ROPE_DIM))
    ang = jnp.arange(seq, dtype=F32)[:, None] * inv_freq[None, :]
    cos, sin = jnp.cos(ang), jnp.sin(ang)
    ones = jnp.ones((seq, MLA_NOPE_DIM), F32)
    zeros = jnp.zeros((seq, MLA_NOPE_DIM), F32)
    cos_t = jnp.concatenate([ones, cos, cos, ones[:, :pad]], axis=1)
    sin_t = jnp.concatenate([zeros, -sin, sin, zeros[:, :pad]], axis=1)
    sblocks = seq // tm
    full = lambda shp: pl.BlockSpec(shp, lambda i: (0,) * len(shp))
    q, k, v = pl.pallas_call(
        functools.partial(_mla_prep_kernel, scale=MLA_QK_DIM ** -0.5 * LOG2E),
        grid=(n // tm,),
        in_specs=[pl.BlockSpec((tm, MLA_Q_RANK), lambda i: (i, 0)),
                  pl.BlockSpec((tm, LANES), lambda i: (i, 0)),
                  pl.BlockSpec((tm, LANES), lambda i: (i, 0)),
                  full((1, MLA_Q_RANK)), full((1, MLA_KV_RANK)),
                  full((MLA_Q_RANK, h * LANES)), full((MLA_KV_RANK, h * LANES)), full((MLA_KV_RANK, h * LANES)),
                  full((1, LANES)), full((1, LANES)),
                  pl.BlockSpec((tm, LANES), lambda i: (i % sblocks, 0)),
                  pl.BlockSpec((tm, LANES), lambda i: (i % sblocks, 0)),
                  full((LANES, LANES))],
        out_specs=[pl.BlockSpec((tm, h * LANES), lambda i: (i, 0)),
                   pl.BlockSpec((tm, h * LANES), lambda i: (i, 0)),
                   pl.BlockSpec((tm, h * LANES), lambda i: (i, 0))],
        out_shape=[jax.ShapeDtypeStruct((n, h * LANES), BF16),
                   jax.ShapeDtypeStruct((n, h * LANES), BF16),
                   jax.ShapeDtypeStruct((n, h * LANES), BF16)],
        compiler_params=_params(("arbitrary",)),
        name="mla_prep",
    )(c_q, c_kv, misc, q_a_norm.reshape(1, -1), kv_a_norm.reshape(1, -1), wq, wk, wv, gq, gk, cos_t, sin_t,
      _group_sum_matrix(LANES))
    return q, k, v


def _cols(w, a, b):
    return w[:, a:b]


def _nsa_swa_mixer(x, batch, seq, mix_norm, w_in, nsa_gate_b, nsa_q_norm, nsa_kc_norm, nsa_ks_norm, nsa_kw_norm,
                   cmp_pos_k, cmp_pos_v, cmpk_w1, cmpk_w2, cmpv_w1, cmpv_w2,
                   swa_q_norm, swa_k_norm, swa_sinks, w_out):
    n = batch * seq
    o = np.cumsum([0, 512, 128, 128, 128, 128, 128, 128, 24, 512, 128, 128])
    seg = lambda j: _cols(w_in, o[j], o[j + 1])
    q_a, kc, vc, ks, vs, kw, vw, gl, q_b, k_b, v_b = [seg(j) for j in range(11)]
    gl = jnp.pad(gl, ((0, 0), (0, LANES - gl.shape[1])))
    w = jnp.concatenate([q_a, q_b, ks, kw, k_b, kc, vc, vs, vw, v_b, gl], axis=1).astype(BF16)
    cos_t, sin_t = _rope_tables(seq)

    s_q = HEAD_DIM ** -0.5 * LOG2E
    gains = jnp.stack([_pair_gain(g) for g in (nsa_q_norm, swa_q_norm, nsa_ks_norm, nsa_kw_norm, swa_k_norm)])
    jobs = [_Job(blk=c, out=0, col=2 * c, gain=c // 4, rope=True, scale=s_q, mode="q", dst=((c % 4) // 2,) * 2)
            for c in range(8)]
    jobs += [_Job(blk=c, out=1, col=2 * c, gain=0, scale=s_q, mode="q", dst=(c // 2,) * 2)
             for c in range(4)]
    jobs += [_Job(blk=8, out=2, col=0, gain=2, rope=True, mode="kaug"),
             _Job(blk=9, out=3, col=0, gain=3, rope=True), _Job(blk=10, out=3, col=1, gain=4, rope=True),
             _Job(blk=13, out=4, col=0, mode="v"),
             _Job(blk=14, out=5, col=0, mode="v"), _Job(blk=15, out=5, col=2, mode="v"),
             _Job(blk=11, out=6, col=0), _Job(blk=12, out=7, col=0), _Job(blk=16, out=8, col=0)]
    outs = [(16, BF16), (8, BF16), (2, BF16), (2, BF16), (2, BF16), (4, BF16), (1, F32), (1, F32), (1, F32)]
    q_rot, q_cmp, k_aug, k_ws, v_slc, v_ws, kc_raw, vc_raw, gates = (
        a.reshape(batch, seq, -1) for a in _proj_prep(x, mix_norm, w, jobs, outs, gains, seq, cos_t, sin_t))

    k_cmp = _compress(kc_raw, cmp_pos_k, cmpk_w1, cmpk_w2, nsa_kc_norm)
    v_cmp = _compress(vc_raw, cmp_pos_v, cmpv_w1, cmpv_w2, None)
    gate_b = nsa_gate_b.astype(F32)

    o_cmp, sel = _nsa_compressed(q_cmp, k_cmp, v_cmp, gates, gate_b, seq=seq)
    o_slc = _nsa_selected(q_rot, k_aug, v_slc, sel, gates, gate_b, seq=seq)
    o_win = _window_attention(q_rot, 0, k_ws, 0, v_ws, gate_b, gates, seq=seq, window=NSA_WINDOW,
                              gated=True, sinks=False)
    o_swa = _window_attention(q_rot, 8, k_ws, 1, v_ws, swa_sinks.astype(F32), gates, seq=seq, window=SWA_WINDOW,
                              gated=False, sinks=True, tq=256)
    flat = lambda a: a.reshape(n, -1)
    return _outproj(x, [flat(o_cmp), flat(o_slc), flat(o_win)], [flat(o_swa)], w_out)


def _fox_mla_mixer(x, batch, seq, mix_norm, w_in, fox_f_bias, fox_q_norm, fox_k_norm, mla_q_a_norm, mla_w_q_b,
                   mla_kv_a_norm, mla_w_kv_b, mla_q_norm, mla_k_norm, w_out):
    n = batch * seq
    o = np.cumsum([0, 512, 512, 512, 8, 256, 128, 32])
    seg = lambda j: _cols(w_in, o[j], o[j + 1])
    q_c, k_c, v_c, f_c, c_q, c_kv, k_r = [seg(j) for j in range(7)]
    d = w_in.shape[0]
    misc = jnp.concatenate([f_c, jnp.zeros((d, MLA_NOPE_DIM - 8), w_in.dtype), k_r,
                            jnp.zeros((d, LANES - MLA_QK_DIM), w_in.dtype)], axis=1)
    w = jnp.concatenate([q_c, k_c, v_c, c_q, c_kv, misc], axis=1).astype(BF16)
    cos_t, sin_t = _rope_tables(seq)
    gains = jnp.stack([_pair_gain(fox_q_norm), _pair_gain(fox_k_norm)])
    jobs = [_Job(blk=c, out=0, col=2 * c, gain=0, scale=HEAD_DIM ** -0.5 * LOG2E, mode="q", aug="ones")
            for c in range(4)]
    jobs += [_Job(blk=8 + c, out=1, col=2 * c, mode="v") for c in range(4)]
    jobs += [_Job(blk=4 + c, out=2, col=c) for c in range(4)]
    jobs += [_Job(blk=12, out=3, col=0), _Job(blk=13, out=3, col=1), _Job(blk=14, out=4, col=0),
             _Job(blk=15, out=5, col=0)]
    outs = [(8, BF16), (8, BF16), (4, F32), (2, F32), (1, F32), (1, F32)]
    q_f, v_f, kc_raw, c_q_out, c_kv_out, misc_out = _proj_prep(x, mix_norm, w, jobs, outs, gains, seq, cos_t, sin_t)

    f_t = misc_out[:, :FOX_HEADS].reshape(batch, seq, FOX_HEADS).transpose(0, 2, 1)
    dc = _decay_cumsum(f_t, fox_f_bias)
    dc_tok = jnp.pad(dc.transpose(0, 2, 1).reshape(n, FOX_HEADS), ((0, 0), (0, LANES - FOX_HEADS)))
    kjobs = [_Job(blk=c, out=0, col=2 * c, gain=1, mode="q", aug="decay", heads=(2 * c, 2 * c + 1))
             for c in range(4)]
    (k_f,) = _prep(kc_raw, kjobs, [(8, BF16)], gains, seq, cos_t, sin_t, aux=dc_tok)
    b3 = lambda a: a.reshape(batch, seq, -1)
    o_fox = _dense_attention(b3(q_f), b3(k_f), b3(v_f), seq=seq)

    q_m, k_m, v_m = _mla_prep(c_q_out, c_kv_out, misc_out, mla_q_a_norm, mla_w_q_b, mla_kv_a_norm, mla_w_kv_b,
                              mla_q_norm, mla_k_norm, seq=seq)
    o_mla = _dense_attention(b3(q_m), b3(k_m), b3(v_m), seq=seq)
    flat = lambda a: a.reshape(n, -1)
    return _outproj(x, [flat(o_fox)], [flat(o_mla)], w_out)


def kernel(x, l0_ffn1_norm, l0_ffn1_w_gate, l0_ffn1_w_up, l0_ffn1_w_down, l0_mix_norm, l0_w_in, l0_nsa_gate_b, l0_nsa_q_norm, l0_nsa_kc_norm, l0_nsa_ks_norm, l0_nsa_kw_norm, l0_cmp_pos_k, l0_cmp_pos_v, l0_cmpk_w1, l0_cmpk_w2, l0_cmpv_w1, l0_cmpv_w2, l0_swa_q_norm, l0_swa_k_norm, l0_swa_sinks, l0_w_out, l0_ffn2_norm, l0_ffn2_w_gate, l0_ffn2_w_up, l0_ffn2_w_down, l1_ffn1_norm, l1_ffn1_w_gate, l1_ffn1_w_up, l1_ffn1_w_down, l1_mix_norm, l1_w_in, l1_fox_f_bias, l1_fox_q_norm, l1_fox_k_norm, l1_mla_q_a_norm, l1_mla_w_q_b, l1_mla_kv_a_norm, l1_mla_w_kv_b, l1_mla_q_norm, l1_mla_k_norm, l1_w_out, l1_ffn2_norm, l1_ffn2_w_gate, l1_ffn2_w_up, l1_ffn2_w_down):
    batch, seq, d = x.shape
    h = x.reshape(batch * seq, d)
    h = _ffn(h, l0_ffn1_norm, l0_ffn1_w_gate, l0_ffn1_w_up, l0_ffn1_w_down)
    h = _nsa_swa_mixer(h, batch, seq, l0_mix_norm, l0_w_in, l0_nsa_gate_b, l0_nsa_q_norm, l0_nsa_kc_norm,
                       l0_nsa_ks_norm, l0_nsa_kw_norm, l0_cmp_pos_k, l0_cmp_pos_v, l0_cmpk_w1, l0_cmpk_w2,
                       l0_cmpv_w1, l0_cmpv_w2, l0_swa_q_norm, l0_swa_k_norm, l0_swa_sinks, l0_w_out)
    h = _ffn(h, l0_ffn2_norm, l0_ffn2_w_gate, l0_ffn2_w_up, l0_ffn2_w_down)
    h = _ffn(h, l1_ffn1_norm, l1_ffn1_w_gate, l1_ffn1_w_up, l1_ffn1_w_down)
    h = _fox_mla_mixer(h, batch, seq, l1_mix_norm, l1_w_in, l1_fox_f_bias, l1_fox_q_norm, l1_fox_k_norm,
                       l1_mla_q_a_norm, l1_mla_w_q_b, l1_mla_kv_a_norm, l1_mla_w_kv_b, l1_mla_q_norm,
                       l1_mla_k_norm, l1_w_out)
    h = _ffn(h, l1_ffn2_norm, l1_ffn2_w_gate, l1_ffn2_w_up, l1_ffn2_w_down)
    return h.reshape(batch, seq, d)
```
